```python
import jax, jax.numpy as jnp
from jax import lax
import numpy as np


D_MODEL = 1024
BATCH = 2
SEQ = 8192
DEPTH = 2
DEC_BATCH = 16
DEC_SEQ = 16
PAST_LEN = 2048

CHUNK = 64
N_A_LAYERS = DEPTH // 2
N_B_LAYERS = DEPTH - N_A_LAYERS
GMLP_BLOCK = 128
A_WIDTH = 2 * D_MODEL
A_HEADS = 8
A_HEAD_DIM = A_WIDTH // A_HEADS
N_HEADS = 8
HEAD_DIM = D_MODEL // N_HEADS
Q_BLOCK = 128
N_GROUPS = 4
EXPERTS_PER_GROUP = 4
TOP_K_INNER = 2
N_EXPERTS = N_GROUPS * EXPERTS_PER_GROUP
EXPERT_DIM = D_MODEL // 2
EPS = 1e-6

kernel_name = 'yoco_gmlp_stickbreak_hmoe_step'


def rms_norm(x, g):
    xf = x.astype(jnp.float32)
    y = xf * lax.rsqrt(jnp.mean(xf * xf, axis=-1, keepdims=True) + EPS)
    return (y * g.astype(jnp.float32)).astype(x.dtype)


def gmlp_mixer(xn, w_in, v_norm_g, w_s, b_s, w_out):
    bsz, length, _ = xn.shape
    z = jax.nn.gelu(xn @ w_in)
    u, v = jnp.split(z, 2, axis=-1)
    v = rms_norm(v, v_norm_g)
    blk = min(length, GMLP_BLOCK)
    pos = jnp.arange(GMLP_BLOCK)
    mask = (pos[None, :] // CHUNK) <= (pos[:, None] // CHUNK)
    w = jnp.where(mask, w_s, 0.0)[:, :blk, :blk]
    b = b_s[:, :blk]
    vb = v.reshape(bsz, length // blk, blk, A_HEADS, A_HEAD_DIM)
    gate = jnp.einsum('hts,bnshd->bnthd', w, vb) + b.T[None, None, :, :, None]
    y = (u * gate.reshape(bsz, length, A_WIDTH)) @ w_out
    return y, v


def stick_breaking(q, k, v, q_pos, k_pos):
    z = jnp.einsum('bthd,bshd->bhts', q, k).astype(jnp.float32) * (HEAD_DIM ** -0.5)
    visible = k_pos[None, :] < q_pos[:, None]
    log_keep = jnp.where(visible, jax.nn.log_sigmoid(-z), 0.0)
    log_between = lax.cumsum(log_keep, axis=3, reverse=True) - log_keep
    weight = jnp.where(visible, jnp.exp(jax.nn.log_sigmoid(z) + log_between), 0.0)
    return jnp.einsum('bhts,bshd->bthd', weight.astype(v.dtype), v)


def stick_breaking_blocked(q, k, v, pos):
    bsz, length = q.shape[:2]
    nblk = length // Q_BLOCK
    qb = q.reshape(bsz, nblk, Q_BLOCK, N_HEADS, HEAD_DIM).transpose(1, 0, 2, 3, 4)
    pb = pos.reshape(nblk, Q_BLOCK)
    ob = lax.map(lambda a: stick_breaking(a[0], k, v, a[1], pos), (qb, pb))
    return ob.transpose(1, 0, 2, 3, 4).reshape(bsz, length, N_HEADS, HEAD_DIM)


def shared_kv(h, kv_norm_g, w_kv, k_norm_g):
    bsz, length, _ = h.shape
    kv = rms_norm(h, kv_norm_g) @ w_kv
    k, v = jnp.split(kv, 2, axis=-1)
    k = rms_norm(k.reshape(bsz, length, N_HEADS, HEAD_DIM), k_norm_g)
    v = v.reshape(bsz, length, N_HEADS, HEAD_DIM)
    return k, v


def hier_moe(h, norm_g, wg, bg, we, be, w_gate, w_up, w_down):
    bsz, length, _ = h.shape
    x = rms_norm(h, norm_g).reshape(bsz * length, D_MODEL)
    n_tok = x.shape[0]
    logit_g = (x @ wg).astype(jnp.float32) + bg.astype(jnp.float32)
    g_idx = jnp.argmax(logit_g, axis=-1)
    p_g = jnp.take_along_axis(jax.nn.softmax(logit_g, axis=-1), g_idx[:, None], axis=-1)
    logit_e_all = jnp.einsum('td,gde->tge', x, we).astype(jnp.float32) + be.astype(jnp.float32)
    logit_e = logit_e_all[jnp.arange(n_tok), g_idx]
    top_val, top_idx = lax.top_k(logit_e, TOP_K_INNER)
    gate = p_g * jax.nn.softmax(top_val, axis=-1)
    expert = g_idx[:, None] * EXPERTS_PER_GROUP + top_idx
    combine = jnp.sum(jax.nn.one_hot(expert, N_EXPERTS, dtype=jnp.float32) * gate[..., None], axis=1)
    hidden = jax.nn.silu(jnp.einsum('td,edf->tef', x, w_gate)) * jnp.einsum('td,edf->tef', x, w_up)
    y = jnp.einsum('tef,efd->td', hidden * combine[..., None].astype(hidden.dtype), w_down)
    return y.reshape(bsz, length, D_MODEL)


def setup_inputs(seed: int = 0) -> dict:
    key = jax.random.key(seed)
    ks = jax.random.split(key, 26)
    f32 = jnp.float32

    def nrm(k, shape, scale):
        return jax.random.normal(k, shape, f32) * scale

    def gain(k, shape):
        return 1.0 + 0.02 * jax.random.normal(k, shape, f32)

    return {
        'x_prompt': nrm(ks[0], (BATCH, SEQ, D_MODEL), 1.0),
        'x_sample': nrm(ks[1], (DEC_BATCH, DEC_SEQ, D_MODEL), 1.0),
        'cache_k': nrm(ks[2], (DEC_BATCH, PAST_LEN, N_HEADS, HEAD_DIM), 1.0),
        'cache_v': nrm(ks[3], (DEC_BATCH, PAST_LEN, N_HEADS, HEAD_DIM), 1.0),
        'a_norm_g': gain(ks[4], (N_A_LAYERS, D_MODEL)),
        'a_w_in': nrm(ks[5], (N_A_LAYERS, D_MODEL, 2 * A_WIDTH), D_MODEL ** -0.5),
        'a_v_norm_g': gain(ks[6], (N_A_LAYERS, A_WIDTH)),
        'a_w_s': nrm(ks[7], (N_A_LAYERS, A_HEADS, GMLP_BLOCK, GMLP_BLOCK), GMLP_BLOCK ** -0.5),
        'a_b_s': 1.0 + 0.1 * jax.random.normal(ks[8], (N_A_LAYERS, A_HEADS, GMLP_BLOCK), f32),
        'a_w_out': nrm(ks[9], (N_A_LAYERS, A_WIDTH, D_MODEL), A_WIDTH ** -0.5),
        'kv_norm_g': gain(ks[10], (D_MODEL,)),
        'w_kv': nrm(ks[11], (D_MODEL, 2 * N_HEADS * HEAD_DIM), D_MODEL ** -0.5),
        'k_norm_g': gain(ks[12], (HEAD_DIM,)),
        'b_norm_g': gain(ks[13], (N_B_LAYERS, D_MODEL)),
        'b_w_q': nrm(ks[14], (N_B_LAYERS, D_MODEL, N_HEADS * HEAD_DIM), D_MODEL ** -0.5),
        'q_norm_g': gain(ks[15], (N_B_LAYERS, HEAD_DIM)),
        'b_w_o': nrm(ks[16], (N_B_LAYERS, N_HEADS * HEAD_DIM, D_MODEL), (N_HEADS * HEAD_DIM) ** -0.5),
        'ffn_norm_g': gain(ks[17], (DEPTH, D_MODEL)),
        'router_g_w': nrm(ks[18], (DEPTH, D_MODEL, N_GROUPS), D_MODEL ** -0.5),
        'router_g_b': nrm(ks[19], (DEPTH, N_GROUPS), 0.01),
        'router_e_w': nrm(ks[20], (DEPTH, N_GROUPS, D_MODEL, EXPERTS_PER_GROUP), D_MODEL ** -0.5),
        'router_e_b': nrm(ks[21], (DEPTH, N_GROUPS, EXPERTS_PER_GROUP), 0.01),
        'moe_w_gate': nrm(ks[22], (DEPTH, N_EXPERTS, D_MODEL, EXPERT_DIM), D_MODEL ** -0.5),
        'moe_w_up': nrm(ks[23], (DEPTH, N_EXPERTS, D_MODEL, EXPERT_DIM), D_MODEL ** -0.5),
        'moe_w_down': nrm(ks[24], (DEPTH, N_EXPERTS, EXPERT_DIM, D_MODEL), EXPERT_DIM ** -0.5),
    }


def reference(x_prompt, x_sample, cache_k, cache_v,
              a_norm_g, a_w_in, a_v_norm_g, a_w_s, a_b_s, a_w_out,
              kv_norm_g, w_kv, k_norm_g,
              b_norm_g, b_w_q, q_norm_g, b_w_o,
              ffn_norm_g, router_g_w, router_g_b, router_e_w, router_e_b,
              moe_w_gate, moe_w_up, moe_w_down):

    def trunk(x, past_k, past_v):
        bsz, length, _ = x.shape
        past = 0 if past_k is None else past_k.shape[1]
        q_pos = past + jnp.arange(length)
        h = x
        v_rows = []
        k_new = v_new = keys = vals = k_pos = None
        for layer in range(DEPTH):
            if layer < N_A_LAYERS:
                i = layer
                mix, v_row = gmlp_mixer(rms_norm(h, a_norm_g[i]), a_w_in[i], a_v_norm_g[i],
                                        a_w_s[i], a_b_s[i], a_w_out[i])
                v_rows.append(v_row)
            else:
                if layer == N_A_LAYERS:
                    k_new, v_new = shared_kv(h, kv_norm_g, w_kv, k_norm_g)
                    if past_k is None:
                        keys, vals = k_new, v_new
                    else:
                        keys = jnp.concatenate([past_k, k_new], axis=1)
                        vals = jnp.concatenate([past_v, v_new], axis=1)
                    k_pos = jnp.arange(past + length)
                j = layer - N_A_LAYERS
                q = (rms_norm(h, b_norm_g[j]) @ b_w_q[j]).reshape(bsz, length, N_HEADS, HEAD_DIM)
                q = rms_norm(q, q_norm_g[j])
                if past_k is None:
                    o = stick_breaking_blocked(q, keys, vals, q_pos)
                else:
                    o = stick_breaking(q, keys, vals, q_pos, k_pos)
                mix = o.reshape(bsz, length, N_HEADS * HEAD_DIM) @ b_w_o[j]
            h = h + mix
            h = h + hier_moe(h, ffn_norm_g[layer], router_g_w[layer], router_g_b[layer],
                             router_e_w[layer], router_e_b[layer],
                             moe_w_gate[layer], moe_w_up[layer], moe_w_down[layer])
        return h, k_new, v_new, v_rows

    y_prompt, k_prompt, v_prompt, _ = trunk(x_prompt, None, None)
    y_sample, k_sample, v_sample, v_rows_sample = trunk(x_sample, cache_k, cache_v)
    gmlp_v_sample = jnp.stack(v_rows_sample, axis=0)
    return (y_prompt, y_sample, k_prompt, v_prompt, k_sample, v_sample, gmlp_v_sample)
```

```python
import functools
import math

import jax
import jax.numpy as jnp
from jax import lax
from jax.experimental import pallas as pl
from jax.experimental.pallas import tpu as pltpu

F32 = jnp.float32
BF16 = jnp.bfloat16
I32 = jnp.int32

D_MODEL = 1024
CHUNK = 64
GMLP_BLOCK = 128
A_WIDTH = 2 * D_MODEL
A_HEADS = 8
A_HEAD_DIM = A_WIDTH // A_HEADS
N_HEADS = 8
HEAD_DIM = D_MODEL // N_HEADS
N_GROUPS = 4
EXPERTS_PER_GROUP = 4
EXPERT_DIM = D_MODEL // 2
EPS = 1e-6

LANES = 128
PAIRS = ((0, 1), (0, 2), (0, 3), (1, 2), (1, 3), (2, 3))
N_BUCKETS = N_GROUPS * len(PAIRS)
GATE_COLS = LANES
ROW_W = D_MODEL + GATE_COLS

TOKEN_BLOCK = GMLP_BLOCK
EXPERT_TILE = 256
KVQ_TILE = 256
ATTN_TILE = 256
CACHE_CHUNK = 512
DMA_CHUNK = 128
VMEM_LIMIT = 56 * 1024 * 1024
EXP_UNDERFLOW = 104.0


def _cparams(*sem):
    return pltpu.CompilerParams(dimension_semantics=sem, vmem_limit_bytes=VMEM_LIMIT)


def _rms(x, g):
    ms = jnp.mean(x * x, axis=-1, keepdims=True)
    return x * lax.rsqrt(ms + EPS) * g


def _first_argmax(vals):
    m = vals[0]
    for v in vals[1:]:
        m = jnp.maximum(m, v)
    idx = jnp.full(m.shape, len(vals) - 1, I32)
    for r in range(len(vals) - 2, -1, -1):
        idx = jnp.where(vals[r] == m, r, idx)
    return m, idx


def _route(xn, wrt_ref, rb_ref, xs_ref, bucket_ref):
    tb = xn.shape[0]
    assert tb == GATE_COLS
    lt = lax.dot_general(wrt_ref[...], xn, (((1,), (1,)), ((), ())),
                         precision=lax.Precision.HIGHEST, preferred_element_type=F32) + rb_ref[...]
    lg = [lt[r:r + 1, :] for r in range(N_GROUPS)]
    m_g, g_idx = _first_argmax(lg)
    den = jnp.exp(lg[0] - m_g)
    for r in range(1, N_GROUPS):
        den = den + jnp.exp(lg[r] - m_g)
    p_g = 1.0 / den
    le = []
    for e in range(EXPERTS_PER_GROUP):
        v = lt[8 + 4 * (N_GROUPS - 1) + e:8 + 4 * (N_GROUPS - 1) + e + 1, :]
        for g in range(N_GROUPS - 2, -1, -1):
            v = jnp.where(g_idx == g, lt[8 + 4 * g + e:8 + 4 * g + e + 1, :], v)
        le.append(v)
    m1, i1 = _first_argmax(le)
    m2, i2 = _first_argmax([jnp.where(i1 == e, -jnp.inf, le[e]) for e in range(EXPERTS_PER_GROUP)])
    e2 = jnp.exp(m2 - m1)
    g1 = p_g * (1.0 / (1.0 + e2))
    g2 = p_g * (e2 / (1.0 + e2))
    first_low = i1 < i2
    lo = jnp.minimum(i1, i2)
    hi = jnp.maximum(i1, i2)
    g_lo = jnp.where(first_low, g1, g2)
    g_hi = jnp.where(first_low, g2, g1)
    pair = jnp.where(lo == 0, hi - 1, jnp.where(lo == 1, hi + 1, 5))
    bucket_ref[0] = g_idx * len(PAIRS) + pair
    rowi = lax.broadcasted_iota(I32, (tb, tb), 0)
    gsq = jnp.where(rowi == 0, g_lo, jnp.where(rowi == 1, g_hi, 0.0))
    xs_ref[:, :D_MODEL] = xn
    xs_ref[:, D_MODEL:] = gsq.T


def _a_layer_kernel(xp_ref, xsm_ref, ag_ref, win_ref, vg_ref, ws_ref, bs_ref, wout_ref, fg_ref, wrt_ref, rb_ref,
                    h_ref, xs_ref, bucket_ref, vrow_ref, *, npb):
    i = pl.program_id(0)
    x = jnp.where(i < npb, xp_ref[...], xsm_ref[...])
    xn = _rms(x, ag_ref[...]).astype(BF16)
    z = jax.nn.gelu(jnp.dot(xn, win_ref[...], preferred_element_type=F32))
    u = z[:, :A_WIDTH]
    v = _rms(z[:, A_WIDTH:], vg_ref[...])

    @pl.when(i >= npb)
    def _():
        vrow_ref[...] = v

    vb = v.astype(BF16)
    gate = jnp.concatenate(
        [jnp.dot(ws_ref[0, h], vb[:, h * A_HEAD_DIM:(h + 1) * A_HEAD_DIM], preferred_element_type=F32)
         for h in range(A_HEADS)], axis=1) + bs_ref[0]
    s = (u * gate).astype(BF16)
    h1 = x + jnp.dot(s, wout_ref[...], preferred_element_type=F32)
    h_ref[...] = h1
    _route(_rms(h1, fg_ref[...]), wrt_ref, rb_ref, xs_ref, bucket_ref)


def _oproj_kernel(op_ref, osm_ref, h_in_ref, wo_ref, fg_ref, wrt_ref, rb_ref, h_ref, xs_ref, bucket_ref, *, npb):
    i = pl.program_id(0)
    o = jnp.where(i < npb, op_ref[...], osm_ref[...])
    h3 = h_in_ref[...] + jnp.dot(o, wo_ref[...], preferred_element_type=F32)
    h_ref[...] = h3
    _route(_rms(h3, fg_ref[...]), wrt_ref, rb_ref, xs_ref, bucket_ref)


def _rank_kernel(b_ref, pos_ref, tb_ref, *, tm):
    bk = b_ref[...]
    nr = bk.shape[0]
    r_i = lax.broadcasted_iota(I32, (LANES, LANES), 0)
    c_i = lax.broadcasted_iota(I32, (LANES, LANES), 1)
    upper = (r_i <= c_i).astype(BF16)
    rr = lax.broadcasted_iota(I32, (nr, nr), 0)
    cc = lax.broadcasted_iota(I32, (nr, nr), 1)
    before_rows = (cc < rr).astype(BF16)
    tile_start = lax.broadcasted_iota(I32, (1, LANES), 1).astype(F32) * tm
    pos = jnp.zeros((nr, LANES), F32)
    seg_start = jnp.zeros((1, LANES), F32)
    tile_bucket = jnp.zeros((1, LANES), I32)
    for b in range(N_BUCKETS):
        m = bk == b
        pref = jnp.dot(jnp.where(m, 1.0, 0.0).astype(BF16), upper, preferred_element_type=F32)
        rowtot = jnp.broadcast_to(pref[:, LANES - 1:LANES], (nr, LANES))
        before = jnp.dot(before_rows, rowtot.astype(BF16), preferred_element_type=F32)
        cnt = jnp.sum(rowtot, axis=0, keepdims=True)
        pos = pos + jnp.where(m, seg_start + before + pref - 1.0, 0.0)
        seg_start = seg_start + jnp.ceil(cnt / tm) * tm
        tile_bucket = tile_bucket + (seg_start <= tile_start).astype(I32)
    pos_ref[...] = pos.astype(I32)
    tb_ref[...] = tile_bucket


def _permute_kernel(idx_ref, src_ref, *rest, n, scatter):
    out_ref, sem = rest[-2:]
    nchunks = n // DMA_CHUNK

    def copy(t, slot):
        j = idx_ref[t]
        s, d = (t, j) if scatter else (j, t)
        return pltpu.make_async_copy(src_ref.at[pl.ds(s, 1)], out_ref.at[pl.ds(d, 1)], sem.at[slot])

    def issue(c):
        def body(k, carry):
            copy(c * DMA_CHUNK + k, c % 2).start()
            return carry
        lax.fori_loop(0, DMA_CHUNK, body, 0)

    def drain(c):
        def body(k, carry):
            copy(c * DMA_CHUNK + k, c % 2).wait()
            return carry
        lax.fori_loop(0, DMA_CHUNK, body, 0)

    issue(0)

    def step(c, carry):
        issue(c)
        drain(c - 1)
        return carry

    lax.fori_loop(1, nchunks, step, 0)
    drain(nchunks - 1)


def _expert_kernel(telo_ref, tehi_ref, tsrc_ref, tval_ref, xs_ref, wgl_ref, wul_ref, wdl_ref, wgh_ref, wuh_ref,
                   wdh_ref, y_ref):
    del telo_ref, tehi_ref, tsrc_ref
    r = pl.program_id(0)

    @pl.when(tval_ref[r] == 1)
    def _():
        x = xs_ref[:, :D_MODEL].astype(BF16)

        def ffn(wg_ref, wu_ref, wd_ref, g):
            a = jnp.dot(x, wg_ref[0, 0].astype(BF16), preferred_element_type=F32)
            b = jnp.dot(x, wu_ref[0, 0].astype(BF16), preferred_element_type=F32)
            hid = (jax.nn.silu(a) * b * g).astype(BF16)
            return jnp.dot(hid, wd_ref[0, 0].astype(BF16), preferred_element_type=F32)

        y_ref[...] = (ffn(wgl_ref, wul_ref, wdl_ref, xs_ref[:, D_MODEL:D_MODEL + 1])
                      + ffn(wgh_ref, wuh_ref, wdh_ref, xs_ref[:, D_MODEL + 1:D_MODEL + 2]))

    @pl.when(tval_ref[r] == 0)
    def _():
        y_ref[...] = jnp.zeros_like(y_ref)


def _head_rms(x, g):
    return jnp.concatenate([_rms(x[:, h * HEAD_DIM:(h + 1) * HEAD_DIM], g) for h in range(N_HEADS)], axis=1)


def _kvq_kernel(h_in_ref, y_ref, kvg_ref, wkv_ref, kng_ref, bg_ref, wq_ref, qng_ref,
                h_ref, kp_ref, vp_ref, ksm_ref, vsm_ref, kb_ref, vb_ref, qb_ref, qsm_ref, *, npt):
    i = pl.program_id(0)
    h2 = h_in_ref[...] + y_ref[...]
    h_ref[...] = h2
    kv = jnp.dot(_rms(h2, kvg_ref[...]).astype(BF16), wkv_ref[...], preferred_element_type=F32)
    q = jnp.dot(_rms(h2, bg_ref[...]).astype(BF16), wq_ref[...], preferred_element_type=F32)
    k = _head_rms(kv[:, :D_MODEL], kng_ref[...])
    v = kv[:, D_MODEL:]
    q = _head_rms(q, qng_ref[...])

    @pl.when(i < npt)
    def _():
        kp_ref[...] = k
        vp_ref[...] = v
        for h in range(N_HEADS):
            sl = slice(h * HEAD_DIM, (h + 1) * HEAD_DIM)
            kb_ref[0, h] = k[:, sl].astype(BF16)
            vb_ref[0, h] = v[:, sl].astype(BF16)
            qb_ref[0, h] = q[:, sl].astype(BF16)

    @pl.when(i >= npt)
    def _():
        ksm_ref[...] = k
        vsm_ref[...] = v
        qsm_ref[...] = q


def _softplus(z):
    return jnp.maximum(z, 0.0) + jnp.log1p(jnp.exp(-jnp.abs(z)))


def _stick_block(q, k, v, r, visible, later_keys):
    z = lax.dot_general(q, k, (((1,), (1,)), ((), ())), preferred_element_type=F32) * (HEAD_DIM ** -0.5)
    sp = _softplus(z)
    if visible is not None:
        sp = jnp.where(visible, sp, 0.0)
    sp_hi = sp.astype(BF16)
    sp_lo = (sp - sp_hi.astype(F32)).astype(BF16)
    c = (jnp.dot(sp_hi, later_keys, preferred_element_type=F32)
         + jnp.dot(sp_lo, later_keys, preferred_element_type=F32))
    w = jnp.exp(z - c - r)
    if visible is not None:
        w = jnp.where(visible, w, 0.0)
    out = jnp.dot(w.astype(BF16), v, preferred_element_type=F32)
    return out, r + c[:, 0:1]


def _attn_kernel(thr_ref, q_ref, k_ref, v_ref, o_ref, r_sc, acc_sc):
    i = pl.program_id(1)
    tq = q_ref.shape[2]
    rowi = lax.broadcasted_iota(I32, (tq, tq), 0)
    coli = lax.broadcasted_iota(I32, (tq, tq), 1)
    visible = coli < rowi
    later_keys = (rowi >= coli).astype(BF16)
    thr = thr_ref[0, 0]
    for h in range(N_HEADS):
        q = q_ref[0, h]

        def block(j, vis):
            start = pl.multiple_of(j * tq, tq)
            out, r = _stick_block(q, k_ref[0, h, pl.ds(start, tq), :], v_ref[0, h, pl.ds(start, tq), :],
                                  r_sc[...], vis, later_keys)
            acc_sc[...] += out
            r_sc[...] = r
            return jnp.min(r)

        r_sc[...] = jnp.zeros_like(r_sc)
        acc_sc[...] = jnp.zeros_like(acc_sc)
        rmin = block(i, visible)

        def cond(st):
            return jnp.logical_and(st[0] >= 0, st[1] <= thr)

        def body(st):
            return st[0] - 1, block(st[0], None)

        lax.while_loop(cond, body, (i - 1, rmin))
        o_ref[:, h * HEAD_DIM:(h + 1) * HEAD_DIM] = acc_sc[...].astype(o_ref.dtype)


def _sample_attn_kernel(q_ref, kn_ref, vn_ref, ck_ref, cv_ref, o_ref, r_sc, acc_sc, *, nkc):
    c = pl.program_id(1)
    ds = q_ref.shape[0]
    ck = ck_ref.shape[1]

    def tri(n):
        rowi = lax.broadcasted_iota(I32, (n, n), 0)
        coli = lax.broadcasted_iota(I32, (n, n), 1)
        return (rowi >= coli).astype(BF16)

    @pl.when(c == 0)
    def _():
        rowi = lax.broadcasted_iota(I32, (ds, LANES), 0)
        coli = lax.broadcasted_iota(I32, (ds, LANES), 1)
        visible = coli < rowi
        later_keys = tri(LANES)
        pad = jnp.zeros((LANES - ds, HEAD_DIM), BF16)
        for h in range(N_HEADS):
            sl = slice(h * HEAD_DIM, (h + 1) * HEAD_DIM)
            k = jnp.concatenate([kn_ref[:, sl].astype(BF16), pad], axis=0)
            v = jnp.concatenate([vn_ref[:, sl].astype(BF16), pad], axis=0)
            out, r = _stick_block(q_ref[:, sl].astype(BF16), k, v, jnp.zeros((ds, 1), F32), visible, later_keys)
            acc_sc[:, sl] = out
            r_sc[h] = r

    @pl.when(c > 0)
    def _():
        later_keys = tri(ck)
        for h in range(N_HEADS):
            sl = slice(h * HEAD_DIM, (h + 1) * HEAD_DIM)
            out, r = _stick_block(q_ref[:, sl].astype(BF16), ck_ref[0, :, sl].astype(BF16),
                                  cv_ref[0, :, sl].astype(BF16), r_sc[h], None, later_keys)
            acc_sc[:, sl] += out
            r_sc[h] = r

    @pl.when(c == nkc)
    def _():
        o_ref[...] = acc_sc[...].astype(o_ref.dtype)


def _final_kernel(h_ref, y_ref, yp_ref, ysm_ref, *, npt):
    i = pl.program_id(0)
    y = h_ref[...] + y_ref[...]

    @pl.when(i < npt)
    def _():
        yp_ref[...] = y

    @pl.when(i >= npt)
    def _():
        ysm_ref[...] = y


def _const_spec(shape):
    return pl.BlockSpec(shape, lambda *_: (0,) * len(shape))


def _router_operands(layer, ffn_norm_g, router_g_w, router_g_b, router_e_w, router_e_b):
    wrt = jnp.zeros((32, D_MODEL), F32)
    wrt = wrt.at[:N_GROUPS].set(router_g_w[layer].T)
    wrt = wrt.at[8:8 + N_GROUPS * EXPERTS_PER_GROUP].set(
        router_e_w[layer].transpose(0, 2, 1).reshape(N_GROUPS * EXPERTS_PER_GROUP, D_MODEL))
    rb = jnp.zeros((32,), F32)
    rb = rb.at[:N_GROUPS].set(router_g_b[layer])
    rb = rb.at[8:8 + N_GROUPS * EXPERTS_PER_GROUP].set(router_e_b[layer].reshape(-1))
    return ffn_norm_g[layer][None, :], wrt, jnp.broadcast_to(rb[:, None], (32, LANES))


def _permute(idx, src, n_out, scatter, name):
    n, w = idx.shape[0], src.shape[1]
    assert n % DMA_CHUNK == 0
    extra = (jnp.zeros((n_out, w), src.dtype),) if scatter else ()
    return pl.pallas_call(
        functools.partial(_permute_kernel, n=n, scatter=scatter),
        out_shape=jax.ShapeDtypeStruct((n_out, w), src.dtype),
        grid_spec=pltpu.PrefetchScalarGridSpec(
            num_scalar_prefetch=1, grid=(1,),
            in_specs=[pl.BlockSpec(memory_space=pl.ANY)] * (1 + len(extra)),
            out_specs=pl.BlockSpec(memory_space=pl.ANY),
            scratch_shapes=[pltpu.SemaphoreType.DMA((2,))]),
        input_output_aliases={2: 0} if scatter else {},
        compiler_params=pltpu.CompilerParams(dimension_semantics=("arbitrary",)),
        name=name,
    )(idx, src, *extra)


def _moe(xs_ext, bucket, layer, w_gate, w_up, w_down):
    nblk = bucket.shape[0]
    n_tok = nblk * TOKEN_BLOCK
    tm = EXPERT_TILE
    nr = -(-nblk // LANES) * LANES
    nt = -(-(n_tok + N_BUCKETS * (tm - 1)) // tm)
    assert nt <= LANES
    bk = jnp.pad(bucket.reshape(nblk, TOKEN_BLOCK), ((0, nr - nblk), (0, 0)), constant_values=N_BUCKETS)
    pos, tile_bucket = pl.pallas_call(
        functools.partial(_rank_kernel, tm=tm),
        out_shape=(jax.ShapeDtypeStruct((nr, LANES), I32), jax.ShapeDtypeStruct((1, LANES), I32)),
        name=f"moe_rank_{layer}",
    )(bk)
    pos = pos[:nblk].reshape(n_tok)
    tb = tile_bucket[0, :nt]
    n_used = jnp.sum((tb < N_BUCKETS).astype(I32))
    tsrc = jnp.minimum(jnp.arange(nt, dtype=I32), n_used - 1)
    tval = (jnp.arange(nt, dtype=I32) < n_used).astype(I32)
    tbc = jnp.minimum(tb[tsrc], N_BUCKETS - 1)
    pair_lo = jnp.array([p[0] for p in PAIRS], I32)
    pair_hi = jnp.array([p[1] for p in PAIRS], I32)
    grp = tbc // len(PAIRS)
    telo = grp * EXPERTS_PER_GROUP + pair_lo[tbc % len(PAIRS)]
    tehi = grp * EXPERTS_PER_GROUP + pair_hi[tbc % len(PAIRS)]

    xs_sorted = _permute(pos, xs_ext, nt * tm, True, f"moe_dispatch_{layer}")

    def wspec(shape, which):
        return pl.BlockSpec((1, 1) + shape, lambda r, lo, hi, src, val: (layer, (lo, hi)[which][r], 0, 0))

    up_shape = (D_MODEL, EXPERT_DIM)
    down_shape = (EXPERT_DIM, D_MODEL)
    y_sorted = pl.pallas_call(
        _expert_kernel,
        out_shape=jax.ShapeDtypeStruct((nt * tm, D_MODEL), F32),
        grid_spec=pltpu.PrefetchScalarGridSpec(
            num_scalar_prefetch=4, grid=(nt,),
            in_specs=[pl.BlockSpec((tm, ROW_W), lambda r, lo, hi, src, val: (src[r], 0)),
                      wspec(up_shape, 0), wspec(up_shape, 0), wspec(down_shape, 0),
                      wspec(up_shape, 1), wspec(up_shape, 1), wspec(down_shape, 1)],
            out_specs=pl.BlockSpec((tm, D_MODEL), lambda r, lo, hi, src, val: (r, 0))),
        compiler_params=_cparams("arbitrary"),
        name=f"moe_experts_{layer}",
    )(telo, tehi, tsrc, tval, xs_sorted, w_gate, w_up, w_down, w_gate, w_up, w_down)
    return _permute(pos, y_sorted, n_tok, False, f"moe_combine_{layer}")


def kernel(x_prompt, x_sample, cache_k, cache_v, a_norm_g, a_w_in, a_v_norm_g, a_w_s, a_b_s, a_w_out, kv_norm_g, w_kv, k_norm_g, b_norm_g, b_w_q, q_norm_g, b_w_o, ffn_norm_g, router_g_w, router_g_b, router_e_w, router_e_b, moe_w_gate, moe_w_up, moe_w_down):
    bsz, seq, _ = x_prompt.shape
    dbsz, dseq, _ = x_sample.shape
    past = cache_k.shape[1]
    tp, ts = bsz * seq, dbsz * dseq
    n_tok = tp + ts
    tb = TOKEN_BLOCK
    assert seq % ATTN_TILE == 0 and tp % KVQ_TILE == 0 and ts % KVQ_TILE == 0 and ts % tb == 0
    assert tb % dseq == 0 and dseq <= CHUNK and past % CACHE_CHUNK == 0
    npb, nsb = tp // tb, ts // tb
    nblk = npb + nsb
    xp = x_prompt.reshape(tp, D_MODEL)
    xsm = x_sample.reshape(ts, D_MODEL)

    def pblk(i):
        return jnp.minimum(i, npb - 1)

    def sblk(i):
        return jnp.maximum(i - npb, 0)

    tok_spec = pl.BlockSpec((tb, D_MODEL), lambda i: (i, 0))
    route_in_specs = [_const_spec((1, D_MODEL)), _const_spec((32, D_MODEL)), _const_spec((32, LANES))]
    route_out_shapes = (jax.ShapeDtypeStruct((n_tok, D_MODEL), F32), jax.ShapeDtypeStruct((n_tok, ROW_W), F32),
                        jax.ShapeDtypeStruct((nblk, 1, tb), I32))
    route_out_specs = [tok_spec, pl.BlockSpec((tb, ROW_W), lambda i: (i, 0)),
                       pl.BlockSpec((1, 1, tb), lambda i: (i, 0, 0))]

    pos_i = jnp.arange(GMLP_BLOCK)
    mask = (pos_i[None, :] // CHUNK) <= (pos_i[:, None] // CHUNK)
    w_prompt = jnp.where(mask, a_w_s[0], 0.0)
    rep = tb // dseq
    w_sample = jnp.einsum("ij,hts->hitjs", jnp.eye(rep, dtype=F32), w_prompt[:, :dseq, :dseq]).reshape(
        A_HEADS, tb, tb)
    ws_all = jnp.stack([w_prompt, w_sample]).astype(BF16)
    b_prompt = a_b_s[0]
    b_sample = jnp.tile(a_b_s[0][:, :dseq], (1, rep))
    bs_all = jnp.stack([jnp.repeat(b.T, A_HEAD_DIM, axis=1) for b in (b_prompt, b_sample)])
    fg0, wrt0, rb0 = _router_operands(0, ffn_norm_g, router_g_w, router_g_b, router_e_w, router_e_b)
    h1, xs1, bucket1, v_rows = pl.pallas_call(
        functools.partial(_a_layer_kernel, npb=npb),
        out_shape=route_out_shapes + (jax.ShapeDtypeStruct((ts, A_WIDTH), F32),),
        grid=(nblk,),
        in_specs=[pl.BlockSpec((tb, D_MODEL), lambda i: (pblk(i), 0)),
                  pl.BlockSpec((tb, D_MODEL), lambda i: (sblk(i), 0)),
                  _const_spec((1, D_MODEL)), _const_spec((D_MODEL, 2 * A_WIDTH)), _const_spec((1, A_WIDTH)),
                  pl.BlockSpec((1, A_HEADS, tb, tb), lambda i: (i // npb, 0, 0, 0)),
                  pl.BlockSpec((1, tb, A_WIDTH), lambda i: (i // npb, 0, 0)),
                  _const_spec((A_WIDTH, D_MODEL))] + route_in_specs,
        out_specs=route_out_specs + [pl.BlockSpec((tb, A_WIDTH), lambda i: (sblk(i), 0))],
        compiler_params=_cparams("arbitrary"),
        name="a_layer",
    )(xp, xsm, a_norm_g[0][None, :], a_w_in[0].astype(BF16), a_v_norm_g[0][None, :], ws_all, bs_all,
      a_w_out[0].astype(BF16), fg0, wrt0, rb0)
    y1 = _moe(xs1, bucket1, 0, moe_w_gate, moe_w_up, moe_w_down)

    tk = KVQ_TILE
    npt, nst = tp // tk, ts // tk
    spt = seq // tk

    def ptile(i):
        return jnp.minimum(i, npt - 1)

    def stile(i):
        return jnp.maximum(i - npt, 0)

    row_spec = pl.BlockSpec((tk, D_MODEL), lambda i: (i, 0))
    prow_spec = pl.BlockSpec((tk, D_MODEL), lambda i: (ptile(i), 0))
    srow_spec = pl.BlockSpec((tk, D_MODEL), lambda i: (stile(i), 0))
    head_spec = pl.BlockSpec((1, N_HEADS, tk, HEAD_DIM), lambda i: (ptile(i) // spt, 0, ptile(i) % spt, 0))
    f32_rows = lambda n: jax.ShapeDtypeStruct((n, D_MODEL), F32)
    head_major = jax.ShapeDtypeStruct((bsz, N_HEADS, seq, HEAD_DIM), BF16)
    h2, k_p, v_p, k_s, v_s, kb, vb, qb, q_s = pl.pallas_call(
        functools.partial(_kvq_kernel, npt=npt),
        out_shape=(f32_rows(n_tok), f32_rows(tp), f32_rows(tp), f32_rows(ts), f32_rows(ts),
                   head_major, head_major, head_major, f32_rows(ts)),
        grid=(npt + nst,),
        in_specs=[row_spec, row_spec, _const_spec((1, D_MODEL)), _const_spec((D_MODEL, 2 * D_MODEL)),
                  _const_spec((1, HEAD_DIM)), _const_spec((1, D_MODEL)), _const_spec((D_MODEL, D_MODEL)),
                  _const_spec((1, HEAD_DIM))],
        out_specs=[row_spec, prow_spec, prow_spec, srow_spec, srow_spec, head_spec, head_spec, head_spec,
                   srow_spec],
        compiler_params=_cparams("arbitrary"),
        name="kvq",
    )(h1, y1, kv_norm_g[None, :], w_kv.astype(BF16), k_norm_g[None, :], b_norm_g[0][None, :],
      b_w_q[0].astype(BF16), q_norm_g[0][None, :])

    tq = ATTN_TILE
    nq = seq // tq
    zmax = math.sqrt(HEAD_DIM) * jnp.max(jnp.abs(q_norm_g[0])) * jnp.max(jnp.abs(k_norm_g)) * 1.02
    thr = (EXP_UNDERFLOW + zmax).astype(F32).reshape(1, 1)
    qspec = pl.BlockSpec((1, N_HEADS, tq, HEAD_DIM), lambda b, i: (b, 0, i, 0))
    kvspec = pl.BlockSpec((1, N_HEADS, seq, HEAD_DIM), lambda b, i: (b, 0, 0, 0), pipeline_mode=pl.Buffered(1))
    o_p = pl.pallas_call(
        _attn_kernel,
        out_shape=jax.ShapeDtypeStruct((tp, D_MODEL), BF16),
        grid=(bsz, nq),
        in_specs=[pl.BlockSpec(memory_space=pltpu.SMEM), qspec, kvspec, kvspec],
        out_specs=pl.BlockSpec((tq, D_MODEL), lambda b, i: (b * nq + i, 0)),
        scratch_shapes=[pltpu.VMEM((tq, 1), F32), pltpu.VMEM((tq, HEAD_DIM), F32)],
        compiler_params=_cparams("arbitrary", "arbitrary"),
        name="attn_prompt",
    )(thr, qb, kb, vb)

    nkc = past // CACHE_CHUNK
    new_spec = pl.BlockSpec((dseq, D_MODEL), lambda b, c: (b, 0))
    cache_spec = pl.BlockSpec((1, CACHE_CHUNK, D_MODEL), lambda b, c: (b, nkc - jnp.maximum(c, 1), 0))
    o_s = pl.pallas_call(
        functools.partial(_sample_attn_kernel, nkc=nkc),
        out_shape=jax.ShapeDtypeStruct((ts, D_MODEL), BF16),
        grid=(dbsz, nkc + 1),
        in_specs=[new_spec, new_spec, new_spec, cache_spec, cache_spec],
        out_specs=new_spec,
        scratch_shapes=[pltpu.VMEM((N_HEADS, dseq, 1), F32), pltpu.VMEM((dseq, D_MODEL), F32)],
        compiler_params=_cparams("arbitrary", "arbitrary"),
        name="attn_sample",
    )(q_s, k_s, v_s, cache_k.reshape(dbsz, past, D_MODEL), cache_v.reshape(dbsz, past, D_MODEL))

    fg1, wrt1, rb1 = _router_operands(1, ffn_norm_g, router_g_w, router_g_b, router_e_w, router_e_b)
    h3, xs2, bucket2 = pl.pallas_call(
        functools.partial(_oproj_kernel, npb=npb),
        out_shape=route_out_shapes,
        grid=(nblk,),
        in_specs=[pl.BlockSpec((tb, D_MODEL), lambda i: (pblk(i), 0)),
                  pl.BlockSpec((tb, D_MODEL), lambda i: (sblk(i), 0)),
                  tok_spec, _const_spec((D_MODEL, D_MODEL))] + route_in_specs,
        out_specs=route_out_specs,
        compiler_params=_cparams("arbitrary"),
        name="o_proj",
    )(o_p, o_s, h2, b_w_o[0].astype(BF16), fg1, wrt1, rb1)
    y2 = _moe(xs2, bucket2, 1, moe_w_gate, moe_w_up, moe_w_down)

    y_p, y_s = pl.pallas_call(
        functools.partial(_final_kernel, npt=npt),
        out_shape=(f32_rows(tp), f32_rows(ts)),
        grid=(npt + nst,),
        in_specs=[row_spec, row_spec],
        out_specs=[prow_spec, srow_spec],
        compiler_params=_cparams("arbitrary"),
        name="final_residual",
    )(h3, y2)

    kv_shape_p = (bsz, seq, N_HEADS, HEAD_DIM)
    kv_shape_s = (dbsz, dseq, N_HEADS, HEAD_DIM)
    return (y_p.reshape(bsz, seq, D_MODEL), y_s.reshape(dbsz, dseq, D_MODEL),
            k_p.reshape(kv_shape_p), v_p.reshape(kv_shape_p), k_s.reshape(kv_shape_s), v_s.reshape(kv_shape_s),
            v_rows.reshape(1, dbsz, dseq, A_WIDTH))
```

```python
import functools
import math

import jax
import jax.numpy as jnp
from jax import lax
from jax.experimental import pallas as pl
from jax.experimental.pallas import tpu as pltpu

F32 = jnp.float32
BF16 = jnp.bfloat16
I32 = jnp.int32

D_MODEL = 1024
CHUNK = 64
GMLP_BLOCK = 128
A_WIDTH = 2 * D_MODEL
A_HEADS = 8
A_HEAD_DIM = A_WIDTH // A_HEADS
N_HEADS = 8
HEAD_DIM = D_MODEL // N_HEADS
N_GROUPS = 4
EXPERTS_PER_GROUP = 4
EXPERT_DIM = D_MODEL // 2
EPS = 1e-6

LANES = 128
SUBLANES = 8
SLABS = D_MODEL // LANES
assert SLABS == SUBLANES and N_HEADS == SLABS and HEAD_DIM == LANES
PAIRS = ((0, 1), (0, 2), (0, 3), (1, 2), (1, 3), (2, 3))
N_BUCKETS = N_GROUPS * len(PAIRS)
ROUTER_ROWS = 32
EXPERT_ROW0 = 8

TOKEN_BLOCK = GMLP_BLOCK
EXPERT_TILE = 256
KVQ_TILE = 256
ATTN_TILE = 256
CACHE_CHUNK = 512
DMA_CHUNK = 256
VMEM_LIMIT = 56 * 1024 * 1024
EXP_UNDERFLOW = 104.0


def _cparams(*sem):
    return pltpu.CompilerParams(dimension_semantics=sem, vmem_limit_bytes=VMEM_LIMIT)


def _rms(x, g):
    ms = jnp.mean(x * x, axis=-1, keepdims=True)
    return x * lax.rsqrt(ms + EPS) * g


def _load_rows(ref, n):
    return jnp.concatenate([ref[pl.ds(c, n, stride=SLABS), :] for c in range(SLABS)], axis=1)


def _store_rows(ref, x):
    n = x.shape[0]
    for c in range(SLABS):
        ref[pl.ds(c, n, stride=SLABS), :] = x[:, c * LANES:(c + 1) * LANES]


def _first_argmax(vals):
    m = vals[0]
    for v in vals[1:]:
        m = jnp.maximum(m, v)
    idx = jnp.full(m.shape, len(vals) - 1, I32)
    for r in range(len(vals) - 2, -1, -1):
        idx = jnp.where(vals[r] == m, r, idx)
    return m, idx


def _route(xn, wrt_ref, rb_ref, xs_ref, bucket_ref):
    lt = lax.dot_general(wrt_ref[...], xn, (((1,), (1,)), ((), ())),
                         precision=lax.Precision.HIGHEST, preferred_element_type=F32) + rb_ref[...]
    _, g_idx = _first_argmax([lt[r:r + 1, :] for r in range(N_GROUPS)])
    le = []
    for e in range(EXPERTS_PER_GROUP):
        row = EXPERT_ROW0 + EXPERTS_PER_GROUP * (N_GROUPS - 1) + e
        v = lt[row:row + 1, :]
        for g in range(N_GROUPS - 2, -1, -1):
            row = EXPERT_ROW0 + EXPERTS_PER_GROUP * g + e
            v = jnp.where(g_idx == g, lt[row:row + 1, :], v)
        le.append(v)
    _, i1 = _first_argmax(le)
    _, i2 = _first_argmax([jnp.where(i1 == e, -jnp.inf, le[e]) for e in range(EXPERTS_PER_GROUP)])
    lo = jnp.minimum(i1, i2)
    hi = jnp.maximum(i1, i2)
    pair = jnp.where(lo == 0, hi - 1, jnp.where(lo == 1, hi + 1, 5))
    bucket_ref[0] = g_idx * len(PAIRS) + pair
    _store_rows(xs_ref, xn)


def _a_layer_kernel(xp_ref, xsm_ref, ag_ref, win_ref, vg_ref, ws_ref, bs_ref, wout_ref, fg_ref, wrt_ref, rb_ref,
                    h_ref, xs_ref, bucket_ref, vrow_ref, *, npb):
    i = pl.program_id(0)
    x = jnp.where(i < npb, xp_ref[...], xsm_ref[...])
    xn = _rms(x, ag_ref[...]).astype(BF16)
    z = jax.nn.gelu(jnp.dot(xn, win_ref[...], preferred_element_type=F32))
    u = z[:, :A_WIDTH]
    v = _rms(z[:, A_WIDTH:], vg_ref[...])

    @pl.when(i >= npb)
    def _():
        vrow_ref[...] = v

    vb = v.astype(BF16)
    gate = jnp.concatenate(
        [jnp.dot(ws_ref[0, h], vb[:, h * A_HEAD_DIM:(h + 1) * A_HEAD_DIM], preferred_element_type=F32)
         for h in range(A_HEADS)], axis=1) + bs_ref[0]
    s = (u * gate).astype(BF16)
    h1 = x + jnp.dot(s, wout_ref[...], preferred_element_type=F32)
    h_ref[...] = h1
    _route(_rms(h1, fg_ref[...]), wrt_ref, rb_ref, xs_ref, bucket_ref)


def _oproj_kernel(op_ref, osm_ref, h_in_ref, wo_ref, fg_ref, wrt_ref, rb_ref, h_ref, xs_ref, bucket_ref, *, npb):
    i = pl.program_id(0)
    o = jnp.where(i < npb, op_ref[...], osm_ref[...])
    h3 = h_in_ref[...] + jnp.dot(o, wo_ref[...], preferred_element_type=F32)
    h_ref[...] = h3
    _route(_rms(h3, fg_ref[...]), wrt_ref, rb_ref, xs_ref, bucket_ref)


def _rank_kernel(b_ref, pos_ref, tb_ref, *, tm):
    bk = b_ref[...]
    nr = bk.shape[0]
    r_i = lax.broadcasted_iota(I32, (LANES, LANES), 0)
    c_i = lax.broadcasted_iota(I32, (LANES, LANES), 1)
    upper = (r_i <= c_i).astype(BF16)
    rr = lax.broadcasted_iota(I32, (nr, nr), 0)
    cc = lax.broadcasted_iota(I32, (nr, nr), 1)
    before_rows = (cc < rr).astype(BF16)
    tile_start = lax.broadcasted_iota(I32, (1, LANES), 1).astype(F32) * tm
    pos = jnp.zeros((nr, LANES), F32)
    seg_start = jnp.zeros((1, LANES), F32)
    tile_bucket = jnp.zeros((1, LANES), I32)
    for b in range(N_BUCKETS):
        m = bk == b
        pref = jnp.dot(jnp.where(m, 1.0, 0.0).astype(BF16), upper, preferred_element_type=F32)
        rowtot = jnp.broadcast_to(pref[:, LANES - 1:LANES], (nr, LANES))
        before = jnp.dot(before_rows, rowtot.astype(BF16), preferred_element_type=F32)
        cnt = jnp.sum(rowtot, axis=0, keepdims=True)
        pos = pos + jnp.where(m, seg_start + before + pref - 1.0, 0.0)
        seg_start = seg_start + jnp.ceil(cnt / tm) * tm
        tile_bucket = tile_bucket + (seg_start <= tile_start).astype(I32)
    pos_ref[...] = pos.astype(I32)
    tb_ref[...] = tile_bucket


def _inverse_kernel(pos_ref, inv_ref, *, n_tok, n_rows):
    def clear(r, carry):
        inv_ref[r] = 0
        return carry

    def put(t, carry):
        inv_ref[pos_ref[t]] = t
        return carry

    lax.fori_loop(0, n_rows, clear, 0, unroll=8)
    lax.fori_loop(0, n_tok, put, 0, unroll=8)


def _gather_rows_kernel(idx_ref, nchunks_ref, src_ref, out_ref, sem):
    nchunks = nchunks_ref[0]

    def copy(t, slot):
        s = pl.multiple_of(idx_ref[t] * SLABS, SLABS)
        d = pl.multiple_of(t * SLABS, SLABS)
        return pltpu.make_async_copy(src_ref.at[pl.ds(s, SLABS)], out_ref.at[pl.ds(d, SLABS)], sem.at[slot])

    def issue(c):
        def body(k, carry):
            copy(c * DMA_CHUNK + k, c % 2).start()
            return carry
        lax.fori_loop(0, DMA_CHUNK, body, 0, unroll=8)

    def drain(c):
        def body(k, carry):
            copy(c * DMA_CHUNK + k, c % 2).wait()
            return carry
        lax.fori_loop(0, DMA_CHUNK, body, 0, unroll=8)

    issue(0)

    def step(c, carry):
        issue(c)
        drain(c - 1)
        return carry

    lax.fori_loop(1, nchunks, step, 0)
    drain(nchunks - 1)


def _expert_kernel(telo_ref, tehi_ref, tsrc_ref, tval_ref, xs_ref, wr_ref, rb_ref, wgl_ref, wul_ref, wdl_ref,
                   wgh_ref, wuh_ref, wdh_ref, y_ref):
    del tsrc_ref
    r = pl.program_id(0)
    tm = xs_ref.shape[0] // SLABS

    @pl.when(tval_ref[r] == 1)
    def _():
        x = _load_rows(xs_ref, tm).astype(BF16)
        logits = jnp.dot(x, wr_ref[...], preferred_element_type=F32) + rb_ref[...]
        lane = lax.broadcasted_iota(I32, logits.shape, 1)
        grp = telo_ref[r] // EXPERTS_PER_GROUP
        is_group = lane < N_GROUPS
        m = jnp.max(jnp.where(is_group, logits, -jnp.inf), axis=1, keepdims=True)
        ex = jnp.exp(logits - m)
        p_g = (jnp.sum(jnp.where(lane == grp, ex, 0.0), axis=1, keepdims=True)
               / jnp.sum(jnp.where(is_group, ex, 0.0), axis=1, keepdims=True))
        l_lo = jnp.sum(jnp.where(lane == EXPERT_ROW0 + telo_ref[r], logits, 0.0), axis=1, keepdims=True)
        l_hi = jnp.sum(jnp.where(lane == EXPERT_ROW0 + tehi_ref[r], logits, 0.0), axis=1, keepdims=True)
        mm = jnp.maximum(l_lo, l_hi)
        e_lo = jnp.exp(l_lo - mm)
        e_hi = jnp.exp(l_hi - mm)
        g_lo = p_g * (e_lo / (e_lo + e_hi))
        g_hi = p_g * (e_hi / (e_lo + e_hi))

        def ffn(wg_ref, wu_ref, wd_ref, g):
            a = jnp.dot(x, wg_ref[0, 0].astype(BF16), preferred_element_type=F32)
            b = jnp.dot(x, wu_ref[0, 0].astype(BF16), preferred_element_type=F32)
            hid = (jax.nn.silu(a) * b * g).astype(BF16)
            return jnp.dot(hid, wd_ref[0, 0].astype(BF16), preferred_element_type=F32)

        _store_rows(y_ref, ffn(wgl_ref, wul_ref, wdl_ref, g_lo) + ffn(wgh_ref, wuh_ref, wdh_ref, g_hi))

    @pl.when(tval_ref[r] == 0)
    def _():
        y_ref[...] = jnp.zeros_like(y_ref)


def _head_rms(x, g):
    return jnp.concatenate([_rms(x[:, h * HEAD_DIM:(h + 1) * HEAD_DIM], g) for h in range(N_HEADS)], axis=1)


def _kvq_kernel(h_in_ref, y_ref, kvg_ref, wkv_ref, kng_ref, bg_ref, wq_ref, qng_ref,
                h_ref, kp_ref, vp_ref, ksm_ref, vsm_ref, kb_ref, vb_ref, qb_ref, qsm_ref, *, npt):
    i = pl.program_id(0)
    tk = h_in_ref.shape[0]
    h2 = h_in_ref[...] + _load_rows(y_ref, tk)
    h_ref[...] = h2
    kv = jnp.dot(_rms(h2, kvg_ref[...]).astype(BF16), wkv_ref[...], preferred_element_type=F32)
    q = jnp.dot(_rms(h2, bg_ref[...]).astype(BF16), wq_ref[...], preferred_element_type=F32)
    k = _head_rms(kv[:, :D_MODEL], kng_ref[...])
    v = kv[:, D_MODEL:]
    q = _head_rms(q, qng_ref[...])

    @pl.when(i < npt)
    def _():
        _store_rows(kp_ref, k)
        _store_rows(vp_ref, v)
        for h in range(N_HEADS):
            sl = slice(h * HEAD_DIM, (h + 1) * HEAD_DIM)
            kb_ref[0, h] = k[:, sl].astype(BF16)
            vb_ref[0, h] = v[:, sl].astype(BF16)
            qb_ref[0, h] = q[:, sl].astype(BF16)

    @pl.when(i >= npt)
    def _():
        _store_rows(ksm_ref, k)
        _store_rows(vsm_ref, v)
        qsm_ref[...] = q


def _softplus(z):
    return jnp.maximum(z, 0.0) + jnp.log1p(jnp.exp(-jnp.abs(z)))


def _stick_block(q, k, v, r, visible, later_keys):
    z = lax.dot_general(q, k, (((1,), (1,)), ((), ())), preferred_element_type=F32) * (HEAD_DIM ** -0.5)
    sp = _softplus(z)
    if visible is not None:
        sp = jnp.where(visible, sp, 0.0)
    sp_hi = sp.astype(BF16)
    sp_lo = (sp - sp_hi.astype(F32)).astype(BF16)
    c = (jnp.dot(sp_hi, later_keys, preferred_element_type=F32)
         + jnp.dot(sp_lo, later_keys, preferred_element_type=F32))
    w = jnp.exp(z - c - r)
    if visible is not None:
        w = jnp.where(visible, w, 0.0)
    out = jnp.dot(w.astype(BF16), v, preferred_element_type=F32)
    return out, r + c[:, 0:1]


def _attn_kernel(thr_ref, q_ref, k_ref, v_ref, o_ref, r_sc, acc_sc):
    i = pl.program_id(1)
    tq = q_ref.shape[2]
    rowi = lax.broadcasted_iota(I32, (tq, tq), 0)
    coli = lax.broadcasted_iota(I32, (tq, tq), 1)
    visible = coli < rowi
    later_keys = (rowi >= coli).astype(BF16)
    thr = thr_ref[0, 0]
    for h in range(N_HEADS):
        q = q_ref[0, h]

        def block(j, vis):
            start = pl.multiple_of(j * tq, tq)
            out, r = _stick_block(q, k_ref[0, h, pl.ds(start, tq), :], v_ref[0, h, pl.ds(start, tq), :],
                                  r_sc[...], vis, later_keys)
            acc_sc[...] += out
            r_sc[...] = r
            return jnp.min(r)

        r_sc[...] = jnp.zeros_like(r_sc)
        acc_sc[...] = jnp.zeros_like(acc_sc)
        rmin = block(i, visible)

        def cond(st):
            return jnp.logical_and(st[0] >= 0, st[1] <= thr)

        def body(st):
            return st[0] - 1, block(st[0], None)

        lax.while_loop(cond, body, (i - 1, rmin))
        o_ref[:, h * HEAD_DIM:(h + 1) * HEAD_DIM] = acc_sc[...].astype(o_ref.dtype)


def _sample_attn_kernel(q_ref, kn_ref, vn_ref, ck_ref, cv_ref, o_ref, r_sc, acc_sc, *, nkc):
    c = pl.program_id(1)
    ds = q_ref.shape[0]
    ck = ck_ref.shape[0] // N_HEADS

    def tri(n):
        rowi = lax.broadcasted_iota(I32, (n, n), 0)
        coli = lax.broadcasted_iota(I32, (n, n), 1)
        return (rowi >= coli).astype(BF16)

    def head_rows(ref, h, n):
        return ref[pl.ds(h, n, stride=N_HEADS), :].astype(BF16)

    @pl.when(c == 0)
    def _():
        rowi = lax.broadcasted_iota(I32, (ds, LANES), 0)
        coli = lax.broadcasted_iota(I32, (ds, LANES), 1)
        visible = coli < rowi
        later_keys = tri(LANES)
        pad = jnp.zeros((LANES - ds, HEAD_DIM), BF16)
        for h in range(N_HEADS):
            sl = slice(h * HEAD_DIM, (h + 1) * HEAD_DIM)
            k = jnp.concatenate([head_rows(kn_ref, h, ds), pad], axis=0)
            v = jnp.concatenate([head_rows(vn_ref, h, ds), pad], axis=0)
            out, r = _stick_block(q_ref[:, sl].astype(BF16), k, v, jnp.zeros((ds, 1), F32), visible, later_keys)
            acc_sc[:, sl] = out
            r_sc[h] = r

    @pl.when(c > 0)
    def _():
        later_keys = tri(ck)
        for h in range(N_HEADS):
            sl = slice(h * HEAD_DIM, (h + 1) * HEAD_DIM)
            out, r = _stick_block(q_ref[:, sl].astype(BF16), head_rows(ck_ref, h, ck), head_rows(cv_ref, h, ck),
                                  r_sc[h], None, later_keys)
            acc_sc[:, sl] += out
            r_sc[h] = r

    @pl.when(c == nkc)
    def _():
        o_ref[...] = acc_sc[...].astype(o_ref.dtype)


def _final_kernel(h_ref, y_ref, yp_ref, ysm_ref, *, npt):
    i = pl.program_id(0)
    y = h_ref[...] + _load_rows(y_ref, h_ref.shape[0])

    @pl.when(i < npt)
    def _():
        yp_ref[...] = y

    @pl.when(i >= npt)
    def _():
        ysm_ref[...] = y


def _const_spec(shape):
    return pl.BlockSpec(shape, lambda *_: (0,) * len(shape))


def _router_operands(layer, ffn_norm_g, router_g_w, router_g_b, router_e_w, router_e_b):
    n_e = N_GROUPS * EXPERTS_PER_GROUP
    wrt = jnp.zeros((ROUTER_ROWS, D_MODEL), F32)
    wrt = wrt.at[:N_GROUPS].set(router_g_w[layer].T)
    wrt = wrt.at[EXPERT_ROW0:EXPERT_ROW0 + n_e].set(router_e_w[layer].transpose(0, 2, 1).reshape(n_e, D_MODEL))
    rb = jnp.zeros((ROUTER_ROWS,), F32)
    rb = rb.at[:N_GROUPS].set(router_g_b[layer])
    rb = rb.at[EXPERT_ROW0:EXPERT_ROW0 + n_e].set(router_e_b[layer].reshape(-1))
    wr_cols = jnp.zeros((D_MODEL, LANES), F32).at[:, :ROUTER_ROWS].set(wrt.T).astype(BF16)
    rb_cols = jnp.zeros((1, LANES), F32).at[0, :ROUTER_ROWS].set(rb)
    return (ffn_norm_g[layer][None, :], wrt, jnp.broadcast_to(rb[:, None], (ROUTER_ROWS, LANES))), (wr_cols, rb_cols)


def _gather_rows(idx, src, name):
    n = idx.shape[0]
    assert n % DMA_CHUNK == 0 and n >= DMA_CHUNK
    return pl.pallas_call(
        _gather_rows_kernel,
        out_shape=jax.ShapeDtypeStruct((n * SLABS, LANES), src.dtype),
        grid_spec=pltpu.PrefetchScalarGridSpec(
            num_scalar_prefetch=2, grid=(1,),
            in_specs=[pl.BlockSpec(memory_space=pl.ANY)],
            out_specs=pl.BlockSpec(memory_space=pl.ANY),
            scratch_shapes=[pltpu.SemaphoreType.DMA((2,))]),
        compiler_params=pltpu.CompilerParams(dimension_semantics=("arbitrary",)),
        name=name,
    )(idx, jnp.full((1,), n // DMA_CHUNK, I32), src)


def _moe(xs_rt, bucket, layer, route_cols, w_gate, w_up, w_down):
    nblk = bucket.shape[0]
    n_tok = nblk * TOKEN_BLOCK
    tm = EXPERT_TILE
    nr = -(-nblk // LANES) * LANES
    nt = -(-(n_tok + N_BUCKETS * (tm - 1)) // tm)
    assert nt <= LANES
    bk = jnp.pad(bucket.reshape(nblk, TOKEN_BLOCK), ((0, nr - nblk), (0, 0)), constant_values=N_BUCKETS)
    pos, tile_bucket = pl.pallas_call(
        functools.partial(_rank_kernel, tm=tm),
        out_shape=(jax.ShapeDtypeStruct((nr, LANES), I32), jax.ShapeDtypeStruct((1, LANES), I32)),
        name=f"moe_rank_{layer}",
    )(bk)
    pos = pos[:nblk].reshape(n_tok)
    inv = pl.pallas_call(
        functools.partial(_inverse_kernel, n_tok=n_tok, n_rows=nt * tm),
        out_shape=jax.ShapeDtypeStruct((nt * tm,), I32),
        in_specs=[pl.BlockSpec(memory_space=pltpu.SMEM)],
        out_specs=pl.BlockSpec(memory_space=pltpu.SMEM),
        name=f"moe_inverse_{layer}",
    )(pos)
    tb = tile_bucket[0, :nt]
    n_used = jnp.sum((tb < N_BUCKETS).astype(I32))
    tsrc = jnp.minimum(jnp.arange(nt, dtype=I32), n_used - 1)
    tval = (jnp.arange(nt, dtype=I32) < n_used).astype(I32)
    tbc = jnp.minimum(tb[tsrc], N_BUCKETS - 1)
    pair_lo = jnp.array([p[0] for p in PAIRS], I32)
    pair_hi = jnp.array([p[1] for p in PAIRS], I32)
    grp = tbc // len(PAIRS)
    telo = grp * EXPERTS_PER_GROUP + pair_lo[tbc % len(PAIRS)]
    tehi = grp * EXPERTS_PER_GROUP + pair_hi[tbc % len(PAIRS)]

    xs_sorted = _gather_rows(inv, xs_rt, f"moe_dispatch_{layer}")

    def wspec(shape, which):
        return pl.BlockSpec((1, 1) + shape, lambda r, lo, hi, src, val: (layer, (lo, hi)[which][r], 0, 0))

    up_shape = (D_MODEL, EXPERT_DIM)
    down_shape = (EXPERT_DIM, D_MODEL)
    wr_cols, rb_cols = route_cols
    y_sorted = pl.pallas_call(
        _expert_kernel,
        out_shape=jax.ShapeDtypeStruct((nt * tm * SLABS, LANES), F32),
        grid_spec=pltpu.PrefetchScalarGridSpec(
            num_scalar_prefetch=4, grid=(nt,),
            in_specs=[pl.BlockSpec((tm * SLABS, LANES), lambda r, lo, hi, src, val: (src[r], 0)),
                      pl.BlockSpec((D_MODEL, LANES), lambda r, lo, hi, src, val: (0, 0)),
                      pl.BlockSpec((1, LANES), lambda r, lo, hi, src, val: (0, 0)),
                      wspec(up_shape, 0), wspec(up_shape, 0), wspec(down_shape, 0),
                      wspec(up_shape, 1), wspec(up_shape, 1), wspec(down_shape, 1)],
            out_specs=pl.BlockSpec((tm * SLABS, LANES), lambda r, lo, hi, src, val: (r, 0))),
        compiler_params=_cparams("arbitrary"),
        name=f"moe_experts_{layer}",
    )(telo, tehi, tsrc, tval, xs_sorted, wr_cols, rb_cols, w_gate, w_up, w_down, w_gate, w_up, w_down)
    return _gather_rows(pos, y_sorted, f"moe_combine_{layer}")


def kernel(x_prompt, x_sample, cache_k, cache_v, a_norm_g, a_w_in, a_v_norm_g, a_w_s, a_b_s, a_w_out, kv_norm_g, w_kv, k_norm_g, b_norm_g, b_w_q, q_norm_g, b_w_o, ffn_norm_g, router_g_w, router_g_b, router_e_w, router_e_b, moe_w_gate, moe_w_up, moe_w_down):
    bsz, seq, _ = x_prompt.shape
    dbsz, dseq, _ = x_sample.shape
    past = cache_k.shape[1]
    tp, ts = bsz * seq, dbsz * dseq
    n_tok = tp + ts
    tb = TOKEN_BLOCK
    assert seq % ATTN_TILE == 0 and tp % KVQ_TILE == 0 and ts % KVQ_TILE == 0 and ts % tb == 0
    assert tb % dseq == 0 and dseq <= CHUNK and past % CACHE_CHUNK == 0 and n_tok % DMA_CHUNK == 0
    npb, nsb = tp // tb, ts // tb
    nblk = npb + nsb
    xp = x_prompt.reshape(tp, D_MODEL)
    xsm = x_sample.reshape(ts, D_MODEL)

    def pblk(i):
        return jnp.minimum(i, npb - 1)

    def sblk(i):
        return jnp.maximum(i - npb, 0)

    def rt_shape(n):
        return jax.ShapeDtypeStruct((n * SLABS, LANES), F32)

    def rt_spec(rows, index):
        return pl.BlockSpec((rows * SLABS, LANES), lambda *i: (index(*i), 0))

    tok_spec = pl.BlockSpec((tb, D_MODEL), lambda i: (i, 0))
    route_in_specs = [_const_spec((1, D_MODEL)), _const_spec((ROUTER_ROWS, D_MODEL)),
                      _const_spec((ROUTER_ROWS, LANES))]
    route_out_shapes = (jax.ShapeDtypeStruct((n_tok, D_MODEL), F32), rt_shape(n_tok),
                        jax.ShapeDtypeStruct((nblk, 1, tb), I32))
    route_out_specs = [tok_spec, rt_spec(tb, lambda i: i), pl.BlockSpec((1, 1, tb), lambda i: (i, 0, 0))]

    pos_i = jnp.arange(GMLP_BLOCK)
    mask = (pos_i[None, :] // CHUNK) <= (pos_i[:, None] // CHUNK)
    w_prompt = jnp.where(mask, a_w_s[0], 0.0)
    rep = tb // dseq
    w_sample = jnp.einsum("ij,hts->hitjs", jnp.eye(rep, dtype=F32), w_prompt[:, :dseq, :dseq]).reshape(
        A_HEADS, tb, tb)
    ws_all = jnp.stack([w_prompt, w_sample]).astype(BF16)
    b_prompt = a_b_s[0]
    b_sample = jnp.tile(a_b_s[0][:, :dseq], (1, rep))
    bs_all = jnp.stack([jnp.repeat(b.T, A_HEAD_DIM, axis=1) for b in (b_prompt, b_sample)])
    route0, cols0 = _router_operands(0, ffn_norm_g, router_g_w, router_g_b, router_e_w, router_e_b)
    h1, xs1, bucket1, v_rows = pl.pallas_call(
        functools.partial(_a_layer_kernel, npb=npb),
        out_shape=route_out_shapes + (jax.ShapeDtypeStruct((ts, A_WIDTH), F32),),
        grid=(nblk,),
        in_specs=[pl.BlockSpec((tb, D_MODEL), lambda i: (pblk(i), 0)),
                  pl.BlockSpec((tb, D_MODEL), lambda i: (sblk(i), 0)),
                  _const_spec((1, D_MODEL)), _const_spec((D_MODEL, 2 * A_WIDTH)), _const_spec((1, A_WIDTH)),
                  pl.BlockSpec((1, A_HEADS, tb, tb), lambda i: (i // npb, 0, 0, 0)),
                  pl.BlockSpec((1, tb, A_WIDTH), lambda i: (i // npb, 0, 0)),
                  _const_spec((A_WIDTH, D_MODEL))] + route_in_specs,
        out_specs=route_out_specs + [pl.BlockSpec((tb, A_WIDTH), lambda i: (sblk(i), 0))],
        compiler_params=_cparams("arbitrary"),
        name="a_layer",
    )(xp, xsm, a_norm_g[0][None, :], a_w_in[0].astype(BF16), a_v_norm_g[0][None, :], ws_all, bs_all,
      a_w_out[0].astype(BF16), *route0)
    y1 = _moe(xs1, bucket1, 0, cols0, moe_w_gate, moe_w_up, moe_w_down)

    tk = KVQ_TILE
    npt, nst = tp // tk, ts // tk
    spt = seq // tk

    def ptile(i):
        return jnp.minimum(i, npt - 1)

    def stile(i):
        return jnp.maximum(i - npt, 0)

    row_spec = pl.BlockSpec((tk, D_MODEL), lambda i: (i, 0))
    prow_spec = pl.BlockSpec((tk, D_MODEL), lambda i: (ptile(i), 0))
    srow_spec = pl.BlockSpec((tk, D_MODEL), lambda i: (stile(i), 0))
    head_spec = pl.BlockSpec((1, N_HEADS, tk, HEAD_DIM), lambda i: (ptile(i) // spt, 0, ptile(i) % spt, 0))
    f32_rows = lambda n: jax.ShapeDtypeStruct((n, D_MODEL), F32)
    head_major = jax.ShapeDtypeStruct((bsz, N_HEADS, seq, HEAD_DIM), BF16)
    h2, k_p, v_p, k_s, v_s, kb, vb, qb, q_s = pl.pallas_call(
        functools.partial(_kvq_kernel, npt=npt),
        out_shape=(f32_rows(n_tok), rt_shape(tp), rt_shape(tp), rt_shape(ts), rt_shape(ts),
                   head_major, head_major, head_major, f32_rows(ts)),
        grid=(npt + nst,),
        in_specs=[row_spec, rt_spec(tk, lambda i: i), _const_spec((1, D_MODEL)),
                  _const_spec((D_MODEL, 2 * D_MODEL)), _const_spec((1, HEAD_DIM)), _const_spec((1, D_MODEL)),
                  _const_spec((D_MODEL, D_MODEL)), _const_spec((1, HEAD_DIM))],
        out_specs=[row_spec, rt_spec(tk, ptile), rt_spec(tk, ptile), rt_spec(tk, stile), rt_spec(tk, stile),
                   head_spec, head_spec, head_spec, srow_spec],
        compiler_params=_cparams("arbitrary"),
        name="kvq",
    )(h1, y1, kv_norm_g[None, :], w_kv.astype(BF16), k_norm_g[None, :], b_norm_g[0][None, :],
      b_w_q[0].astype(BF16), q_norm_g[0][None, :])

    tq = ATTN_TILE
    nq = seq // tq
    zmax = math.sqrt(HEAD_DIM) * jnp.max(jnp.abs(q_norm_g[0])) * jnp.max(jnp.abs(k_norm_g)) * 1.02
    thr = (EXP_UNDERFLOW + zmax).astype(F32).reshape(1, 1)
    qspec = pl.BlockSpec((1, N_HEADS, tq, HEAD_DIM), lambda b, i: (b, 0, i, 0))
    kvspec = pl.BlockSpec((1, N_HEADS, seq, HEAD_DIM), lambda b, i: (b, 0, 0, 0), pipeline_mode=pl.Buffered(1))
    o_p = pl.pallas_call(
        _attn_kernel,
        out_shape=jax.ShapeDtypeStruct((tp, D_MODEL), BF16),
        grid=(bsz, nq),
        in_specs=[pl.BlockSpec(memory_space=pltpu.SMEM), qspec, kvspec, kvspec],
        out_specs=pl.BlockSpec((tq, D_MODEL), lambda b, i: (b * nq + i, 0)),
        scratch_shapes=[pltpu.VMEM((tq, 1), F32), pltpu.VMEM((tq, HEAD_DIM), F32)],
        compiler_params=_cparams("arbitrary", "arbitrary"),
        name="attn_prompt",
    )(thr, qb, kb, vb)

    nkc = past // CACHE_CHUNK
    new_spec = pl.BlockSpec((dseq, D_MODEL), lambda b, c: (b, 0))
    new_rt_spec = rt_spec(dseq, lambda b, c: b)
    cache_spec = rt_spec(CACHE_CHUNK, lambda b, c: b * nkc + nkc - jnp.maximum(c, 1))
    o_s = pl.pallas_call(
        functools.partial(_sample_attn_kernel, nkc=nkc),
        out_shape=jax.ShapeDtypeStruct((ts, D_MODEL), BF16),
        grid=(dbsz, nkc + 1),
        in_specs=[new_spec, new_rt_spec, new_rt_spec, cache_spec, cache_spec],
        out_specs=new_spec,
        scratch_shapes=[pltpu.VMEM((N_HEADS, dseq, 1), F32), pltpu.VMEM((dseq, D_MODEL), F32)],
        compiler_params=_cparams("arbitrary", "arbitrary"),
        name="attn_sample",
    )(q_s, k_s, v_s, cache_k.reshape(dbsz * past * N_HEADS, HEAD_DIM),
      cache_v.reshape(dbsz * past * N_HEADS, HEAD_DIM))

    route1, cols1 = _router_operands(1, ffn_norm_g, router_g_w, router_g_b, router_e_w, router_e_b)
    h3, xs2, bucket2 = pl.pallas_call(
        functools.partial(_oproj_kernel, npb=npb),
        out_shape=route_out_shapes,
        grid=(nblk,),
        in_specs=[pl.BlockSpec((tb, D_MODEL), lambda i: (pblk(i), 0)),
                  pl.BlockSpec((tb, D_MODEL), lambda i: (sblk(i), 0)),
                  tok_spec, _const_spec((D_MODEL, D_MODEL))] + route_in_specs,
        out_specs=route_out_specs,
        compiler_params=_cparams("arbitrary"),
        name="o_proj",
    )(o_p, o_s, h2, b_w_o[0].astype(BF16), *route1)
    y2 = _moe(xs2, bucket2, 1, cols1, moe_w_gate, moe_w_up, moe_w_down)

    y_p, y_s = pl.pallas_call(
        functools.partial(_final_kernel, npt=npt),
        out_shape=(f32_rows(tp), f32_rows(ts)),
        grid=(npt + nst,),
        in_specs=[row_spec, rt_spec(tk, lambda i: i)],
        out_specs=[prow_spec, srow_spec],
        compiler_params=_cparams("arbitrary"),
        name="final_residual",
    )(h3, y2)

    kv_shape_p = (bsz, seq, N_HEADS, HEAD_DIM)
    kv_shape_s = (dbsz, dseq, N_HEADS, HEAD_DIM)
    return (y_p.reshape(bsz, seq, D_MODEL), y_s.reshape(dbsz, dseq, D_MODEL),
            k_p.reshape(kv_shape_p), v_p.reshape(kv_shape_p), k_s.reshape(kv_shape_s), v_s.reshape(kv_shape_s),
            v_rows.reshape(1, dbsz, dseq, A_WIDTH))
```

```python
import functools
import math

import jax
import jax.numpy as jnp
from jax import lax
from jax.experimental import pallas as pl
from jax.experimental.pallas import tpu as pltpu

F32 = jnp.float32
BF16 = jnp.bfloat16
I32 = jnp.int32

D_MODEL = 1024
CHUNK = 64
GMLP_BLOCK = 128
A_WIDTH = 2 * D_MODEL
A_HEADS = 8
A_HEAD_DIM = A_WIDTH // A_HEADS
N_HEADS = 8
HEAD_DIM = D_MODEL // N_HEADS
N_GROUPS = 4
EXPERTS_PER_GROUP = 4
EXPERT_DIM = D_MODEL // 2
EPS = 1e-6

LANES = 128
SUBLANES = 8
SLABS = D_MODEL // LANES
assert SLABS == SUBLANES and N_HEADS == SLABS and HEAD_DIM == LANES
PAIRS = ((0, 1), (0, 2), (0, 3), (1, 2), (1, 3), (2, 3))
N_BUCKETS = N_GROUPS * len(PAIRS)
ROUTER_ROWS = 32
EXPERT_ROW0 = 8

TOKEN_BLOCK = GMLP_BLOCK
EXPERT_TILE = 256
KVQ_TILE = 256
ATTN_TILE = 256
CACHE_CHUNK = 512
VMEM_LIMIT = 56 * 1024 * 1024
EXP_UNDERFLOW = 104.0


def _cparams(*sem):
    return pltpu.CompilerParams(dimension_semantics=sem, vmem_limit_bytes=VMEM_LIMIT)


def _rms(x, g):
    ms = jnp.mean(x * x, axis=-1, keepdims=True)
    return x * lax.rsqrt(ms + EPS) * g


def _load_rows(ref, n):
    return jnp.concatenate([ref[pl.ds(c, n, stride=SLABS), :] for c in range(SLABS)], axis=1)


def _store_rows(ref, x):
    n = x.shape[0]
    for c in range(SLABS):
        ref[pl.ds(c, n, stride=SLABS), :] = x[:, c * LANES:(c + 1) * LANES]


def _first_argmax(vals):
    m = vals[0]
    for v in vals[1:]:
        m = jnp.maximum(m, v)
    idx = jnp.full(m.shape, len(vals) - 1, I32)
    for r in range(len(vals) - 2, -1, -1):
        idx = jnp.where(vals[r] == m, r, idx)
    return m, idx


def _route(xn, wrt_ref, rb_ref, xs_ref, bucket_ref):
    lt = lax.dot_general(wrt_ref[...], xn, (((1,), (1,)), ((), ())),
                         precision=lax.Precision.HIGHEST, preferred_element_type=F32) + rb_ref[...]
    _, g_idx = _first_argmax([lt[r:r + 1, :] for r in range(N_GROUPS)])
    le = []
    for e in range(EXPERTS_PER_GROUP):
        row = EXPERT_ROW0 + EXPERTS_PER_GROUP * (N_GROUPS - 1) + e
        v = lt[row:row + 1, :]
        for g in range(N_GROUPS - 2, -1, -1):
            row = EXPERT_ROW0 + EXPERTS_PER_GROUP * g + e
            v = jnp.where(g_idx == g, lt[row:row + 1, :], v)
        le.append(v)
    _, i1 = _first_argmax(le)
    _, i2 = _first_argmax([jnp.where(i1 == e, -jnp.inf, le[e]) for e in range(EXPERTS_PER_GROUP)])
    lo = jnp.minimum(i1, i2)
    hi = jnp.maximum(i1, i2)
    pair = jnp.where(lo == 0, hi - 1, jnp.where(lo == 1, hi + 1, 5))
    bucket_ref[0] = g_idx * len(PAIRS) + pair
    _store_rows(xs_ref, xn)


def _a_layer_kernel(xp_ref, xsm_ref, ag_ref, win_ref, vg_ref, ws_ref, bs_ref, wout_ref, fg_ref, wrt_ref, rb_ref,
                    h_ref, xs_ref, bucket_ref, vrow_ref, *, npb):
    i = pl.program_id(0)
    x = jnp.where(i < npb, xp_ref[...], xsm_ref[...])
    xn = _rms(x, ag_ref[...]).astype(BF16)
    z = jax.nn.gelu(jnp.dot(xn, win_ref[...], preferred_element_type=F32))
    u = z[:, :A_WIDTH]
    v = _rms(z[:, A_WIDTH:], vg_ref[...])

    @pl.when(i >= npb)
    def _():
        vrow_ref[...] = v

    vb = v.astype(BF16)
    gate = jnp.concatenate(
        [jnp.dot(ws_ref[0, h], vb[:, h * A_HEAD_DIM:(h + 1) * A_HEAD_DIM], preferred_element_type=F32)
         for h in range(A_HEADS)], axis=1) + bs_ref[0]
    s = (u * gate).astype(BF16)
    h1 = x + jnp.dot(s, wout_ref[...], preferred_element_type=F32)
    h_ref[...] = h1
    _route(_rms(h1, fg_ref[...]), wrt_ref, rb_ref, xs_ref, bucket_ref)


def _oproj_kernel(op_ref, osm_ref, h_in_ref, wo_ref, fg_ref, wrt_ref, rb_ref, h_ref, xs_ref, bucket_ref, *, npb):
    i = pl.program_id(0)
    o = jnp.where(i < npb, op_ref[...], osm_ref[...])
    h3 = h_in_ref[...] + jnp.dot(o, wo_ref[...], preferred_element_type=F32)
    h_ref[...] = h3
    _route(_rms(h3, fg_ref[...]), wrt_ref, rb_ref, xs_ref, bucket_ref)


def _rank_kernel(b_ref, pos_ref, tb_ref, *, tm):
    bk = b_ref[...]
    nr = bk.shape[0]
    r_i = lax.broadcasted_iota(I32, (LANES, LANES), 0)
    c_i = lax.broadcasted_iota(I32, (LANES, LANES), 1)
    upper = (r_i <= c_i).astype(BF16)
    rr = lax.broadcasted_iota(I32, (nr, nr), 0)
    cc = lax.broadcasted_iota(I32, (nr, nr), 1)
    before_rows = (cc < rr).astype(BF16)
    tile_start = lax.broadcasted_iota(I32, (1, LANES), 1).astype(F32) * tm
    pos = jnp.zeros((nr, LANES), F32)
    seg_start = jnp.zeros((1, LANES), F32)
    tile_bucket = jnp.zeros((1, LANES), I32)
    for b in range(N_BUCKETS):
        m = bk == b
        pref = jnp.dot(jnp.where(m, 1.0, 0.0).astype(BF16), upper, preferred_element_type=F32)
        rowtot = jnp.broadcast_to(pref[:, LANES - 1:LANES], (nr, LANES))
        before = jnp.dot(before_rows, rowtot.astype(BF16), preferred_element_type=F32)
        cnt = jnp.sum(rowtot, axis=0, keepdims=True)
        pos = pos + jnp.where(m, seg_start + before + pref - 1.0, 0.0)
        seg_start = seg_start + jnp.ceil(cnt / tm) * tm
        tile_bucket = tile_bucket + (seg_start <= tile_start).astype(I32)
    pos_ref[...] = pos.astype(I32)
    tb_ref[...] = tile_bucket


def _inverse_kernel(pos_ref, inv_ref, *, n_tok, n_rows):
    def clear(r, carry):
        inv_ref[r] = -1
        return carry

    def put(t, carry):
        inv_ref[pos_ref[t]] = t
        return carry

    lax.fori_loop(0, n_rows, clear, 0, unroll=8)
    lax.fori_loop(0, n_tok, put, 0, unroll=8)


def _expert_kernel(telo_ref, tehi_ref, nused_ref, gsrc_ref, sdst_ref, xs_hbm, wr_ref, rb_ref, wgl_ref, wul_ref,
                   wdl_ref, wgh_ref, wuh_ref, wdh_ref, y_hbm, xbuf, ybuf, gsem, ssem, *, tm):
    r = pl.program_id(0)
    nt = pl.num_programs(0)
    n_used = nused_ref[0]

    def row(k):
        return pl.ds(k * SLABS, SLABS)

    def gather(tile, slot, wait):
        for k in range(tm):
            src = 0 if wait else pl.multiple_of(gsrc_ref[tile * tm + k] * SLABS, SLABS)
            cp = pltpu.make_async_copy(xs_hbm.at[pl.ds(src, SLABS)], xbuf.at[slot, row(k)], gsem.at[slot])
            cp.wait() if wait else cp.start()

    def scatter(tile, slot, wait):
        for k in range(tm):
            dst = 0 if wait else pl.multiple_of(sdst_ref[tile * tm + k] * SLABS, SLABS)
            cp = pltpu.make_async_copy(ybuf.at[slot, row(k)], y_hbm.at[pl.ds(dst, SLABS)], ssem.at[slot])
            cp.wait() if wait else cp.start()

    @pl.when(r == 0)
    def _():
        gather(0, 0, False)

    @pl.when(r < n_used)
    def _():
        slot = r % 2

        @pl.when(r + 1 < n_used)
        def _():
            gather(r + 1, 1 - slot, False)

        gather(r, slot, True)

        @pl.when(r >= 2)
        def _():
            scatter(r - 2, slot, True)

        x = _load_rows(xbuf.at[slot], tm).astype(BF16)
        logits = jnp.dot(x, wr_ref[...], preferred_element_type=F32) + rb_ref[...]
        lane = lax.broadcasted_iota(I32, logits.shape, 1)
        grp = telo_ref[r] // EXPERTS_PER_GROUP
        is_group = lane < N_GROUPS
        m = jnp.max(jnp.where(is_group, logits, -jnp.inf), axis=1, keepdims=True)
        ex = jnp.exp(logits - m)
        p_g = (jnp.sum(jnp.where(lane == grp, ex, 0.0), axis=1, keepdims=True)
               / jnp.sum(jnp.where(is_group, ex, 0.0), axis=1, keepdims=True))
        l_lo = jnp.sum(jnp.where(lane == EXPERT_ROW0 + telo_ref[r], logits, 0.0), axis=1, keepdims=True)
        l_hi = jnp.sum(jnp.where(lane == EXPERT_ROW0 + tehi_ref[r], logits, 0.0), axis=1, keepdims=True)
        mm = jnp.maximum(l_lo, l_hi)
        e_lo = jnp.exp(l_lo - mm)
        e_hi = jnp.exp(l_hi - mm)
        g_lo = p_g * (e_lo / (e_lo + e_hi))
        g_hi = p_g * (e_hi / (e_lo + e_hi))

        def ffn(wg_ref, wu_ref, wd_ref, g):
            a = jnp.dot(x, wg_ref[0, 0].astype(BF16), preferred_element_type=F32)
            b = jnp.dot(x, wu_ref[0, 0].astype(BF16), preferred_element_type=F32)
            hid = (jax.nn.silu(a) * b * g).astype(BF16)
            return jnp.dot(hid, wd_ref[0, 0].astype(BF16), preferred_element_type=F32)

        _store_rows(ybuf.at[slot], ffn(wgl_ref, wul_ref, wdl_ref, g_lo) + ffn(wgh_ref, wuh_ref, wdh_ref, g_hi))
        scatter(r, slot, False)

    @pl.when(r == nt - 1)
    def _():
        @pl.when(n_used >= 2)
        def _():
            scatter(0, n_used % 2, True)

        scatter(0, (n_used - 1) % 2, True)


def _head_rms(x, g):
    return jnp.concatenate([_rms(x[:, h * HEAD_DIM:(h + 1) * HEAD_DIM], g) for h in range(N_HEADS)], axis=1)


def _kvq_kernel(h_in_ref, y_ref, kvg_ref, wkv_ref, kng_ref, bg_ref, wq_ref, qng_ref,
                h_ref, kp_ref, vp_ref, ksm_ref, vsm_ref, kb_ref, vb_ref, qb_ref, qsm_ref, *, npt):
    i = pl.program_id(0)
    tk = h_in_ref.shape[0]
    h2 = h_in_ref[...] + _load_rows(y_ref, tk)
    h_ref[...] = h2
    kv = jnp.dot(_rms(h2, kvg_ref[...]).astype(BF16), wkv_ref[...], preferred_element_type=F32)
    q = jnp.dot(_rms(h2, bg_ref[...]).astype(BF16), wq_ref[...], preferred_element_type=F32)
    k = _head_rms(kv[:, :D_MODEL], kng_ref[...])
    v = kv[:, D_MODEL:]
    q = _head_rms(q, qng_ref[...])

    @pl.when(i < npt)
    def _():
        _store_rows(kp_ref, k)
        _store_rows(vp_ref, v)
        for h in range(N_HEADS):
            sl = slice(h * HEAD_DIM, (h + 1) * HEAD_DIM)
            kb_ref[0, h] = k[:, sl].astype(BF16)
            vb_ref[0, h] = v[:, sl].astype(BF16)
            qb_ref[0, h] = q[:, sl].astype(BF16)

    @pl.when(i >= npt)
    def _():
        _store_rows(ksm_ref, k)
        _store_rows(vsm_ref, v)
        qsm_ref[...] = q


def _softplus(z):
    return jnp.maximum(z, 0.0) + jnp.log1p(jnp.exp(-jnp.abs(z)))


def _stick_block(q, k, v, r, visible, later_keys):
    z = lax.dot_general(q, k, (((1,), (1,)), ((), ())), preferred_element_type=F32) * (HEAD_DIM ** -0.5)
    sp = _softplus(z)
    if visible is not None:
        sp = jnp.where(visible, sp, 0.0)
    sp_hi = sp.astype(BF16)
    sp_lo = (sp - sp_hi.astype(F32)).astype(BF16)
    c = (jnp.dot(sp_hi, later_keys, preferred_element_type=F32)
         + jnp.dot(sp_lo, later_keys, preferred_element_type=F32))
    w = jnp.exp(z - c - r)
    if visible is not None:
        w = jnp.where(visible, w, 0.0)
    out = jnp.dot(w.astype(BF16), v, preferred_element_type=F32)
    return out, r + c[:, 0:1]


def _attn_kernel(thr_ref, q_ref, k_ref, v_ref, o_ref, r_sc, acc_sc):
    i = pl.program_id(1)
    tq = q_ref.shape[2]
    rowi = lax.broadcasted_iota(I32, (tq, tq), 0)
    coli = lax.broadcasted_iota(I32, (tq, tq), 1)
    visible = coli < rowi
    later_keys = (rowi >= coli).astype(BF16)
    thr = thr_ref[0, 0]
    for h in range(N_HEADS):
        q = q_ref[0, h]

        def block(j, vis):
            start = pl.multiple_of(j * tq, tq)
            out, r = _stick_block(q, k_ref[0, h, pl.ds(start, tq), :], v_ref[0, h, pl.ds(start, tq), :],
                                  r_sc[...], vis, later_keys)
            acc_sc[...] += out
            r_sc[...] = r
            return jnp.min(r)

        r_sc[...] = jnp.zeros_like(r_sc)
        acc_sc[...] = jnp.zeros_like(acc_sc)
        rmin = block(i, visible)

        def cond(st):
            return jnp.logical_and(st[0] >= 0, st[1] <= thr)

        def body(st):
            return st[0] - 1, block(st[0], None)

        lax.while_loop(cond, body, (i - 1, rmin))
        o_ref[:, h * HEAD_DIM:(h + 1) * HEAD_DIM] = acc_sc[...].astype(o_ref.dtype)


def _sample_attn_kernel(q_ref, kn_ref, vn_ref, ck_ref, cv_ref, o_ref, r_sc, acc_sc, *, nkc):
    c = pl.program_id(1)
    ds = q_ref.shape[0]
    ck = ck_ref.shape[0] // N_HEADS

    def tri(n):
        rowi = lax.broadcasted_iota(I32, (n, n), 0)
        coli = lax.broadcasted_iota(I32, (n, n), 1)
        return (rowi >= coli).astype(BF16)

    def head_rows(ref, h, n):
        return ref[pl.ds(h, n, stride=N_HEADS), :].astype(BF16)

    @pl.when(c == 0)
    def _():
        rowi = lax.broadcasted_iota(I32, (ds, LANES), 0)
        coli = lax.broadcasted_iota(I32, (ds, LANES), 1)
        visible = coli < rowi
        later_keys = tri(LANES)
        pad = jnp.zeros((LANES - ds, HEAD_DIM), BF16)
        for h in range(N_HEADS):
            sl = slice(h * HEAD_DIM, (h + 1) * HEAD_DIM)
            k = jnp.concatenate([head_rows(kn_ref, h, ds), pad], axis=0)
            v = jnp.concatenate([head_rows(vn_ref, h, ds), pad], axis=0)
            out, r = _stick_block(q_ref[:, sl].astype(BF16), k, v, jnp.zeros((ds, 1), F32), visible, later_keys)
            acc_sc[:, sl] = out
            r_sc[h] = r

    @pl.when(c > 0)
    def _():
        later_keys = tri(ck)
        for h in range(N_HEADS):
            sl = slice(h * HEAD_DIM, (h + 1) * HEAD_DIM)
            out, r = _stick_block(q_ref[:, sl].astype(BF16), head_rows(ck_ref, h, ck), head_rows(cv_ref, h, ck),
                                  r_sc[h], None, later_keys)
            acc_sc[:, sl] += out
            r_sc[h] = r

    @pl.when(c == nkc)
    def _():
        o_ref[...] = acc_sc[...].astype(o_ref.dtype)


def _final_kernel(h_ref, y_ref, yp_ref, ysm_ref, *, npt):
    i = pl.program_id(0)
    y = h_ref[...] + _load_rows(y_ref, h_ref.shape[0])

    @pl.when(i < npt)
    def _():
        yp_ref[...] = y

    @pl.when(i >= npt)
    def _():
        ysm_ref[...] = y


def _const_spec(shape):
    return pl.BlockSpec(shape, lambda *_: (0,) * len(shape))


def _router_operands(layer, ffn_norm_g, router_g_w, router_g_b, router_e_w, router_e_b):
    n_e = N_GROUPS * EXPERTS_PER_GROUP
    wrt = jnp.zeros((ROUTER_ROWS, D_MODEL), F32)
    wrt = wrt.at[:N_GROUPS].set(router_g_w[layer].T)
    wrt = wrt.at[EXPERT_ROW0:EXPERT_ROW0 + n_e].set(router_e_w[layer].transpose(0, 2, 1).reshape(n_e, D_MODEL))
    rb = jnp.zeros((ROUTER_ROWS,), F32)
    rb = rb.at[:N_GROUPS].set(router_g_b[layer])
    rb = rb.at[EXPERT_ROW0:EXPERT_ROW0 + n_e].set(router_e_b[layer].reshape(-1))
    wr_cols = jnp.zeros((D_MODEL, LANES), F32).at[:, :ROUTER_ROWS].set(wrt.T).astype(BF16)
    rb_cols = jnp.zeros((1, LANES), F32).at[0, :ROUTER_ROWS].set(rb)
    return (ffn_norm_g[layer][None, :], wrt, jnp.broadcast_to(rb[:, None], (ROUTER_ROWS, LANES))), (wr_cols, rb_cols)


def _moe(xs_rt, bucket, layer, route_cols, w_gate, w_up, w_down):
    nblk = bucket.shape[0]
    n_tok = nblk * TOKEN_BLOCK
    tm = EXPERT_TILE
    nr = -(-nblk // LANES) * LANES
    nt = -(-(n_tok + N_BUCKETS * (tm - 1)) // tm)
    assert nt <= LANES
    bk = jnp.pad(bucket.reshape(nblk, TOKEN_BLOCK), ((0, nr - nblk), (0, 0)), constant_values=N_BUCKETS)
    pos, tile_bucket = pl.pallas_call(
        functools.partial(_rank_kernel, tm=tm),
        out_shape=(jax.ShapeDtypeStruct((nr, LANES), I32), jax.ShapeDtypeStruct((1, LANES), I32)),
        name=f"moe_rank_{layer}",
    )(bk)
    pos = pos[:nblk].reshape(n_tok)
    inv = pl.pallas_call(
        functools.partial(_inverse_kernel, n_tok=n_tok, n_rows=nt * tm),
        out_shape=jax.ShapeDtypeStruct((nt * tm,), I32),
        in_specs=[pl.BlockSpec(memory_space=pltpu.SMEM)],
        out_specs=pl.BlockSpec(memory_space=pltpu.SMEM),
        name=f"moe_inverse_{layer}",
    )(pos)
    rows = jnp.arange(nt * tm, dtype=I32)
    gsrc = jnp.maximum(inv, 0)
    sdst = jnp.where(inv < 0, n_tok + rows, inv)
    tb = tile_bucket[0, :nt]
    n_used = jnp.sum((tb < N_BUCKETS).astype(I32))
    tsrc = jnp.minimum(jnp.arange(nt, dtype=I32), n_used - 1)
    tbc = jnp.minimum(tb[tsrc], N_BUCKETS - 1)
    pair_lo = jnp.array([p[0] for p in PAIRS], I32)
    pair_hi = jnp.array([p[1] for p in PAIRS], I32)
    grp = tbc // len(PAIRS)
    telo = grp * EXPERTS_PER_GROUP + pair_lo[tbc % len(PAIRS)]
    tehi = grp * EXPERTS_PER_GROUP + pair_hi[tbc % len(PAIRS)]

    def wspec(shape, which):
        return pl.BlockSpec((1, 1) + shape, lambda r, lo, hi, *_: (layer, (lo, hi)[which][r], 0, 0))

    up_shape = (D_MODEL, EXPERT_DIM)
    down_shape = (EXPERT_DIM, D_MODEL)
    wr_cols, rb_cols = route_cols
    buf = pltpu.VMEM((2, tm * SLABS, LANES), F32)
    return pl.pallas_call(
        functools.partial(_expert_kernel, tm=tm),
        out_shape=jax.ShapeDtypeStruct(((n_tok + nt * tm) * SLABS, LANES), F32),
        grid_spec=pltpu.PrefetchScalarGridSpec(
            num_scalar_prefetch=5, grid=(nt,),
            in_specs=[pl.BlockSpec(memory_space=pl.ANY),
                      pl.BlockSpec((D_MODEL, LANES), lambda r, *_: (0, 0)),
                      pl.BlockSpec((1, LANES), lambda r, *_: (0, 0)),
                      wspec(up_shape, 0), wspec(up_shape, 0), wspec(down_shape, 0),
                      wspec(up_shape, 1), wspec(up_shape, 1), wspec(down_shape, 1)],
            out_specs=pl.BlockSpec(memory_space=pl.ANY),
            scratch_shapes=[buf, buf, pltpu.SemaphoreType.DMA((2,)), pltpu.SemaphoreType.DMA((2,))]),
        compiler_params=_cparams("arbitrary"),
        name=f"moe_experts_{layer}",
    )(telo, tehi, n_used.reshape(1), gsrc, sdst, xs_rt, wr_cols, rb_cols,
      w_gate, w_up, w_down, w_gate, w_up, w_down)


def kernel(x_prompt, x_sample, cache_k, cache_v, a_norm_g, a_w_in, a_v_norm_g, a_w_s, a_b_s, a_w_out, kv_norm_g, w_kv, k_norm_g, b_norm_g, b_w_q, q_norm_g, b_w_o, ffn_norm_g, router_g_w, router_g_b, router_e_w, router_e_b, moe_w_gate, moe_w_up, moe_w_down):
    bsz, seq, _ = x_prompt.shape
    dbsz, dseq, _ = x_sample.shape
    past = cache_k.shape[1]
    tp, ts = bsz * seq, dbsz * dseq
    n_tok = tp + ts
    tb = TOKEN_BLOCK
    assert seq % ATTN_TILE == 0 and tp % KVQ_TILE == 0 and ts % KVQ_TILE == 0 and ts % tb == 0
    assert tb % dseq == 0 and dseq <= CHUNK and past % CACHE_CHUNK == 0
    npb, nsb = tp // tb, ts // tb
    nblk = npb + nsb
    xp = x_prompt.reshape(tp, D_MODEL)
    xsm = x_sample.reshape(ts, D_MODEL)

    def pblk(i):
        return jnp.minimum(i, npb - 1)

    def sblk(i):
        return jnp.maximum(i - npb, 0)

    def rt_shape(n):
        return jax.ShapeDtypeStruct((n * SLABS, LANES), F32)

    def rt_spec(rows, index):
        return pl.BlockSpec((rows * SLABS, LANES), lambda *i: (index(*i), 0))

    tok_spec = pl.BlockSpec((tb, D_MODEL), lambda i: (i, 0))
    route_in_specs = [_const_spec((1, D_MODEL)), _const_spec((ROUTER_ROWS, D_MODEL)),
                      _const_spec((ROUTER_ROWS, LANES))]
    route_out_shapes = (jax.ShapeDtypeStruct((n_tok, D_MODEL), F32), rt_shape(n_tok),
                        jax.ShapeDtypeStruct((nblk, 1, tb), I32))
    route_out_specs = [tok_spec, rt_spec(tb, lambda i: i), pl.BlockSpec((1, 1, tb), lambda i: (i, 0, 0))]

    pos_i = jnp.arange(GMLP_BLOCK)
    mask = (pos_i[None, :] // CHUNK) <= (pos_i[:, None] // CHUNK)
    w_prompt = jnp.where(mask, a_w_s[0], 0.0)
    rep = tb // dseq
    w_sample = jnp.einsum("ij,hts->hitjs", jnp.eye(rep, dtype=F32), w_prompt[:, :dseq, :dseq]).reshape(
        A_HEADS, tb, tb)
    ws_all = jnp.stack([w_prompt, w_sample]).astype(BF16)
    b_prompt = a_b_s[0]
    b_sample = jnp.tile(a_b_s[0][:, :dseq], (1, rep))
    bs_all = jnp.stack([jnp.repeat(b.T, A_HEAD_DIM, axis=1) for b in (b_prompt, b_sample)])
    route0, cols0 = _router_operands(0, ffn_norm_g, router_g_w, router_g_b, router_e_w, router_e_b)
    h1, xs1, bucket1, v_rows = pl.pallas_call(
        functools.partial(_a_layer_kernel, npb=npb),
        out_shape=route_out_shapes + (jax.ShapeDtypeStruct((ts, A_WIDTH), F32),),
        grid=(nblk,),
        in_specs=[pl.BlockSpec((tb, D_MODEL), lambda i: (pblk(i), 0)),
                  pl.BlockSpec((tb, D_MODEL), lambda i: (sblk(i), 0)),
                  _const_spec((1, D_MODEL)), _const_spec((D_MODEL, 2 * A_WIDTH)), _const_spec((1, A_WIDTH)),
                  pl.BlockSpec((1, A_HEADS, tb, tb), lambda i: (i // npb, 0, 0, 0)),
                  pl.BlockSpec((1, tb, A_WIDTH), lambda i: (i // npb, 0, 0)),
                  _const_spec((A_WIDTH, D_MODEL))] + route_in_specs,
        out_specs=route_out_specs + [pl.BlockSpec((tb, A_WIDTH), lambda i: (sblk(i), 0))],
        compiler_params=_cparams("arbitrary"),
        name="a_layer",
    )(xp, xsm, a_norm_g[0][None, :], a_w_in[0].astype(BF16), a_v_norm_g[0][None, :], ws_all, bs_all,
      a_w_out[0].astype(BF16), *route0)
    y1 = _moe(xs1, bucket1, 0, cols0, moe_w_gate, moe_w_up, moe_w_down)

    tk = KVQ_TILE
    npt, nst = tp // tk, ts // tk
    spt = seq // tk

    def ptile(i):
        return jnp.minimum(i, npt - 1)

    def stile(i):
        return jnp.maximum(i - npt, 0)

    row_spec = pl.BlockSpec((tk, D_MODEL), lambda i: (i, 0))
    prow_spec = pl.BlockSpec((tk, D_MODEL), lambda i: (ptile(i), 0))
    srow_spec = pl.BlockSpec((tk, D_MODEL), lambda i: (stile(i), 0))
    head_spec = pl.BlockSpec((1, N_HEADS, tk, HEAD_DIM), lambda i: (ptile(i) // spt, 0, ptile(i) % spt, 0))
    f32_rows = lambda n: jax.ShapeDtypeStruct((n, D_MODEL), F32)
    head_major = jax.ShapeDtypeStruct((bsz, N_HEADS, seq, HEAD_DIM), BF16)
    h2, k_p, v_p, k_s, v_s, kb, vb, qb, q_s = pl.pallas_call(
        functools.partial(_kvq_kernel, npt=npt),
        out_shape=(f32_rows(n_tok), rt_shape(tp), rt_shape(tp), rt_shape(ts), rt_shape(ts),
                   head_major, head_major, head_major, f32_rows(ts)),
        grid=(npt + nst,),
        in_specs=[row_spec, rt_spec(tk, lambda i: i), _const_spec((1, D_MODEL)),
                  _const_spec((D_MODEL, 2 * D_MODEL)), _const_spec((1, HEAD_DIM)), _const_spec((1, D_MODEL)),
                  _const_spec((D_MODEL, D_MODEL)), _const_spec((1, HEAD_DIM))],
        out_specs=[row_spec, rt_spec(tk, ptile), rt_spec(tk, ptile), rt_spec(tk, stile), rt_spec(tk, stile),
                   head_spec, head_spec, head_spec, srow_spec],
        compiler_params=_cparams("arbitrary"),
        name="kvq",
    )(h1, y1, kv_norm_g[None, :], w_kv.astype(BF16), k_norm_g[None, :], b_norm_g[0][None, :],
      b_w_q[0].astype(BF16), q_norm_g[0][None, :])

    tq = ATTN_TILE
    nq = seq // tq
    zmax = math.sqrt(HEAD_DIM) * jnp.max(jnp.abs(q_norm_g[0])) * jnp.max(jnp.abs(k_norm_g)) * 1.02
    thr = (EXP_UNDERFLOW + zmax).astype(F32).reshape(1, 1)
    qspec = pl.BlockSpec((1, N_HEADS, tq, HEAD_DIM), lambda b, i: (b, 0, i, 0))
    kvspec = pl.BlockSpec((1, N_HEADS, seq, HEAD_DIM), lambda b, i: (b, 0, 0, 0), pipeline_mode=pl.Buffered(1))
    o_p = pl.pallas_call(
        _attn_kernel,
        out_shape=jax.ShapeDtypeStruct((tp, D_MODEL), BF16),
        grid=(bsz, nq),
        in_specs=[pl.BlockSpec(memory_space=pltpu.SMEM), qspec, kvspec, kvspec],
        out_specs=pl.BlockSpec((tq, D_MODEL), lambda b, i: (b * nq + i, 0)),
        scratch_shapes=[pltpu.VMEM((tq, 1), F32), pltpu.VMEM((tq, HEAD_DIM), F32)],
        compiler_params=_cparams("arbitrary", "arbitrary"),
        name="attn_prompt",
    )(thr, qb, kb, vb)

    nkc = past // CACHE_CHUNK
    new_spec = pl.BlockSpec((dseq, D_MODEL), lambda b, c: (b, 0))
    new_rt_spec = rt_spec(dseq, lambda b, c: b)
    cache_spec = rt_spec(CACHE_CHUNK, lambda b, c: b * nkc + nkc - jnp.maximum(c, 1))
    o_s = pl.pallas_call(
        functools.partial(_sample_attn_kernel, nkc=nkc),
        out_shape=jax.ShapeDtypeStruct((ts, D_MODEL), BF16),
        grid=(dbsz, nkc + 1),
        in_specs=[new_spec, new_rt_spec, new_rt_spec, cache_spec, cache_spec],
        out_specs=new_spec,
        scratch_shapes=[pltpu.VMEM((N_HEADS, dseq, 1), F32), pltpu.VMEM((dseq, D_MODEL), F32)],
        compiler_params=_cparams("arbitrary", "arbitrary"),
        name="attn_sample",
    )(q_s, k_s, v_s, cache_k.reshape(dbsz * past * N_HEADS, HEAD_DIM),
      cache_v.reshape(dbsz * past * N_HEADS, HEAD_DIM))

    route1, cols1 = _router_operands(1, ffn_norm_g, router_g_w, router_g_b, router_e_w, router_e_b)
    h3, xs2, bucket2 = pl.pallas_call(
        functools.partial(_oproj_kernel, npb=npb),
        out_shape=route_out_shapes,
        grid=(nblk,),
        in_specs=[pl.BlockSpec((tb, D_MODEL), lambda i: (pblk(i), 0)),
                  pl.BlockSpec((tb, D_MODEL), lambda i: (sblk(i), 0)),
                  tok_spec, _const_spec((D_MODEL, D_MODEL))] + route_in_specs,
        out_specs=route_out_specs,
        compiler_params=_cparams("arbitrary"),
        name="o_proj",
    )(o_p, o_s, h2, b_w_o[0].astype(BF16), *route1)
    y2 = _moe(xs2, bucket2, 1, cols1, moe_w_gate, moe_w_up, moe_w_down)

    y_p, y_s = pl.pallas_call(
        functools.partial(_final_kernel, npt=npt),
        out_shape=(f32_rows(tp), f32_rows(ts)),
        grid=(npt + nst,),
        in_specs=[row_spec, rt_spec(tk, lambda i: i)],
        out_specs=[prow_spec, srow_spec],
        compiler_params=_cparams("arbitrary"),
        name="final_residual",
    )(h3, y2)

    kv_shape_p = (bsz, seq, N_HEADS, HEAD_DIM)
    kv_shape_s = (dbsz, dseq, N_HEADS, HEAD_DIM)
    return (y_p.reshape(bsz, seq, D_MODEL), y_s.reshape(dbsz, dseq, D_MODEL),
            k_p.reshape(kv_shape_p), v_p.reshape(kv_shape_p), k_s.reshape(kv_shape_s), v_s.reshape(kv_shape_s),
            v_rows.reshape(1, dbsz, dseq, A_WIDTH))
```

```python
import functools

import jax
import jax.numpy as jnp
from jax import lax
from jax.experimental import pallas as pl
from jax.experimental.pallas import tpu as pltpu

F32 = jnp.float32
BF16 = jnp.bfloat16
I32 = jnp.int32

D_MODEL = 1024
CHUNK = 64
GMLP_BLOCK = 128
A_WIDTH = 2 * D_MODEL
A_HEADS = 8
A_HEAD_DIM = A_WIDTH // A_HEADS
N_HEADS = 8
HEAD_DIM = D_MODEL // N_HEADS
N_GROUPS = 4
EXPERTS_PER_GROUP = 4
EXPERT_DIM = D_MODEL // 2
EPS = 1e-6

LANES = 128
SUBLANES = 8
SLABS = D_MODEL // LANES
assert SLABS == SUBLANES and N_HEADS == SLABS and HEAD_DIM == LANES
PAIRS = ((0, 1), (0, 2), (0, 3), (1, 2), (1, 3), (2, 3))
N_BUCKETS = N_GROUPS * len(PAIRS)
ROUTER_ROWS = 32
EXPERT_ROW0 = 8

TOKEN_BLOCK = 2 * GMLP_BLOCK
EXPERT_TILE = 256
KVQ_TILE = 256
ATTN_TILE = 256
VMEM_LIMIT = 56 * 1024 * 1024
STOP_MASS = 105.0


def _cparams(*sem):
    return pltpu.CompilerParams(dimension_semantics=sem, vmem_limit_bytes=VMEM_LIMIT)


def _rms(x, g):
    ms = jnp.mean(x * x, axis=-1, keepdims=True)
    return x * lax.rsqrt(ms + EPS) * g


def _load_rows(ref, n):
    return jnp.concatenate([ref[pl.ds(c, n, stride=SLABS), :] for c in range(SLABS)], axis=1)


def _store_rows(ref, x):
    n = x.shape[0]
    for c in range(SLABS):
        ref[pl.ds(c, n, stride=SLABS), :] = x[:, c * LANES:(c + 1) * LANES]


def _first_argmax(vals):
    m = vals[0]
    for v in vals[1:]:
        m = jnp.maximum(m, v)
    idx = jnp.full(m.shape, len(vals) - 1, I32)
    for r in range(len(vals) - 2, -1, -1):
        idx = jnp.where(vals[r] == m, r, idx)
    return m, idx


def _route(xn, wrt_ref, rb_ref, xs_ref, bucket_ref):
    lt = lax.dot_general(wrt_ref[...], xn, (((1,), (1,)), ((), ())),
                         precision=lax.Precision.HIGHEST, preferred_element_type=F32) + rb_ref[:, 0:1]
    _, g_idx = _first_argmax([lt[r:r + 1, :] for r in range(N_GROUPS)])
    le = []
    for e in range(EXPERTS_PER_GROUP):
        row = EXPERT_ROW0 + EXPERTS_PER_GROUP * (N_GROUPS - 1) + e
        v = lt[row:row + 1, :]
        for g in range(N_GROUPS - 2, -1, -1):
            row = EXPERT_ROW0 + EXPERTS_PER_GROUP * g + e
            v = jnp.where(g_idx == g, lt[row:row + 1, :], v)
        le.append(v)
    _, i1 = _first_argmax(le)
    _, i2 = _first_argmax([jnp.where(i1 == e, -jnp.inf, le[e]) for e in range(EXPERTS_PER_GROUP)])
    lo = jnp.minimum(i1, i2)
    hi = jnp.maximum(i1, i2)
    pair = jnp.where(lo == 0, hi - 1, jnp.where(lo == 1, hi + 1, 5))
    bucket = g_idx * len(PAIRS) + pair
    for sb in range(xn.shape[0] // LANES):
        bucket_ref[sb] = bucket[:, sb * LANES:(sb + 1) * LANES]
    _store_rows(xs_ref, xn)


def _a_layer_kernel(xp_ref, xsm_ref, ag_ref, win_ref, vg_ref, ws_ref, bs_ref, wout_ref, fg_ref, wrt_ref, rb_ref,
                    h_ref, xs_ref, bucket_ref, vrow_ref, *, nps):
    i = pl.program_id(0)
    x = jnp.where(i < nps, xp_ref[...], xsm_ref[...])
    xn = _rms(x, ag_ref[...]).astype(BF16)
    z = jax.nn.gelu(jnp.dot(xn, win_ref[...], preferred_element_type=F32))
    u = z[:, :A_WIDTH]
    v = _rms(z[:, A_WIDTH:], vg_ref[...])

    @pl.when(i >= nps)
    def _():
        vrow_ref[...] = v

    vb = v.astype(BF16)
    gate = jnp.concatenate([
        jnp.concatenate(
            [jnp.dot(ws_ref[0, h], vb[sb * GMLP_BLOCK:(sb + 1) * GMLP_BLOCK, h * A_HEAD_DIM:(h + 1) * A_HEAD_DIM],
                     preferred_element_type=F32) for h in range(A_HEADS)], axis=1) + bs_ref[0]
        for sb in range(x.shape[0] // GMLP_BLOCK)], axis=0)
    s = (u * gate).astype(BF16)
    h1 = x + jnp.dot(s, wout_ref[...], preferred_element_type=F32)
    h_ref[...] = h1
    _route(_rms(h1, fg_ref[...]), wrt_ref, rb_ref, xs_ref, bucket_ref)


def _oproj_kernel(op_ref, osm_ref, h_in_ref, wo_ref, fg_ref, wrt_ref, rb_ref, h_ref, xs_ref, bucket_ref, *, nps):
    i = pl.program_id(0)
    o = jnp.where(i < nps, op_ref[...], osm_ref[...])
    h3 = h_in_ref[...] + jnp.dot(o, wo_ref[...], preferred_element_type=F32)
    h_ref[...] = h3
    _route(_rms(h3, fg_ref[...]), wrt_ref, rb_ref, xs_ref, bucket_ref)


def _rank_kernel(b_ref, pos_ref, tb_ref, lo_ref, hi_ref, *, tm, nt):
    bk = b_ref[...]
    nr = bk.shape[0]
    r_i = lax.broadcasted_iota(I32, (LANES, LANES), 0)
    c_i = lax.broadcasted_iota(I32, (LANES, LANES), 1)
    upper = (r_i <= c_i).astype(BF16)
    rr = lax.broadcasted_iota(I32, (nr, nr), 0)
    cc = lax.broadcasted_iota(I32, (nr, nr), 1)
    before_rows = (cc < rr).astype(BF16)
    lane = lax.broadcasted_iota(I32, (1, LANES), 1)
    tile_start = lane.astype(F32) * tm
    pos = jnp.zeros((nr, LANES), F32)
    seg_start = jnp.zeros((1, LANES), F32)
    tile_bucket = jnp.zeros((1, LANES), I32)
    pad_lo = jnp.zeros((1, LANES), F32)
    pad_hi = jnp.zeros((1, LANES), F32)
    for b in range(N_BUCKETS):
        m = bk == b
        pref = jnp.dot(jnp.where(m, 1.0, 0.0).astype(BF16), upper, preferred_element_type=F32)
        rowtot = jnp.broadcast_to(pref[:, LANES - 1:LANES], (nr, LANES))
        before = jnp.dot(before_rows, rowtot.astype(BF16), preferred_element_type=F32)
        cnt = jnp.sum(rowtot, axis=0, keepdims=True)
        pos = pos + jnp.where(m, seg_start + before + pref - 1.0, 0.0)
        pad_lo = jnp.where(lane == b, seg_start + cnt, pad_lo)
        seg_start = seg_start + jnp.ceil(cnt / tm) * tm
        pad_hi = jnp.where(lane == b, seg_start, pad_hi)
        tile_bucket = tile_bucket + (seg_start <= tile_start).astype(I32)
    pad_lo = jnp.where(lane == N_BUCKETS, seg_start, pad_lo)
    pad_hi = jnp.where(lane == N_BUCKETS, float(nt * tm), pad_hi)
    pos_ref[...] = pos.astype(I32)
    tb_ref[...] = tile_bucket
    lo_ref[...] = pad_lo.astype(I32)
    hi_ref[...] = pad_hi.astype(I32)


def _inverse_kernel(pos_ref, lo_ref, hi_ref, inv_ref, *, n_tok):
    def clear(r, carry):
        inv_ref[r] = -1
        return carry

    def put(t, carry):
        inv_ref[pos_ref[t]] = t
        return carry

    for b in range(N_BUCKETS + 1):
        lax.fori_loop(lo_ref[b], hi_ref[b], clear, 0)
    lax.fori_loop(0, n_tok, put, 0, unroll=8)


def _expert_kernel(telo_ref, tehi_ref, nused_ref, gsrc_ref, sdst_ref, xs_hbm, wr_ref, rb_ref, wgl_ref, wul_ref,
                   wdl_ref, wgh_ref, wuh_ref, wdh_ref, y_hbm, xbuf, ybuf, wup, wdown, gsem, ssem, zsem, *, tm):
    r = pl.program_id(0)
    nt = pl.num_programs(0)
    n_used = nused_ref[0]

    def row(k):
        return pl.ds(k * SLABS, SLABS)

    def gather(tile, slot, wait):
        for k in range(tm):
            src = 0 if wait else pl.multiple_of(gsrc_ref[tile * tm + k] * SLABS, SLABS)
            cp = pltpu.make_async_copy(xs_hbm.at[pl.ds(src, SLABS)], xbuf.at[slot, row(k)], gsem.at[slot])
            cp.wait() if wait else cp.start()

    def scatter(tile, slot, wait):
        for k in range(tm):
            dst = 0 if wait else pl.multiple_of(sdst_ref[tile * tm + k] * SLABS, SLABS)
            cp = pltpu.make_async_copy(ybuf.at[slot, row(k)], y_hbm.at[pl.ds(dst, SLABS)], ssem.at[slot])
            cp.wait() if wait else cp.start()

    @pl.when(r == 0)
    def _():
        gather(0, 0, False)

    @pl.when(r < n_used)
    def _():
        slot = r % 2

        @pl.when(r + 1 < n_used)
        def _():
            gather(r + 1, 1 - slot, False)

        prev = jnp.maximum(r - 1, 0)

        @pl.when(jnp.logical_or(r == 0, telo_ref[r] != telo_ref[prev]))
        def _():
            wup[0] = wgl_ref[0, 0].astype(BF16)
            wup[1] = wul_ref[0, 0].astype(BF16)
            wdown[0] = wdl_ref[0, 0].astype(BF16)

        @pl.when(jnp.logical_or(r == 0, tehi_ref[r] != tehi_ref[prev]))
        def _():
            wup[2] = wgh_ref[0, 0].astype(BF16)
            wup[3] = wuh_ref[0, 0].astype(BF16)
            wdown[1] = wdh_ref[0, 0].astype(BF16)

        gather(r, slot, True)

        @pl.when(r >= 2)
        def _():
            scatter(r - 2, slot, True)

        x = _load_rows(xbuf.at[slot], tm).astype(BF16)
        logits = jnp.dot(x, wr_ref[...], preferred_element_type=F32) + rb_ref[...]
        lane = lax.broadcasted_iota(I32, logits.shape, 1)
        grp = telo_ref[r] // EXPERTS_PER_GROUP
        is_group = lane < N_GROUPS
        m = jnp.max(jnp.where(is_group, logits, -jnp.inf), axis=1, keepdims=True)
        ex = jnp.exp(logits - m)
        p_g = (jnp.sum(jnp.where(lane == grp, ex, 0.0), axis=1, keepdims=True)
               / jnp.sum(jnp.where(is_group, ex, 0.0), axis=1, keepdims=True))
        l_lo = jnp.sum(jnp.where(lane == EXPERT_ROW0 + telo_ref[r], logits, 0.0), axis=1, keepdims=True)
        l_hi = jnp.sum(jnp.where(lane == EXPERT_ROW0 + tehi_ref[r], logits, 0.0), axis=1, keepdims=True)
        mm = jnp.maximum(l_lo, l_hi)
        e_lo = jnp.exp(l_lo - mm)
        e_hi = jnp.exp(l_hi - mm)
        g_lo = p_g * (e_lo / (e_lo + e_hi))
        g_hi = p_g * (e_hi / (e_lo + e_hi))

        def ffn(which, g):
            a = jnp.dot(x, wup[2 * which], preferred_element_type=F32)
            b = jnp.dot(x, wup[2 * which + 1], preferred_element_type=F32)
            hid = (jax.nn.silu(a) * b * g).astype(BF16)
            return jnp.dot(hid, wdown[which], preferred_element_type=F32)

        _store_rows(ybuf.at[slot], ffn(0, g_lo) + ffn(1, g_hi))
        scatter(r, slot, False)

    @pl.when(r >= n_used)
    def _():
        xbuf[0] = jnp.zeros(xbuf.shape[1:], xbuf.dtype)
        dst = pl.multiple_of(r * (tm * SLABS), tm * SLABS)
        cp = pltpu.make_async_copy(xbuf.at[0], y_hbm.at[pl.ds(dst, tm * SLABS)], zsem)
        cp.start()
        cp.wait()

    @pl.when(r == nt - 1)
    def _():
        @pl.when(n_used >= 2)
        def _():
            scatter(0, n_used % 2, True)

        scatter(0, (n_used - 1) % 2, True)


def _head_rms(x, g):
    return jnp.concatenate([_rms(x[:, h * HEAD_DIM:(h + 1) * HEAD_DIM], g) for h in range(N_HEADS)], axis=1)


def _kvq_kernel(h_in_ref, y_ref, kvg_ref, wkv_ref, kng_ref, bg_ref, wq_ref, qng_ref,
                h_ref, kp_ref, vp_ref, ksm_ref, vsm_ref, kb_ref, vb_ref, qb_ref, qsm_ref, *, npt):
    i = pl.program_id(0)
    tk = h_in_ref.shape[0]
    h2 = h_in_ref[...] + _load_rows(y_ref, tk)
    h_ref[...] = h2
    kv = jnp.dot(_rms(h2, kvg_ref[...]).astype(BF16), wkv_ref[...], preferred_element_type=F32)
    q = jnp.dot(_rms(h2, bg_ref[...]).astype(BF16), wq_ref[...], preferred_element_type=F32)
    k = _head_rms(kv[:, :D_MODEL], kng_ref[...])
    v = kv[:, D_MODEL:]
    q = _head_rms(q, qng_ref[...]) * (HEAD_DIM ** -0.5)

    @pl.when(i < npt)
    def _():
        _store_rows(kp_ref, k)
        _store_rows(vp_ref, v)
        for h in range(N_HEADS):
            sl = slice(h * HEAD_DIM, (h + 1) * HEAD_DIM)
            kb_ref[0, h] = k[:, sl].astype(BF16)
            vb_ref[0, h] = v[:, sl].astype(BF16)
            qb_ref[0, h] = q[:, sl].astype(BF16)

    @pl.when(i >= npt)
    def _():
        _store_rows(ksm_ref, k)
        _store_rows(vsm_ref, v)
        qsm_ref[...] = q


def _stick_block(q, k, v, r, visible, later_keys):
    z = lax.dot_general(q, k, (((1,), (1,)), ((), ())), preferred_element_type=F32)
    sp = jnp.maximum(z, 0.0) + jnp.log(1.0 + jnp.exp(-jnp.abs(z)))
    if visible is not None:
        sp = jnp.where(visible, sp, 0.0)
    sp_hi = sp.astype(BF16)
    sp_lo = (sp - sp_hi.astype(F32)).astype(BF16)
    c = (jnp.dot(sp_hi, later_keys, preferred_element_type=F32)
         + jnp.dot(sp_lo, later_keys, preferred_element_type=F32))
    w = jnp.exp(z - c - r)
    if visible is not None:
        w = jnp.where(visible, w, 0.0)
    out = jnp.dot(w.astype(BF16), v, preferred_element_type=F32)
    return out, r + c[:, 0:1]


def _later_keys(n):
    rowi = lax.broadcasted_iota(I32, (n, n), 0)
    coli = lax.broadcasted_iota(I32, (n, n), 1)
    return (rowi >= coli).astype(BF16)


def _attn_kernel(q_ref, k_ref, v_ref, o_ref, r_sc, acc_sc):
    i = pl.program_id(1)
    tq = q_ref.shape[2]
    visible = lax.broadcasted_iota(I32, (tq, tq), 1) < lax.broadcasted_iota(I32, (tq, tq), 0)
    later_keys = _later_keys(tq)

    def block(j, first):
        start = pl.multiple_of(j * tq, tq)
        rmin = None
        for h in range(N_HEADS):
            sl = slice(h * HEAD_DIM, (h + 1) * HEAD_DIM)
            r_prev = jnp.zeros((tq, 1), F32) if first else r_sc[h]
            out, r = _stick_block(q_ref[0, h], k_ref[0, h, pl.ds(start, tq), :], v_ref[0, h, pl.ds(start, tq), :],
                                  r_prev, visible if first else None, later_keys)
            if first:
                acc_sc[:, sl] = out
            else:
                acc_sc[:, sl] += out
            r_sc[h] = r
            rmin = r if rmin is None else jnp.minimum(rmin, r)
        return jnp.min(rmin)

    rmin = block(i, True)
    lax.while_loop(lambda st: jnp.logical_and(st[0] >= 0, st[1] <= STOP_MASS),
                   lambda st: (st[0] - 1, block(st[0], False)), (i - 1, rmin))
    o_ref[...] = acc_sc[...].astype(o_ref.dtype)


def _sample_attn_kernel(q_ref, kn_ref, vn_ref, ck_ref, cv_ref, ck_hbm, cv_hbm, o_ref, r_sc, acc_sc, kbuf, vbuf, sem,
                        *, nkc):
    b = pl.program_id(0)
    ds = q_ref.shape[0]
    ck = ck_ref.shape[0] // N_HEADS

    def head_rows(ref, h, n):
        return ref[pl.ds(h, n, stride=N_HEADS), :].astype(BF16)

    visible = lax.broadcasted_iota(I32, (ds, LANES), 1) < lax.broadcasted_iota(I32, (ds, LANES), 0)
    pad = jnp.zeros((LANES - ds, HEAD_DIM), BF16)
    for h in range(N_HEADS):
        sl = slice(h * HEAD_DIM, (h + 1) * HEAD_DIM)
        k = jnp.concatenate([head_rows(kn_ref, h, ds), pad], axis=0)
        v = jnp.concatenate([head_rows(vn_ref, h, ds), pad], axis=0)
        out, r = _stick_block(q_ref[:, sl].astype(BF16), k, v, jnp.zeros((ds, 1), F32), visible,
                              _later_keys(LANES))
        acc_sc[:, sl] = out
        r_sc[h] = r

    def chunk(kref, vref):
        later_keys = _later_keys(ck)
        rmin = None
        for h in range(N_HEADS):
            sl = slice(h * HEAD_DIM, (h + 1) * HEAD_DIM)
            out, r = _stick_block(q_ref[:, sl].astype(BF16), head_rows(kref, h, ck), head_rows(vref, h, ck),
                                  r_sc[h], None, later_keys)
            acc_sc[:, sl] += out
            r_sc[h] = r
            rmin = r if rmin is None else jnp.minimum(rmin, r)
        return jnp.min(rmin)

    def older(st):
        c = st[0]
        rows = ck * N_HEADS
        start = pl.multiple_of((b * nkc + c) * rows, rows)
        cpk = pltpu.make_async_copy(ck_hbm.at[pl.ds(start, rows)], kbuf, sem.at[0])
        cpv = pltpu.make_async_copy(cv_hbm.at[pl.ds(start, rows)], vbuf, sem.at[1])
        cpk.start()
        cpv.start()
        cpk.wait()
        cpv.wait()
        return c - 1, chunk(kbuf, vbuf)

    lax.while_loop(lambda st: jnp.logical_and(st[0] >= 0, st[1] <= STOP_MASS), older,
                   (nkc - 2, chunk(ck_ref, cv_ref)))
    o_ref[...] = acc_sc[...].astype(o_ref.dtype)


def _final_kernel(h_ref, y_ref, yp_ref, ysm_ref, *, npt):
    i = pl.program_id(0)
    y = h_ref[...] + _load_rows(y_ref, h_ref.shape[0])

    @pl.when(i < npt)
    def _():
        yp_ref[...] = y

    @pl.when(i >= npt)
    def _():
        ysm_ref[...] = y


def _const_spec(shape):
    return pl.BlockSpec(shape, lambda *_: (0,) * len(shape))


def _router_operands(layer, ffn_norm_g, router_g_w, router_g_b, router_e_w, router_e_b):
    n_e = N_GROUPS * EXPERTS_PER_GROUP
    wrt = jnp.zeros((ROUTER_ROWS, D_MODEL), F32)
    wrt = wrt.at[:N_GROUPS].set(router_g_w[layer].T)
    wrt = wrt.at[EXPERT_ROW0:EXPERT_ROW0 + n_e].set(router_e_w[layer].transpose(0, 2, 1).reshape(n_e, D_MODEL))
    rb = jnp.zeros((ROUTER_ROWS,), F32)
    rb = rb.at[:N_GROUPS].set(router_g_b[layer])
    rb = rb.at[EXPERT_ROW0:EXPERT_ROW0 + n_e].set(router_e_b[layer].reshape(-1))
    wr_cols = jnp.zeros((D_MODEL, LANES), F32).at[:, :ROUTER_ROWS].set(wrt.T).astype(BF16)
    rb_cols = jnp.zeros((1, LANES), F32).at[0, :ROUTER_ROWS].set(rb)
    return (ffn_norm_g[layer][None, :], wrt, jnp.broadcast_to(rb[:, None], (ROUTER_ROWS, LANES))), (wr_cols, rb_cols)


def _moe(xs_rt, bucket, layer, route_cols, w_gate, w_up, w_down):
    nblk = bucket.shape[0]
    n_tok = nblk * LANES
    tm = EXPERT_TILE
    nr = -(-nblk // LANES) * LANES
    nt = -(-(n_tok + N_BUCKETS * (tm - 1)) // tm)
    assert nt <= LANES
    bk = jnp.pad(bucket.reshape(nblk, LANES), ((0, nr - nblk), (0, 0)), constant_values=N_BUCKETS)
    lane_row = jax.ShapeDtypeStruct((1, LANES), I32)
    pos, tile_bucket, pad_lo, pad_hi = pl.pallas_call(
        functools.partial(_rank_kernel, tm=tm, nt=nt),
        out_shape=(jax.ShapeDtypeStruct((nr, LANES), I32), lane_row, lane_row, lane_row),
        name=f"moe_rank_{layer}",
    )(bk)
    pos = pos[:nblk].reshape(n_tok)
    smem = pl.BlockSpec(memory_space=pltpu.SMEM)
    inv = pl.pallas_call(
        functools.partial(_inverse_kernel, n_tok=n_tok),
        out_shape=jax.ShapeDtypeStruct((nt * tm,), I32),
        in_specs=[smem, smem, smem],
        out_specs=smem,
        name=f"moe_inverse_{layer}",
    )(pos, pad_lo[0], pad_hi[0])
    is_pad = inv < 0
    gsrc = jnp.maximum(inv, 0)
    sdst = jnp.where(is_pad, n_tok + jnp.cumsum(is_pad.astype(I32)) - 1, inv)
    tb = tile_bucket[0, :nt]
    n_used = jnp.sum((tb < N_BUCKETS).astype(I32))
    tsrc = jnp.minimum(jnp.arange(nt, dtype=I32), n_used - 1)
    tbc = jnp.minimum(tb[tsrc], N_BUCKETS - 1)
    pair_lo = jnp.array([p[0] for p in PAIRS], I32)
    pair_hi = jnp.array([p[1] for p in PAIRS], I32)
    grp = tbc // len(PAIRS)
    telo = grp * EXPERTS_PER_GROUP + pair_lo[tbc % len(PAIRS)]
    tehi = grp * EXPERTS_PER_GROUP + pair_hi[tbc % len(PAIRS)]

    def wspec(shape, which):
        return pl.BlockSpec((1, 1) + shape, lambda r, lo, hi, *_: (layer, (lo, hi)[which][r], 0, 0))

    up_shape = (D_MODEL, EXPERT_DIM)
    down_shape = (EXPERT_DIM, D_MODEL)
    wr_cols, rb_cols = route_cols
    buf = pltpu.VMEM((2, tm * SLABS, LANES), F32)
    dma2 = pltpu.SemaphoreType.DMA((2,))
    return pl.pallas_call(
        functools.partial(_expert_kernel, tm=tm),
        out_shape=jax.ShapeDtypeStruct((nt * tm * SLABS, LANES), F32),
        grid_spec=pltpu.PrefetchScalarGridSpec(
            num_scalar_prefetch=5, grid=(nt,),
            in_specs=[pl.BlockSpec(memory_space=pl.ANY),
                      pl.BlockSpec((D_MODEL, LANES), lambda r, *_: (0, 0)),
                      pl.BlockSpec((1, LANES), lambda r, *_: (0, 0)),
                      wspec(up_shape, 0), wspec(up_shape, 0), wspec(down_shape, 0),
                      wspec(up_shape, 1), wspec(up_shape, 1), wspec(down_shape, 1)],
            out_specs=pl.BlockSpec(memory_space=pl.ANY),
            scratch_shapes=[buf, buf, pltpu.VMEM((4,) + up_shape, BF16), pltpu.VMEM((2,) + down_shape, BF16),
                            dma2, dma2, pltpu.SemaphoreType.DMA(())]),
        compiler_params=_cparams("arbitrary"),
        name=f"moe_experts_{layer}",
    )(telo, tehi, n_used.reshape(1), gsrc, sdst, xs_rt, wr_cols, rb_cols,
      w_gate, w_up, w_down, w_gate, w_up, w_down)


def kernel(x_prompt, x_sample, cache_k, cache_v, a_norm_g, a_w_in, a_v_norm_g, a_w_s, a_b_s, a_w_out, kv_norm_g, w_kv, k_norm_g, b_norm_g, b_w_q, q_norm_g, b_w_o, ffn_norm_g, router_g_w, router_g_b, router_e_w, router_e_b, moe_w_gate, moe_w_up, moe_w_down):
    bsz, seq, _ = x_prompt.shape
    dbsz, dseq, _ = x_sample.shape
    past = cache_k.shape[1]
    tp, ts = bsz * seq, dbsz * dseq
    n_tok = tp + ts
    tb = TOKEN_BLOCK
    gb = GMLP_BLOCK
    assert seq % ATTN_TILE == 0 and tp % KVQ_TILE == 0 and ts % KVQ_TILE == 0 and tp % tb == 0 and ts % tb == 0
    assert gb % dseq == 0 and dseq <= CHUNK and past % ATTN_TILE == 0
    nps, nss = tp // tb, ts // tb
    xp = x_prompt.reshape(tp, D_MODEL)
    xsm = x_sample.reshape(ts, D_MODEL)

    def pstep(i):
        return jnp.minimum(i, nps - 1)

    def sstep(i):
        return jnp.maximum(i - nps, 0)

    def rt_shape(n):
        return jax.ShapeDtypeStruct((n * SLABS, LANES), F32)

    def rt_spec(rows, index):
        return pl.BlockSpec((rows * SLABS, LANES), lambda *i: (index(*i), 0))

    tok_spec = pl.BlockSpec((tb, D_MODEL), lambda i: (i, 0))
    route_in_specs = [_const_spec((1, D_MODEL)), _const_spec((ROUTER_ROWS, D_MODEL)),
                      _const_spec((ROUTER_ROWS, LANES))]
    route_out_shapes = (jax.ShapeDtypeStruct((n_tok, D_MODEL), F32), rt_shape(n_tok),
                        jax.ShapeDtypeStruct((n_tok // LANES, 1, LANES), I32))
    route_out_specs = [tok_spec, rt_spec(tb, lambda i: i), pl.BlockSpec((tb // LANES, 1, LANES), lambda i: (i, 0, 0))]

    pos_i = jnp.arange(gb)
    mask = (pos_i[None, :] // CHUNK) <= (pos_i[:, None] // CHUNK)
    w_prompt = jnp.where(mask, a_w_s[0], 0.0)
    rep = gb // dseq
    w_sample = jnp.einsum("ij,hts->hitjs", jnp.eye(rep, dtype=F32), w_prompt[:, :dseq, :dseq]).reshape(
        A_HEADS, gb, gb)
    ws_all = jnp.stack([w_prompt, w_sample]).astype(BF16)
    b_prompt = a_b_s[0]
    b_sample = jnp.tile(a_b_s[0][:, :dseq], (1, rep))
    bs_all = jnp.stack([jnp.repeat(b.T, A_HEAD_DIM, axis=1) for b in (b_prompt, b_sample)])
    route0, cols0 = _router_operands(0, ffn_norm_g, router_g_w, router_g_b, router_e_w, router_e_b)
    h1, xs1, bucket1, v_rows = pl.pallas_call(
        functools.partial(_a_layer_kernel, nps=nps),
        out_shape=route_out_shapes + (jax.ShapeDtypeStruct((ts, A_WIDTH), F32),),
        grid=(nps + nss,),
        in_specs=[pl.BlockSpec((tb, D_MODEL), lambda i: (pstep(i), 0)),
                  pl.BlockSpec((tb, D_MODEL), lambda i: (sstep(i), 0)),
                  _const_spec((1, D_MODEL)), _const_spec((D_MODEL, 2 * A_WIDTH)), _const_spec((1, A_WIDTH)),
                  pl.BlockSpec((1, A_HEADS, gb, gb), lambda i: (i // nps, 0, 0, 0)),
                  pl.BlockSpec((1, gb, A_WIDTH), lambda i: (i // nps, 0, 0)),
                  _const_spec((A_WIDTH, D_MODEL))] + route_in_specs,
        out_specs=route_out_specs + [pl.BlockSpec((tb, A_WIDTH), lambda i: (sstep(i), 0))],
        compiler_params=_cparams("arbitrary"),
        name="a_layer",
    )(xp, xsm, a_norm_g[0][None, :], a_w_in[0].astype(BF16), a_v_norm_g[0][None, :], ws_all, bs_all,
      a_w_out[0].astype(BF16), *route0)
    y1 = _moe(xs1, bucket1, 0, cols0, moe_w_gate, moe_w_up, moe_w_down)

    tk = KVQ_TILE
    npt, nst = tp // tk, ts // tk
    spt = seq // tk

    def ptile(i):
        return jnp.minimum(i, npt - 1)

    def stile(i):
        return jnp.maximum(i - npt, 0)

    row_spec = pl.BlockSpec((tk, D_MODEL), lambda i: (i, 0))
    prow_spec = pl.BlockSpec((tk, D_MODEL), lambda i: (ptile(i), 0))
    srow_spec = pl.BlockSpec((tk, D_MODEL), lambda i: (stile(i), 0))
    head_spec = pl.BlockSpec((1, N_HEADS, tk, HEAD_DIM), lambda i: (ptile(i) // spt, 0, ptile(i) % spt, 0))
    f32_rows = lambda n: jax.ShapeDtypeStruct((n, D_MODEL), F32)
    head_major = jax.ShapeDtypeStruct((bsz, N_HEADS, seq, HEAD_DIM), BF16)
    h2, k_p, v_p, k_s, v_s, kb, vb, qb, q_s = pl.pallas_call(
        functools.partial(_kvq_kernel, npt=npt),
        out_shape=(f32_rows(n_tok), rt_shape(tp), rt_shape(tp), rt_shape(ts), rt_shape(ts),
                   head_major, head_major, head_major, f32_rows(ts)),
        grid=(npt + nst,),
        in_specs=[row_spec, rt_spec(tk, lambda i: i), _const_spec((1, D_MODEL)),
                  _const_spec((D_MODEL, 2 * D_MODEL)), _const_spec((1, HEAD_DIM)), _const_spec((1, D_MODEL)),
                  _const_spec((D_MODEL, D_MODEL)), _const_spec((1, HEAD_DIM))],
        out_specs=[row_spec, rt_spec(tk, ptile), rt_spec(tk, ptile), rt_spec(tk, stile), rt_spec(tk, stile),
                   head_spec, head_spec, head_spec, srow_spec],
        compiler_params=_cparams("arbitrary"),
        name="kvq",
    )(h1, y1, kv_norm_g[None, :], w_kv.astype(BF16), k_norm_g[None, :], b_norm_g[0][None, :],
      b_w_q[0].astype(BF16), q_norm_g[0][None, :])

    tq = ATTN_TILE
    nq = seq // tq
    qspec = pl.BlockSpec((1, N_HEADS, tq, HEAD_DIM), lambda b, i: (b, 0, i, 0))
    kvspec = pl.BlockSpec((1, N_HEADS, seq, HEAD_DIM), lambda b, i: (b, 0, 0, 0), pipeline_mode=pl.Buffered(1))
    o_p = pl.pallas_call(
        _attn_kernel,
        out_shape=jax.ShapeDtypeStruct((tp, D_MODEL), BF16),
        grid=(bsz, nq),
        in_specs=[qspec, kvspec, kvspec],
        out_specs=pl.BlockSpec((tq, D_MODEL), lambda b, i: (b * nq + i, 0)),
        scratch_shapes=[pltpu.VMEM((N_HEADS, tq, 1), F32), pltpu.VMEM((tq, D_MODEL), F32)],
        compiler_params=_cparams("arbitrary", "arbitrary"),
        name="attn_prompt",
    )(qb, kb, vb)

    ck = ATTN_TILE
    nkc = past // ck
    new_spec = pl.BlockSpec((dseq, D_MODEL), lambda b: (b, 0))
    new_rt_spec = rt_spec(dseq, lambda b: b)
    newest_spec = rt_spec(ck, lambda b: b * nkc + nkc - 1)
    any_spec = pl.BlockSpec(memory_space=pl.ANY)
    cache_k2 = cache_k.reshape(dbsz * past * N_HEADS, HEAD_DIM)
    cache_v2 = cache_v.reshape(dbsz * past * N_HEADS, HEAD_DIM)
    chunk_buf = pltpu.VMEM((ck * N_HEADS, HEAD_DIM), F32)
    o_s = pl.pallas_call(
        functools.partial(_sample_attn_kernel, nkc=nkc),
        out_shape=jax.ShapeDtypeStruct((ts, D_MODEL), BF16),
        grid=(dbsz,),
        in_specs=[new_spec, new_rt_spec, new_rt_spec, newest_spec, newest_spec, any_spec, any_spec],
        out_specs=new_spec,
        scratch_shapes=[pltpu.VMEM((N_HEADS, dseq, 1), F32), pltpu.VMEM((dseq, D_MODEL), F32), chunk_buf, chunk_buf,
                        pltpu.SemaphoreType.DMA((2,))],
        compiler_params=_cparams("arbitrary"),
        name="attn_sample",
    )(q_s, k_s, v_s, cache_k2, cache_v2, cache_k2, cache_v2)

    route1, cols1 = _router_operands(1, ffn_norm_g, router_g_w, router_g_b, router_e_w, router_e_b)
    h3, xs2, bucket2 = pl.pallas_call(
        functools.partial(_oproj_kernel, nps=nps),
        out_shape=route_out_shapes,
        grid=(nps + nss,),
        in_specs=[pl.BlockSpec((tb, D_MODEL), lambda i: (pstep(i), 0)),
                  pl.BlockSpec((tb, D_MODEL), lambda i: (sstep(i), 0)),
                  tok_spec, _const_spec((D_MODEL, D_MODEL))] + route_in_specs,
        out_specs=route_out_specs,
        compiler_params=_cparams("arbitrary"),
        name="o_proj",
    )(o_p, o_s, h2, b_w_o[0].astype(BF16), *route1)
    y2 = _moe(xs2, bucket2, 1, cols1, moe_w_gate, moe_w_up, moe_w_down)

    y_p, y_s = pl.pallas_call(
        functools.partial(_final_kernel, npt=npt),
        out_shape=(f32_rows(tp), f32_rows(ts)),
        grid=(npt + nst,),
        in_specs=[row_spec, rt_spec(tk, lambda i: i)],
        out_specs=[prow_spec, srow_spec],
        compiler_params=_cparams("arbitrary"),
        name="final_residual",
    )(h3, y2)

    kv_shape_p = (bsz, seq, N_HEADS, HEAD_DIM)
    kv_shape_s = (dbsz, dseq, N_HEADS, HEAD_DIM)
    return (y_p.reshape(bsz, seq, D_MODEL), y_s.reshape(dbsz, dseq, D_MODEL),
            k_p.reshape(kv_shape_p), v_p.reshape(kv_shape_p), k_s.reshape(kv_shape_s), v_s.reshape(kv_shape_s),
            v_rows.reshape(1, dbsz, dseq, A_WIDTH))
```

```python
import functools

import jax
import jax.numpy as jnp
from jax import lax
from jax.experimental import pallas as pl
from jax.experimental.pallas import tpu as pltpu

F32 = jnp.float32
BF16 = jnp.bfloat16
I32 = jnp.int32

D_MODEL = 1024
CHUNK = 64
GMLP_BLOCK = 128
A_WIDTH = 2 * D_MODEL
A_HEADS = 8
A_HEAD_DIM = A_WIDTH // A_HEADS
N_HEADS = 8
HEAD_DIM = D_MODEL // N_HEADS
N_GROUPS = 4
EXPERTS_PER_GROUP = 4
EXPERT_DIM = D_MODEL // 2
EPS = 1e-6

LANES = 128
SUBLANES = 8
MXU_WIDTH = 256
SLABS = D_MODEL // LANES
assert SLABS == SUBLANES and N_HEADS == SLABS and HEAD_DIM == LANES
PAIRS = ((0, 1), (0, 2), (0, 3), (1, 2), (1, 3), (2, 3))
N_BUCKETS = N_GROUPS * len(PAIRS)
ROUTER_ROWS = 32
EXPERT_ROW0 = 8

TOKEN_BLOCK = 2 * GMLP_BLOCK
EXPERT_TILE = 256
KVQ_TILE = 256
ATTN_TILE = 256
VMEM_LIMIT = 56 * 1024 * 1024
STOP_MASS = 105.0


def _cparams(*sem):
    return pltpu.CompilerParams(dimension_semantics=sem, vmem_limit_bytes=VMEM_LIMIT)


def _rms(x, g):
    ms = jnp.mean(x * x, axis=-1, keepdims=True)
    return x * lax.rsqrt(ms + EPS) * g


def _load_rows(ref, n):
    return jnp.concatenate([ref[pl.ds(c, n, stride=SLABS), :] for c in range(SLABS)], axis=1)


def _store_rows(ref, x):
    n = x.shape[0]
    for c in range(SLABS):
        ref[pl.ds(c, n, stride=SLABS), :] = x[:, c * LANES:(c + 1) * LANES]


def _first_argmax(vals):
    m = vals[0]
    for v in vals[1:]:
        m = jnp.maximum(m, v)
    idx = jnp.full(m.shape, len(vals) - 1, I32)
    for r in range(len(vals) - 2, -1, -1):
        idx = jnp.where(vals[r] == m, r, idx)
    return m, idx


def _route(xn, wrt_ref, rb_ref, xs_ref, bucket_ref):
    lt = lax.dot_general(wrt_ref[...], xn, (((1,), (1,)), ((), ())),
                         precision=lax.Precision.HIGHEST, preferred_element_type=F32) + rb_ref[:, 0:1]
    _, g_idx = _first_argmax([lt[r:r + 1, :] for r in range(N_GROUPS)])
    le = []
    for e in range(EXPERTS_PER_GROUP):
        row = EXPERT_ROW0 + EXPERTS_PER_GROUP * (N_GROUPS - 1) + e
        v = lt[row:row + 1, :]
        for g in range(N_GROUPS - 2, -1, -1):
            row = EXPERT_ROW0 + EXPERTS_PER_GROUP * g + e
            v = jnp.where(g_idx == g, lt[row:row + 1, :], v)
        le.append(v)
    _, i1 = _first_argmax(le)
    _, i2 = _first_argmax([jnp.where(i1 == e, -jnp.inf, le[e]) for e in range(EXPERTS_PER_GROUP)])
    lo = jnp.minimum(i1, i2)
    hi = jnp.maximum(i1, i2)
    pair = jnp.where(lo == 0, hi - 1, jnp.where(lo == 1, hi + 1, 5))
    bucket = g_idx * len(PAIRS) + pair
    for sb in range(xn.shape[0] // LANES):
        bucket_ref[sb] = bucket[:, sb * LANES:(sb + 1) * LANES]
    _store_rows(xs_ref, xn)


def _a_layer_kernel(xp_ref, xsm_ref, ag_ref, win_ref, vg_ref, ws_ref, bs_ref, wout_ref, fg_ref, wrt_ref, rb_ref,
                    h_ref, xs_ref, bucket_ref, vrow_ref, *, nps):
    i = pl.program_id(0)
    x = jnp.where(i < nps, xp_ref[...], xsm_ref[...])
    xn = _rms(x, ag_ref[...]).astype(BF16)
    z = jax.nn.gelu(jnp.dot(xn, win_ref[...], preferred_element_type=F32))
    u = z[:, :A_WIDTH]
    v = _rms(z[:, A_WIDTH:], vg_ref[...])

    @pl.when(i >= nps)
    def _():
        vrow_ref[...] = v

    vb = v.astype(BF16)
    gate = jnp.concatenate([
        jnp.concatenate(
            [jnp.dot(ws_ref[0, h], vb[sb * GMLP_BLOCK:(sb + 1) * GMLP_BLOCK, h * A_HEAD_DIM:(h + 1) * A_HEAD_DIM],
                     preferred_element_type=F32) for h in range(A_HEADS)], axis=1) + bs_ref[0]
        for sb in range(x.shape[0] // GMLP_BLOCK)], axis=0)
    s = (u * gate).astype(BF16)
    h1 = x + jnp.dot(s, wout_ref[...], preferred_element_type=F32)
    h_ref[...] = h1
    _route(_rms(h1, fg_ref[...]), wrt_ref, rb_ref, xs_ref, bucket_ref)


def _oproj_kernel(op_ref, osm_ref, h_in_ref, wo_ref, fg_ref, wrt_ref, rb_ref, h_ref, xs_ref, bucket_ref, *, nps):
    i = pl.program_id(0)
    o = jnp.where(i < nps, op_ref[...], osm_ref[...])
    h3 = h_in_ref[...] + jnp.dot(o, wo_ref[...], preferred_element_type=F32)
    h_ref[...] = h3
    _route(_rms(h3, fg_ref[...]), wrt_ref, rb_ref, xs_ref, bucket_ref)


def _rank_kernel(b_ref, pos_ref, tb_ref, lo_ref, hi_ref, *, tm, nt):
    bk = b_ref[...]
    nr = bk.shape[0]
    r_i = lax.broadcasted_iota(I32, (LANES, LANES), 0)
    c_i = lax.broadcasted_iota(I32, (LANES, LANES), 1)
    upper = (r_i <= c_i).astype(BF16)
    rr = lax.broadcasted_iota(I32, (nr, nr), 0)
    cc = lax.broadcasted_iota(I32, (nr, nr), 1)
    before_rows = (cc < rr).astype(BF16)
    lane = lax.broadcasted_iota(I32, (1, LANES), 1)
    tile_start = lane.astype(F32) * tm
    pos = jnp.zeros((nr, LANES), F32)
    seg_start = jnp.zeros((1, LANES), F32)
    tile_bucket = jnp.zeros((1, LANES), I32)
    pad_lo = jnp.zeros((1, LANES), F32)
    pad_hi = jnp.zeros((1, LANES), F32)
    for b in range(N_BUCKETS):
        m = bk == b
        pref = jnp.dot(jnp.where(m, 1.0, 0.0).astype(BF16), upper, preferred_element_type=F32)
        rowtot = jnp.broadcast_to(pref[:, LANES - 1:LANES], (nr, LANES))
        before = jnp.dot(before_rows, rowtot.astype(BF16), preferred_element_type=F32)
        cnt = jnp.sum(rowtot, axis=0, keepdims=True)
        pos = pos + jnp.where(m, seg_start + before + pref - 1.0, 0.0)
        pad_lo = jnp.where(lane == b, seg_start + cnt, pad_lo)
        seg_start = seg_start + jnp.ceil(cnt / tm) * tm
        pad_hi = jnp.where(lane == b, seg_start, pad_hi)
        tile_bucket = tile_bucket + (seg_start <= tile_start).astype(I32)
    pad_lo = jnp.where(lane == N_BUCKETS, seg_start, pad_lo)
    pad_hi = jnp.where(lane == N_BUCKETS, float(nt * tm), pad_hi)
    pos_ref[...] = pos.astype(I32)
    tb_ref[...] = tile_bucket
    lo_ref[...] = pad_lo.astype(I32)
    hi_ref[...] = pad_hi.astype(I32)


def _inverse_kernel(pos_ref, lo_ref, hi_ref, inv_ref, *, n_tok):
    def clear(r, carry):
        inv_ref[r] = -1
        return carry

    def put(t, carry):
        inv_ref[pos_ref[t]] = t
        return carry

    for b in range(N_BUCKETS + 1):
        lax.fori_loop(lo_ref[b], hi_ref[b], clear, 0)
    lax.fori_loop(0, n_tok, put, 0, unroll=8)


def _expert_kernel(telo_ref, tehi_ref, nused_ref, gsrc_ref, sdst_ref, xs_hbm, wr_ref, rb_ref, wgl_ref, wul_ref,
                   wdl_ref, wgh_ref, wuh_ref, wdh_ref, y_hbm, xbuf, ybuf, wup, wdown, gsem, ssem, zsem, *, tm):
    r = pl.program_id(0)
    nt = pl.num_programs(0) - 1
    n_used = nused_ref[0]
    slot = r % 2
    other = 1 - slot

    def row(k):
        return pl.ds(k * SLABS, SLABS)

    def gather_copy(k, tile, dst_slot, wait=False):
        src = 0 if wait else pl.multiple_of(gsrc_ref[tile * tm + k] * SLABS, SLABS)
        return pltpu.make_async_copy(xs_hbm.at[pl.ds(src, SLABS)], xbuf.at[dst_slot, row(k)], gsem.at[dst_slot])

    def scatter_copy(k, tile, src_slot, wait=False):
        dst = 0 if wait else pl.multiple_of(sdst_ref[(tile + 1) * tm + k] * SLABS, SLABS)
        return pltpu.make_async_copy(ybuf.at[src_slot, row(k)], y_hbm.at[pl.ds(dst, SLABS)], ssem.at[src_slot])

    @pl.when(r == 0)
    def _():
        ybuf[1] = jnp.zeros(ybuf.shape[1:], ybuf.dtype)
        for k in range(tm):
            gather_copy(k, 0, 0).start(priority=k % 2)

    @pl.when(jnp.logical_and(r >= 1, r <= n_used + 1))
    def _():
        for k in range(tm):
            scatter_copy(k, 0, slot, wait=True).wait()

    @pl.when(r <= n_used)
    def _():
        for k in range(tm):
            gather_copy(k, 0, slot, wait=True).wait()

    @pl.when(r < n_used)
    def _():
        nxt = jnp.minimum(r + 1, nt - 1)
        prev = jnp.maximum(r - 1, 0)

        @pl.when(jnp.logical_or(r == 0, telo_ref[r] != telo_ref[prev]))
        def _():
            wup[0] = wgl_ref[0, 0].astype(BF16)
            wup[1] = wul_ref[0, 0].astype(BF16)
            wdown[0] = wdl_ref[0, 0].astype(BF16)

        @pl.when(jnp.logical_or(r == 0, tehi_ref[r] != tehi_ref[prev]))
        def _():
            wup[2] = wgh_ref[0, 0].astype(BF16)
            wup[3] = wuh_ref[0, 0].astype(BF16)
            wdown[1] = wdh_ref[0, 0].astype(BF16)

        def issue(part, parts):
            for k in range(part * tm // parts, (part + 1) * tm // parts):
                gather_copy(k, nxt, other).start(priority=k % 2)
                scatter_copy(k, r - 1, other).start(priority=(k + 1) % 2)

        x = _load_rows(xbuf.at[slot], tm).astype(BF16)
        logits = jnp.dot(x, wr_ref[...], preferred_element_type=F32) + rb_ref[...]
        lane = lax.broadcasted_iota(I32, logits.shape, 1)
        grp = telo_ref[r] // EXPERTS_PER_GROUP
        is_group = lane < N_GROUPS
        m = jnp.max(jnp.where(is_group, logits, -jnp.inf), axis=1, keepdims=True)
        ex = jnp.exp(logits - m)
        p_g = (jnp.sum(jnp.where(lane == grp, ex, 0.0), axis=1, keepdims=True)
               / jnp.sum(jnp.where(is_group, ex, 0.0), axis=1, keepdims=True))
        l_lo = jnp.sum(jnp.where(lane == EXPERT_ROW0 + telo_ref[r], logits, 0.0), axis=1, keepdims=True)
        l_hi = jnp.sum(jnp.where(lane == EXPERT_ROW0 + tehi_ref[r], logits, 0.0), axis=1, keepdims=True)
        mm = jnp.maximum(l_lo, l_hi)
        e_lo = jnp.exp(l_lo - mm)
        e_hi = jnp.exp(l_hi - mm)
        gates = (p_g * (e_lo / (e_lo + e_hi)), p_g * (e_hi / (e_lo + e_hi)))

        n_chunks = EXPERT_DIM // MXU_WIDTH
        y = None
        for which in range(2):
            hid = []
            for c in range(n_chunks):
                issue(which * n_chunks + c, 2 * n_chunks)
                cs = slice(c * MXU_WIDTH, (c + 1) * MXU_WIDTH)
                a = jnp.dot(x, wup[2 * which, :, cs], preferred_element_type=F32)
                b = jnp.dot(x, wup[2 * which + 1, :, cs], preferred_element_type=F32)
                hid.append((jax.nn.silu(a) * b * gates[which]).astype(BF16))
            part = jnp.dot(jnp.concatenate(hid, axis=1), wdown[which], preferred_element_type=F32)
            y = part if y is None else y + part
        _store_rows(ybuf.at[slot], y)

    @pl.when(r == n_used)
    def _():
        for k in range(tm):
            scatter_copy(k, r - 1, other).start(priority=k % 2)

    @pl.when(jnp.logical_and(r >= n_used, r < nt))
    def _():
        xbuf[0] = jnp.zeros(xbuf.shape[1:], xbuf.dtype)
        dst = pl.multiple_of(r * (tm * SLABS), tm * SLABS)
        cp = pltpu.make_async_copy(xbuf.at[0], y_hbm.at[pl.ds(dst, tm * SLABS)], zsem)
        cp.start()
        cp.wait()

    @pl.when(jnp.logical_and(r == nt, n_used == nt))
    def _():
        for k in range(tm):
            scatter_copy(k, 0, other, wait=True).wait()


def _head_rms(x, g):
    return jnp.concatenate([_rms(x[:, h * HEAD_DIM:(h + 1) * HEAD_DIM], g) for h in range(N_HEADS)], axis=1)


def _kvq_kernel(h_in_ref, y_ref, kvg_ref, wkv_ref, kng_ref, bg_ref, wq_ref, qng_ref,
                h_ref, kp_ref, vp_ref, ksm_ref, vsm_ref, kb_ref, vb_ref, qb_ref, qsm_ref, *, npt):
    i = pl.program_id(0)
    tk = h_in_ref.shape[0]
    h2 = h_in_ref[...] + _load_rows(y_ref, tk)
    h_ref[...] = h2
    kv = jnp.dot(_rms(h2, kvg_ref[...]).astype(BF16), wkv_ref[...], preferred_element_type=F32)
    q = jnp.dot(_rms(h2, bg_ref[...]).astype(BF16), wq_ref[...], preferred_element_type=F32)
    k = _head_rms(kv[:, :D_MODEL], kng_ref[...])
    v = kv[:, D_MODEL:]
    q = _head_rms(q, qng_ref[...]) * (HEAD_DIM ** -0.5)

    @pl.when(i < npt)
    def _():
        _store_rows(kp_ref, k)
        _store_rows(vp_ref, v)
        for h in range(N_HEADS):
            sl = slice(h * HEAD_DIM, (h + 1) * HEAD_DIM)
            kb_ref[0, h] = k[:, sl].astype(BF16)
            vb_ref[0, h] = v[:, sl].astype(BF16)
            qb_ref[0, h] = q[:, sl].astype(BF16)

    @pl.when(i >= npt)
    def _():
        _store_rows(ksm_ref, k)
        _store_rows(vsm_ref, v)
        qsm_ref[...] = q


def _stick_blocks(qs, ks, vs, rs, visible, later_keys):
    n = len(qs)
    zs = [lax.dot_general(qs[h], ks[h], (((1,), (1,)), ((), ())), preferred_element_type=F32) for h in range(n)]
    sps = [jnp.maximum(z, 0.0) + jnp.log(1.0 + jnp.exp(-jnp.abs(z))) for z in zs]
    if visible is not None:
        sps = [jnp.where(visible, sp, 0.0) for sp in sps]
    his = [sp.astype(BF16) for sp in sps]
    los = [(sp - hi.astype(F32)).astype(BF16) for sp, hi in zip(sps, his)]
    cs = [jnp.dot(hi, later_keys, preferred_element_type=F32) + jnp.dot(lo, later_keys, preferred_element_type=F32)
          for hi, lo in zip(his, los)]
    ws = [jnp.exp(zs[h] - cs[h] - rs[h]) for h in range(n)]
    if visible is not None:
        ws = [jnp.where(visible, w, 0.0) for w in ws]
    outs = [jnp.dot(ws[h].astype(BF16), vs[h], preferred_element_type=F32) for h in range(n)]
    return outs, [rs[h] + cs[h][:, 0:1] for h in range(n)]


def _later_keys(n):
    rowi = lax.broadcasted_iota(I32, (n, n), 0)
    coli = lax.broadcasted_iota(I32, (n, n), 1)
    return (rowi >= coli).astype(BF16)


def _min_over(rs):
    m = rs[0]
    for r in rs[1:]:
        m = jnp.minimum(m, r)
    return jnp.min(m)


def _attn_kernel(q_ref, k_ref, v_ref, o_ref, r_sc, acc_sc):
    i = pl.program_id(1)
    tq = q_ref.shape[2]
    visible = lax.broadcasted_iota(I32, (tq, tq), 1) < lax.broadcasted_iota(I32, (tq, tq), 0)
    later_keys = _later_keys(tq)
    heads = range(N_HEADS)

    def block(j, first):
        start = pl.multiple_of(j * tq, tq)
        qs = [q_ref[0, h] for h in heads]
        ks = [k_ref[0, h, pl.ds(start, tq), :] for h in heads]
        vs = [v_ref[0, h, pl.ds(start, tq), :] for h in heads]
        rs = [jnp.zeros((tq, 1), F32) if first else r_sc[h] for h in heads]
        outs, rs = _stick_blocks(qs, ks, vs, rs, visible if first else None, later_keys)
        acc = jnp.concatenate(outs, axis=1)
        if first:
            acc_sc[...] = acc
        else:
            acc_sc[...] += acc
        for h in heads:
            r_sc[h] = rs[h]
        return _min_over(rs)

    rmin = block(i, True)
    lax.while_loop(lambda st: jnp.logical_and(st[0] >= 0, st[1] <= STOP_MASS),
                   lambda st: (st[0] - 1, block(st[0], False)), (i - 1, rmin))
    o_ref[...] = acc_sc[...].astype(o_ref.dtype)


def _sample_attn_kernel(q_ref, kn_ref, vn_ref, ck_ref, cv_ref, ck_hbm, cv_hbm, o_ref, r_sc, acc_sc, kbuf, vbuf, sem,
                        *, nkc):
    b = pl.program_id(0)
    ds = q_ref.shape[0]
    ck = ck_ref.shape[0] // N_HEADS
    heads = range(N_HEADS)

    def head_rows(ref, h, n):
        return ref[pl.ds(h, n, stride=N_HEADS), :].astype(BF16)

    qs = [q_ref[:, h * HEAD_DIM:(h + 1) * HEAD_DIM].astype(BF16) for h in heads]
    visible = lax.broadcasted_iota(I32, (ds, LANES), 1) < lax.broadcasted_iota(I32, (ds, LANES), 0)
    pad = jnp.zeros((LANES - ds, HEAD_DIM), BF16)
    outs, rs = _stick_blocks(qs, [jnp.concatenate([head_rows(kn_ref, h, ds), pad], axis=0) for h in heads],
                             [jnp.concatenate([head_rows(vn_ref, h, ds), pad], axis=0) for h in heads],
                             [jnp.zeros((ds, 1), F32)] * N_HEADS, visible, _later_keys(LANES))
    acc_sc[...] = jnp.concatenate(outs, axis=1)
    for h in heads:
        r_sc[h] = rs[h]

    def chunk(kref, vref):
        outs, rs = _stick_blocks(qs, [head_rows(kref, h, ck) for h in heads], [head_rows(vref, h, ck) for h in heads],
                                 [r_sc[h] for h in heads], None, _later_keys(ck))
        acc_sc[...] += jnp.concatenate(outs, axis=1)
        for h in heads:
            r_sc[h] = rs[h]
        return _min_over(rs)

    def older(st):
        c = st[0]
        rows = ck * N_HEADS
        start = pl.multiple_of((b * nkc + c) * rows, rows)
        cpk = pltpu.make_async_copy(ck_hbm.at[pl.ds(start, rows)], kbuf, sem.at[0])
        cpv = pltpu.make_async_copy(cv_hbm.at[pl.ds(start, rows)], vbuf, sem.at[1])
        cpk.start()
        cpv.start()
        cpk.wait()
        cpv.wait()
        return c - 1, chunk(kbuf, vbuf)

    lax.while_loop(lambda st: jnp.logical_and(st[0] >= 0, st[1] <= STOP_MASS), older,
                   (nkc - 2, chunk(ck_ref, cv_ref)))
    o_ref[...] = acc_sc[...].astype(o_ref.dtype)


def _final_kernel(h_ref, y_ref, yp_ref, ysm_ref, *, npt):
    i = pl.program_id(0)
    y = h_ref[...] + _load_rows(y_ref, h_ref.shape[0])

    @pl.when(i < npt)
    def _():
        yp_ref[...] = y

    @pl.when(i >= npt)
    def _():
        ysm_ref[...] = y


def _const_spec(shape):
    return pl.BlockSpec(shape, lambda *_: (0,) * len(shape))


def _router_operands(layer, ffn_norm_g, router_g_w, router_g_b, router_e_w, router_e_b):
    n_e = N_GROUPS * EXPERTS_PER_GROUP
    wrt = jnp.zeros((ROUTER_ROWS, D_MODEL), F32)
    wrt = wrt.at[:N_GROUPS].set(router_g_w[layer].T)
    wrt = wrt.at[EXPERT_ROW0:EXPERT_ROW0 + n_e].set(router_e_w[layer].transpose(0, 2, 1).reshape(n_e, D_MODEL))
    rb = jnp.zeros((ROUTER_ROWS,), F32)
    rb = rb.at[:N_GROUPS].set(router_g_b[layer])
    rb = rb.at[EXPERT_ROW0:EXPERT_ROW0 + n_e].set(router_e_b[layer].reshape(-1))
    wr_cols = jnp.zeros((D_MODEL, LANES), F32).at[:, :ROUTER_ROWS].set(wrt.T).astype(BF16)
    rb_cols = jnp.zeros((1, LANES), F32).at[0, :ROUTER_ROWS].set(rb)
    return (ffn_norm_g[layer][None, :], wrt, jnp.broadcast_to(rb[:, None], (ROUTER_ROWS, LANES))), (wr_cols, rb_cols)


def _moe(xs_rt, bucket, layer, route_cols, w_gate, w_up, w_down):
    nblk = bucket.shape[0]
    n_tok = nblk * LANES
    tm = EXPERT_TILE
    nr = -(-nblk // LANES) * LANES
    nt = -(-(n_tok + N_BUCKETS * (tm - 1)) // tm)
    assert nt <= LANES
    bk = jnp.pad(bucket.reshape(nblk, LANES), ((0, nr - nblk), (0, 0)), constant_values=N_BUCKETS)
    lane_row = jax.ShapeDtypeStruct((1, LANES), I32)
    pos, tile_bucket, pad_lo, pad_hi = pl.pallas_call(
        functools.partial(_rank_kernel, tm=tm, nt=nt),
        out_shape=(jax.ShapeDtypeStruct((nr, LANES), I32), lane_row, lane_row, lane_row),
        name=f"moe_rank_{layer}",
    )(bk)
    pos = pos[:nblk].reshape(n_tok)
    smem = pl.BlockSpec(memory_space=pltpu.SMEM)
    inv = pl.pallas_call(
        functools.partial(_inverse_kernel, n_tok=n_tok),
        out_shape=jax.ShapeDtypeStruct((nt * tm,), I32),
        in_specs=[smem, smem, smem],
        out_specs=smem,
        name=f"moe_inverse_{layer}",
    )(pos, pad_lo[0], pad_hi[0])
    is_pad = inv < 0
    gsrc = jnp.maximum(inv, 0)
    sdst = jnp.where(is_pad, n_tok + jnp.cumsum(is_pad.astype(I32)) - 1, inv)
    sdst = jnp.concatenate([nt * tm + jnp.arange(tm, dtype=I32), sdst])
    tb = tile_bucket[0, :nt]
    n_used = jnp.sum((tb < N_BUCKETS).astype(I32))
    tsrc = jnp.minimum(jnp.arange(nt + 1, dtype=I32), n_used - 1)
    tbc = jnp.minimum(tb[tsrc], N_BUCKETS - 1)
    pair_lo = jnp.array([p[0] for p in PAIRS], I32)
    pair_hi = jnp.array([p[1] for p in PAIRS], I32)
    grp = tbc // len(PAIRS)
    telo = grp * EXPERTS_PER_GROUP + pair_lo[tbc % len(PAIRS)]
    tehi = grp * EXPERTS_PER_GROUP + pair_hi[tbc % len(PAIRS)]

    def wspec(shape, which):
        return pl.BlockSpec((1, 1) + shape, lambda r, lo, hi, *_: (layer, (lo, hi)[which][r], 0, 0))

    up_shape = (D_MODEL, EXPERT_DIM)
    down_shape = (EXPERT_DIM, D_MODEL)
    wr_cols, rb_cols = route_cols
    buf = pltpu.VMEM((2, tm * SLABS, LANES), F32)
    dma2 = pltpu.SemaphoreType.DMA((2,))
    return pl.pallas_call(
        functools.partial(_expert_kernel, tm=tm),
        out_shape=jax.ShapeDtypeStruct(((nt + 1) * tm * SLABS, LANES), F32),
        grid_spec=pltpu.PrefetchScalarGridSpec(
            num_scalar_prefetch=5, grid=(nt + 1,),
            in_specs=[pl.BlockSpec(memory_space=pl.ANY),
                      pl.BlockSpec((D_MODEL, LANES), lambda r, *_: (0, 0)),
                      pl.BlockSpec((1, LANES), lambda r, *_: (0, 0)),
                      wspec(up_shape, 0), wspec(up_shape, 0), wspec(down_shape, 0),
                      wspec(up_shape, 1), wspec(up_shape, 1), wspec(down_shape, 1)],
            out_specs=pl.BlockSpec(memory_space=pl.ANY),
            scratch_shapes=[buf, buf, pltpu.VMEM((4,) + up_shape, BF16), pltpu.VMEM((2,) + down_shape, BF16),
                            dma2, dma2, pltpu.SemaphoreType.DMA(())]),
        compiler_params=_cparams("arbitrary"),
        name=f"moe_experts_{layer}",
    )(telo, tehi, n_used.reshape(1), gsrc, sdst, xs_rt, wr_cols, rb_cols,
      w_gate, w_up, w_down, w_gate, w_up, w_down)


def kernel(x_prompt, x_sample, cache_k, cache_v, a_norm_g, a_w_in, a_v_norm_g, a_w_s, a_b_s, a_w_out, kv_norm_g, w_kv, k_norm_g, b_norm_g, b_w_q, q_norm_g, b_w_o, ffn_norm_g, router_g_w, router_g_b, router_e_w, router_e_b, moe_w_gate, moe_w_up, moe_w_down):
    bsz, seq, _ = x_prompt.shape
    dbsz, dseq, _ = x_sample.shape
    past = cache_k.shape[1]
    tp, ts = bsz * seq, dbsz * dseq
    n_tok = tp + ts
    tb = TOKEN_BLOCK
    gb = GMLP_BLOCK
    assert seq % ATTN_TILE == 0 and tp % KVQ_TILE == 0 and ts % KVQ_TILE == 0 and tp % tb == 0 and ts % tb == 0
    assert gb % dseq == 0 and dseq <= CHUNK and past % ATTN_TILE == 0
    nps, nss = tp // tb, ts // tb
    xp = x_prompt.reshape(tp, D_MODEL)
    xsm = x_sample.reshape(ts, D_MODEL)

    def pstep(i):
        return jnp.minimum(i, nps - 1)

    def sstep(i):
        return jnp.maximum(i - nps, 0)

    def rt_shape(n):
        return jax.ShapeDtypeStruct((n * SLABS, LANES), F32)

    def rt_spec(rows, index):
        return pl.BlockSpec((rows * SLABS, LANES), lambda *i: (index(*i), 0))

    tok_spec = pl.BlockSpec((tb, D_MODEL), lambda i: (i, 0))
    route_in_specs = [_const_spec((1, D_MODEL)), _const_spec((ROUTER_ROWS, D_MODEL)),
                      _const_spec((ROUTER_ROWS, LANES))]
    route_out_shapes = (jax.ShapeDtypeStruct((n_tok, D_MODEL), F32), rt_shape(n_tok),
                        jax.ShapeDtypeStruct((n_tok // LANES, 1, LANES), I32))
    route_out_specs = [tok_spec, rt_spec(tb, lambda i: i), pl.BlockSpec((tb // LANES, 1, LANES), lambda i: (i, 0, 0))]

    pos_i = jnp.arange(gb)
    mask = (pos_i[None, :] // CHUNK) <= (pos_i[:, None] // CHUNK)
    w_prompt = jnp.where(mask, a_w_s[0], 0.0)
    rep = gb // dseq
    w_sample = jnp.einsum("ij,hts->hitjs", jnp.eye(rep, dtype=F32), w_prompt[:, :dseq, :dseq]).reshape(
        A_HEADS, gb, gb)
    ws_all = jnp.stack([w_prompt, w_sample]).astype(BF16)
    b_prompt = a_b_s[0]
    b_sample = jnp.tile(a_b_s[0][:, :dseq], (1, rep))
    bs_all = jnp.stack([jnp.repeat(b.T, A_HEAD_DIM, axis=1) for b in (b_prompt, b_sample)])
    route0, cols0 = _router_operands(0, ffn_norm_g, router_g_w, router_g_b, router_e_w, router_e_b)
    h1, xs1, bucket1, v_rows = pl.pallas_call(
        functools.partial(_a_layer_kernel, nps=nps),
        out_shape=route_out_shapes + (jax.ShapeDtypeStruct((ts, A_WIDTH), F32),),
        grid=(nps + nss,),
        in_specs=[pl.BlockSpec((tb, D_MODEL), lambda i: (pstep(i), 0)),
                  pl.BlockSpec((tb, D_MODEL), lambda i: (sstep(i), 0)),
                  _const_spec((1, D_MODEL)), _const_spec((D_MODEL, 2 * A_WIDTH)), _const_spec((1, A_WIDTH)),
                  pl.BlockSpec((1, A_HEADS, gb, gb), lambda i: (i // nps, 0, 0, 0)),
                  pl.BlockSpec((1, gb, A_WIDTH), lambda i: (i // nps, 0, 0)),
                  _const_spec((A_WIDTH, D_MODEL))] + route_in_specs,
        out_specs=route_out_specs + [pl.BlockSpec((tb, A_WIDTH), lambda i: (sstep(i), 0))],
        compiler_params=_cparams("arbitrary"),
        name="a_layer",
    )(xp, xsm, a_norm_g[0][None, :], a_w_in[0].astype(BF16), a_v_norm_g[0][None, :], ws_all, bs_all,
      a_w_out[0].astype(BF16), *route0)
    y1 = _moe(xs1, bucket1, 0, cols0, moe_w_gate, moe_w_up, moe_w_down)

    tk = KVQ_TILE
    npt, nst = tp // tk, ts // tk
    spt = seq // tk

    def ptile(i):
        return jnp.minimum(i, npt - 1)

    def stile(i):
        return jnp.maximum(i - npt, 0)

    row_spec = pl.BlockSpec((tk, D_MODEL), lambda i: (i, 0))
    prow_spec = pl.BlockSpec((tk, D_MODEL), lambda i: (ptile(i), 0))
    srow_spec = pl.BlockSpec((tk, D_MODEL), lambda i: (stile(i), 0))
    head_spec = pl.BlockSpec((1, N_HEADS, tk, HEAD_DIM), lambda i: (ptile(i) // spt, 0, ptile(i) % spt, 0))
    f32_rows = lambda n: jax.ShapeDtypeStruct((n, D_MODEL), F32)
    head_major = jax.ShapeDtypeStruct((bsz, N_HEADS, seq, HEAD_DIM), BF16)
    h2, k_p, v_p, k_s, v_s, kb, vb, qb, q_s = pl.pallas_call(
        functools.partial(_kvq_kernel, npt=npt),
        out_shape=(f32_rows(n_tok), rt_shape(tp), rt_shape(tp), rt_shape(ts), rt_shape(ts),
                   head_major, head_major, head_major, f32_rows(ts)),
        grid=(npt + nst,),
        in_specs=[row_spec, rt_spec(tk, lambda i: i), _const_spec((1, D_MODEL)),
                  _const_spec((D_MODEL, 2 * D_MODEL)), _const_spec((1, HEAD_DIM)), _const_spec((1, D_MODEL)),
                  _const_spec((D_MODEL, D_MODEL)), _const_spec((1, HEAD_DIM))],
        out_specs=[row_spec, rt_spec(tk, ptile), rt_spec(tk, ptile), rt_spec(tk, stile), rt_spec(tk, stile),
                   head_spec, head_spec, head_spec, srow_spec],
        compiler_params=_cparams("arbitrary"),
        name="kvq",
    )(h1, y1, kv_norm_g[None, :], w_kv.astype(BF16), k_norm_g[None, :], b_norm_g[0][None, :],
      b_w_q[0].astype(BF16), q_norm_g[0][None, :])

    tq = ATTN_TILE
    nq = seq // tq
    qspec = pl.BlockSpec((1, N_HEADS, tq, HEAD_DIM), lambda b, i: (b, 0, i, 0))
    kvspec = pl.BlockSpec((1, N_HEADS, seq, HEAD_DIM), lambda b, i: (b, 0, 0, 0), pipeline_mode=pl.Buffered(1))
    o_p = pl.pallas_call(
        _attn_kernel,
        out_shape=jax.ShapeDtypeStruct((tp, D_MODEL), BF16),
        grid=(bsz, nq),
        in_specs=[qspec, kvspec, kvspec],
        out_specs=pl.BlockSpec((tq, D_MODEL), lambda b, i: (b * nq + i, 0)),
        scratch_shapes=[pltpu.VMEM((N_HEADS, tq, 1), F32), pltpu.VMEM((tq, D_MODEL), F32)],
        compiler_params=_cparams("arbitrary", "arbitrary"),
        name="attn_prompt",
    )(qb, kb, vb)

    ck = ATTN_TILE
    nkc = past // ck
    new_spec = pl.BlockSpec((dseq, D_MODEL), lambda b: (b, 0))
    new_rt_spec = rt_spec(dseq, lambda b: b)
    newest_spec = rt_spec(ck, lambda b: b * nkc + nkc - 1)
    any_spec = pl.BlockSpec(memory_space=pl.ANY)
    cache_k2 = cache_k.reshape(dbsz * past * N_HEADS, HEAD_DIM)
    cache_v2 = cache_v.reshape(dbsz * past * N_HEADS, HEAD_DIM)
    chunk_buf = pltpu.VMEM((ck * N_HEADS, HEAD_DIM), F32)
    o_s = pl.pallas_call(
        functools.partial(_sample_attn_kernel, nkc=nkc),
        out_shape=jax.ShapeDtypeStruct((ts, D_MODEL), BF16),
        grid=(dbsz,),
        in_specs=[new_spec, new_rt_spec, new_rt_spec, newest_spec, newest_spec, any_spec, any_spec],
        out_specs=new_spec,
        scratch_shapes=[pltpu.VMEM((N_HEADS, dseq, 1), F32), pltpu.VMEM((dseq, D_MODEL), F32), chunk_buf, chunk_buf,
                        pltpu.SemaphoreType.DMA((2,))],
        compiler_params=_cparams("arbitrary"),
        name="attn_sample",
    )(q_s, k_s, v_s, cache_k2, cache_v2, cache_k2, cache_v2)

    route1, cols1 = _router_operands(1, ffn_norm_g, router_g_w, router_g_b, router_e_w, router_e_b)
    h3, xs2, bucket2 = pl.pallas_call(
        functools.partial(_oproj_kernel, nps=nps),
        out_shape=route_out_shapes,
        grid=(nps + nss,),
        in_specs=[pl.BlockSpec((tb, D_MODEL), lambda i: (pstep(i), 0)),
                  pl.BlockSpec((tb, D_MODEL), lambda i: (sstep(i), 0)),
                  tok_spec, _const_spec((D_MODEL, D_MODEL))] + route_in_specs,
        out_specs=route_out_specs,
        compiler_params=_cparams("arbitrary"),
        name="o_proj",
    )(o_p, o_s, h2, b_w_o[0].astype(BF16), *route1)
    y2 = _moe(xs2, bucket2, 1, cols1, moe_w_gate, moe_w_up, moe_w_down)

    y_p, y_s = pl.pallas_call(
        functools.partial(_final_kernel, npt=npt),
        out_shape=(f32_rows(tp), f32_rows(ts)),
        grid=(npt + nst,),
        in_specs=[row_spec, rt_spec(tk, lambda i: i)],
        out_specs=[prow_spec, srow_spec],
        compiler_params=_cparams("arbitrary"),
        name="final_residual",
    )(h3, y2)

    kv_shape_p = (bsz, seq, N_HEADS, HEAD_DIM)
    kv_shape_s = (dbsz, dseq, N_HEADS, HEAD_DIM)
    return (y_p.reshape(bsz, seq, D_MODEL), y_s.reshape(dbsz, dseq, D_MODEL),
            k_p.reshape(kv_shape_p), v_p.reshape(kv_shape_p), k_s.reshape(kv_shape_s), v_s.reshape(kv_shape_s),
            v_rows.reshape(1, dbsz, dseq, A_WIDTH))
```

```python
import functools

import jax
import jax.numpy as jnp
from jax import lax
from jax.experimental import pallas as pl
from jax.experimental.pallas import tpu as pltpu

F32 = jnp.float32
BF16 = jnp.bfloat16
I32 = jnp.int32

D_MODEL = 1024
CHUNK = 64
GMLP_BLOCK = 128
A_WIDTH = 2 * D_MODEL
A_HEADS = 8
A_HEAD_DIM = A_WIDTH // A_HEADS
N_HEADS = 8
HEAD_DIM = D_MODEL // N_HEADS
N_GROUPS = 4
EXPERTS_PER_GROUP = 4
EXPERT_DIM = D_MODEL // 2
EPS = 1e-6

LANES = 128
SUBLANES = 8
MXU_WIDTH = 256
SLABS = D_MODEL // LANES
assert SLABS == SUBLANES and N_HEADS == SLABS and HEAD_DIM == LANES
PAIRS = ((0, 1), (0, 2), (0, 3), (1, 2), (1, 3), (2, 3))
N_BUCKETS = N_GROUPS * len(PAIRS)
ROUTER_ROWS = 32
EXPERT_ROW0 = 8

TOKEN_BLOCK = 2 * GMLP_BLOCK
EXPERT_TILE = 256
KVQ_TILE = 256
ATTN_TILE = 256
VMEM_LIMIT = 56 * 1024 * 1024
STOP_MASS = 105.0


def _cparams(*sem):
    return pltpu.CompilerParams(dimension_semantics=sem, vmem_limit_bytes=VMEM_LIMIT)


def _rms(x, g):
    ms = jnp.mean(x * x, axis=-1, keepdims=True)
    return x * lax.rsqrt(ms + EPS) * g


def _load_rows(ref, n):
    return jnp.concatenate([ref[pl.ds(c, n, stride=SLABS), :] for c in range(SLABS)], axis=1)


def _store_rows(ref, x):
    n = x.shape[0]
    for c in range(SLABS):
        ref[pl.ds(c, n, stride=SLABS), :] = x[:, c * LANES:(c + 1) * LANES]


def _row_copy(src_hbm, src_row, buf, slot, k, sem):
    src = src_row * SLABS if isinstance(src_row, int) else pl.multiple_of(src_row * SLABS, SLABS)
    return pltpu.make_async_copy(src_hbm.at[pl.ds(src, SLABS)], buf.at[slot, pl.ds(k * SLABS, SLABS)], sem.at[slot])


def _gathered_rows(idx_ref, src_hbm, buf, sem, rows):
    i = pl.program_id(0)
    slot = i % 2

    @pl.when(i == 0)
    def _():
        for k in range(rows):
            _row_copy(src_hbm, idx_ref[k], buf, 0, k, sem).start(priority=k % 2)

    @pl.when(i + 1 < pl.num_programs(0))
    def _():
        for k in range(rows):
            _row_copy(src_hbm, idx_ref[(i + 1) * rows + k], buf, 1 - slot, k, sem).start(priority=k % 2)

    for k in range(rows):
        _row_copy(src_hbm, 0, buf, slot, k, sem).wait()
    return _load_rows(buf.at[slot], rows)


def _first_argmax(vals):
    m = vals[0]
    for v in vals[1:]:
        m = jnp.maximum(m, v)
    idx = jnp.full(m.shape, len(vals) - 1, I32)
    for r in range(len(vals) - 2, -1, -1):
        idx = jnp.where(vals[r] == m, r, idx)
    return m, idx


def _route(xn, wrt_ref, rb_ref, xs_ref, bucket_ref):
    lt = lax.dot_general(wrt_ref[...], xn, (((1,), (1,)), ((), ())),
                         precision=lax.Precision.HIGHEST, preferred_element_type=F32) + rb_ref[:, 0:1]
    _, g_idx = _first_argmax([lt[r:r + 1, :] for r in range(N_GROUPS)])
    le = []
    for e in range(EXPERTS_PER_GROUP):
        row = EXPERT_ROW0 + EXPERTS_PER_GROUP * (N_GROUPS - 1) + e
        v = lt[row:row + 1, :]
        for g in range(N_GROUPS - 2, -1, -1):
            row = EXPERT_ROW0 + EXPERTS_PER_GROUP * g + e
            v = jnp.where(g_idx == g, lt[row:row + 1, :], v)
        le.append(v)
    _, i1 = _first_argmax(le)
    _, i2 = _first_argmax([jnp.where(i1 == e, -jnp.inf, le[e]) for e in range(EXPERTS_PER_GROUP)])
    lo = jnp.minimum(i1, i2)
    hi = jnp.maximum(i1, i2)
    pair = jnp.where(lo == 0, hi - 1, jnp.where(lo == 1, hi + 1, 5))
    bucket = g_idx * len(PAIRS) + pair
    for sb in range(xn.shape[0] // LANES):
        bucket_ref[sb] = bucket[:, sb * LANES:(sb + 1) * LANES]
    _store_rows(xs_ref, xn)


def _a_layer_kernel(xp_ref, xsm_ref, ag_ref, win_ref, vg_ref, ws_ref, bs_ref, wout_ref, fg_ref, wrt_ref, rb_ref,
                    h_ref, xs_ref, bucket_ref, vrow_ref, *, nps):
    i = pl.program_id(0)
    x = jnp.where(i < nps, xp_ref[...], xsm_ref[...])
    xn = _rms(x, ag_ref[...]).astype(BF16)
    n_col = 4
    cw = 2 * A_WIDTH // n_col
    z = [jax.nn.gelu(jnp.dot(xn, win_ref[:, c * cw:(c + 1) * cw], preferred_element_type=F32)) for c in range(n_col)]
    u = jnp.concatenate(z[:n_col // 2], axis=1)
    v = _rms(jnp.concatenate(z[n_col // 2:], axis=1), vg_ref[...])

    @pl.when(i >= nps)
    def _():
        vrow_ref[...] = v

    vb = v.astype(BF16)
    gate = jnp.concatenate([
        jnp.concatenate(
            [jnp.dot(ws_ref[0, h], vb[sb * GMLP_BLOCK:(sb + 1) * GMLP_BLOCK, h * A_HEAD_DIM:(h + 1) * A_HEAD_DIM],
                     preferred_element_type=F32) for h in range(A_HEADS)], axis=1) + bs_ref[0]
        for sb in range(x.shape[0] // GMLP_BLOCK)], axis=0)
    s = (u * gate).astype(BF16)
    h1 = x + jnp.dot(s, wout_ref[...], preferred_element_type=F32)
    h_ref[...] = h1
    _route(_rms(h1, fg_ref[...]), wrt_ref, rb_ref, xs_ref, bucket_ref)


def _oproj_kernel(op_ref, osm_ref, h_in_ref, wo_ref, fg_ref, wrt_ref, rb_ref, h_ref, xs_ref, bucket_ref, *, nps):
    i = pl.program_id(0)
    o = jnp.where(i < nps, op_ref[...], osm_ref[...])
    h3 = h_in_ref[...] + jnp.dot(o, wo_ref[...], preferred_element_type=F32)
    h_ref[...] = h3
    _route(_rms(h3, fg_ref[...]), wrt_ref, rb_ref, xs_ref, bucket_ref)


def _rank_kernel(b_ref, pos_ref, tb_ref, lo_ref, hi_ref, *, tm, nt):
    bk = b_ref[...]
    nr = bk.shape[0]
    r_i = lax.broadcasted_iota(I32, (LANES, LANES), 0)
    c_i = lax.broadcasted_iota(I32, (LANES, LANES), 1)
    upper = (r_i <= c_i).astype(BF16)
    rr = lax.broadcasted_iota(I32, (nr, nr), 0)
    cc = lax.broadcasted_iota(I32, (nr, nr), 1)
    before_rows = (cc < rr).astype(BF16)
    lane = lax.broadcasted_iota(I32, (1, LANES), 1)
    tile_start = lane.astype(F32) * tm
    pos = jnp.zeros((nr, LANES), F32)
    seg_start = jnp.zeros((1, LANES), F32)
    tile_bucket = jnp.zeros((1, LANES), I32)
    pad_lo = jnp.zeros((1, LANES), F32)
    pad_hi = jnp.zeros((1, LANES), F32)
    for b in range(N_BUCKETS):
        m = bk == b
        pref = jnp.dot(jnp.where(m, 1.0, 0.0).astype(BF16), upper, preferred_element_type=F32)
        rowtot = jnp.broadcast_to(pref[:, LANES - 1:LANES], (nr, LANES))
        before = jnp.dot(before_rows, rowtot.astype(BF16), preferred_element_type=F32)
        cnt = jnp.sum(rowtot, axis=0, keepdims=True)
        pos = pos + jnp.where(m, seg_start + before + pref - 1.0, 0.0)
        pad_lo = jnp.where(lane == b, seg_start + cnt, pad_lo)
        seg_start = seg_start + jnp.ceil(cnt / tm) * tm
        pad_hi = jnp.where(lane == b, seg_start, pad_hi)
        tile_bucket = tile_bucket + (seg_start <= tile_start).astype(I32)
    pad_lo = jnp.where(lane == N_BUCKETS, seg_start, pad_lo)
    pad_hi = jnp.where(lane == N_BUCKETS, float(nt * tm), pad_hi)
    pos_ref[...] = pos.astype(I32)
    tb_ref[...] = tile_bucket
    lo_ref[...] = pad_lo.astype(I32)
    hi_ref[...] = pad_hi.astype(I32)


def _inverse_kernel(pos_ref, lo_ref, hi_ref, inv_ref, *, n_tok):
    def clear(r, carry):
        inv_ref[r] = 0
        return carry

    def put(t, carry):
        inv_ref[pos_ref[t]] = t
        return carry

    for b in range(N_BUCKETS + 1):
        lax.fori_loop(lo_ref[b], hi_ref[b], clear, 0)
    lax.fori_loop(0, n_tok, put, 0, unroll=8)


def _expert_kernel(telo_ref, tehi_ref, nused_ref, inv_ref, xs_hbm, wr_ref, rb_ref, wgl_ref, wul_ref, wdl_ref,
                   wgh_ref, wuh_ref, wdh_ref, y_ref, xbuf, wup, wdown, gsem, *, tm):
    r = pl.program_id(0)
    nt = pl.num_programs(0)
    n_used = nused_ref[0]
    slot = r % 2
    other = 1 - slot

    @pl.when(r == 0)
    def _():
        for k in range(tm):
            _row_copy(xs_hbm, inv_ref[k], xbuf, 0, k, gsem).start(priority=k % 2)

    @pl.when(r <= n_used)
    def _():
        for k in range(tm):
            _row_copy(xs_hbm, 0, xbuf, slot, k, gsem).wait()

    @pl.when(r < n_used)
    def _():
        nxt = jnp.minimum(r + 1, nt - 1)
        prev = jnp.maximum(r - 1, 0)

        @pl.when(jnp.logical_or(r == 0, telo_ref[r] != telo_ref[prev]))
        def _():
            wup[0] = wgl_ref[0, 0].astype(BF16)
            wup[1] = wul_ref[0, 0].astype(BF16)
            wdown[0] = wdl_ref[0, 0].astype(BF16)

        @pl.when(jnp.logical_or(r == 0, tehi_ref[r] != tehi_ref[prev]))
        def _():
            wup[2] = wgh_ref[0, 0].astype(BF16)
            wup[3] = wuh_ref[0, 0].astype(BF16)
            wdown[1] = wdh_ref[0, 0].astype(BF16)

        n_chunks = EXPERT_DIM // MXU_WIDTH
        points = 8 * n_chunks

        def issue(point):
            for k in range(point * tm // points, (point + 1) * tm // points):
                _row_copy(xs_hbm, inv_ref[nxt * tm + k], xbuf, other, k, gsem).start(priority=k % 2)

        x = _load_rows(xbuf.at[slot], tm).astype(BF16)
        logits = jnp.dot(x, wr_ref[...], preferred_element_type=F32) + rb_ref[...]
        lane = lax.broadcasted_iota(I32, logits.shape, 1)
        grp = telo_ref[r] // EXPERTS_PER_GROUP
        is_group = lane < N_GROUPS
        m = jnp.max(jnp.where(is_group, logits, -jnp.inf), axis=1, keepdims=True)
        ex = jnp.exp(logits - m)
        p_g = (jnp.sum(jnp.where(lane == grp, ex, 0.0), axis=1, keepdims=True)
               / jnp.sum(jnp.where(is_group, ex, 0.0), axis=1, keepdims=True))
        l_lo = jnp.sum(jnp.where(lane == EXPERT_ROW0 + telo_ref[r], logits, 0.0), axis=1, keepdims=True)
        l_hi = jnp.sum(jnp.where(lane == EXPERT_ROW0 + tehi_ref[r], logits, 0.0), axis=1, keepdims=True)
        mm = jnp.maximum(l_lo, l_hi)
        e_lo = jnp.exp(l_lo - mm)
        e_hi = jnp.exp(l_hi - mm)
        gates = (p_g * (e_lo / (e_lo + e_hi)), p_g * (e_hi / (e_lo + e_hi)))

        half = D_MODEL // 2
        xk = (x[:, :half], x[:, half:])
        point = 0
        y = None
        for which in range(2):
            hid = []
            for c in range(n_chunks):
                cs = slice(c * MXU_WIDTH, (c + 1) * MXU_WIDTH)
                acts = []
                for mat in (2 * which, 2 * which + 1):
                    acc = None
                    for kh in range(2):
                        issue(point)
                        point += 1
                        part = jnp.dot(xk[kh], wup[mat, kh * half:(kh + 1) * half, cs], preferred_element_type=F32)
                        acc = part if acc is None else acc + part
                    acts.append(acc)
                hid.append((jax.nn.silu(acts[0]) * acts[1] * gates[which]).astype(BF16))
            part = jnp.dot(jnp.concatenate(hid, axis=1), wdown[which], preferred_element_type=F32)
            y = part if y is None else y + part
        assert point == points
        _store_rows(y_ref, y)

        @pl.when(r == nt - 1)
        def _():
            for k in range(tm):
                _row_copy(xs_hbm, 0, xbuf, other, k, gsem).wait()

    @pl.when(r >= n_used)
    def _():
        y_ref[...] = jnp.zeros_like(y_ref)


def _head_rms(x, g):
    return jnp.concatenate([_rms(x[:, h * HEAD_DIM:(h + 1) * HEAD_DIM], g) for h in range(N_HEADS)], axis=1)


def _kvq_kernel(pos_ref, h_in_ref, y_hbm, kvg_ref, wkv_ref, kng_ref, bg_ref, wq_ref, qng_ref,
                h_ref, kp_ref, vp_ref, ksm_ref, vsm_ref, kb_ref, vb_ref, qb_ref, qsm_ref, ybuf, sem, *, npt):
    i = pl.program_id(0)
    tk = h_in_ref.shape[0]
    h2 = h_in_ref[...] + _gathered_rows(pos_ref, y_hbm, ybuf, sem, tk)
    h_ref[...] = h2
    kv = jnp.dot(_rms(h2, kvg_ref[...]).astype(BF16), wkv_ref[...], preferred_element_type=F32)
    q = jnp.dot(_rms(h2, bg_ref[...]).astype(BF16), wq_ref[...], preferred_element_type=F32)
    k = _head_rms(kv[:, :D_MODEL], kng_ref[...])
    v = kv[:, D_MODEL:]
    q = _head_rms(q, qng_ref[...]) * (HEAD_DIM ** -0.5)

    @pl.when(i < npt)
    def _():
        _store_rows(kp_ref, k)
        _store_rows(vp_ref, v)
        for h in range(N_HEADS):
            sl = slice(h * HEAD_DIM, (h + 1) * HEAD_DIM)
            kb_ref[0, h] = k[:, sl].astype(BF16)
            vb_ref[0, h] = v[:, sl].astype(BF16)
            qb_ref[0, h] = q[:, sl].astype(BF16)

    @pl.when(i >= npt)
    def _():
        _store_rows(ksm_ref, k)
        _store_rows(vsm_ref, v)
        qsm_ref[...] = q


def _stick_blocks(qs, ks, vs, rs, visible, later_keys):
    n = len(qs)
    zs = [lax.dot_general(qs[h], ks[h], (((1,), (1,)), ((), ())), preferred_element_type=F32) for h in range(n)]
    sps = [jnp.maximum(z, 0.0) + jnp.log(1.0 + jnp.exp(-jnp.abs(z))) for z in zs]
    if visible is not None:
        sps = [jnp.where(visible, sp, 0.0) for sp in sps]
    his = [sp.astype(BF16) for sp in sps]
    los = [(sp - hi.astype(F32)).astype(BF16) for sp, hi in zip(sps, his)]
    cs = [jnp.dot(hi, later_keys, preferred_element_type=F32) + jnp.dot(lo, later_keys, preferred_element_type=F32)
          for hi, lo in zip(his, los)]
    ws = [jnp.exp(zs[h] - cs[h] - rs[h]) for h in range(n)]
    if visible is not None:
        ws = [jnp.where(visible, w, 0.0) for w in ws]
    outs = [jnp.dot(ws[h].astype(BF16), vs[h], preferred_element_type=F32) for h in range(n)]
    return outs, [rs[h] + cs[h][:, 0:1] for h in range(n)]


def _later_keys(n):
    rowi = lax.broadcasted_iota(I32, (n, n), 0)
    coli = lax.broadcasted_iota(I32, (n, n), 1)
    return (rowi >= coli).astype(BF16)


def _min_over(rs):
    m = rs[0]
    for r in rs[1:]:
        m = jnp.minimum(m, r)
    return jnp.min(m)


def _attn_kernel(q_ref, k_ref, v_ref, o_ref, r_sc, acc_sc):
    i = pl.program_id(1)
    tq = q_ref.shape[2]
    visible = lax.broadcasted_iota(I32, (tq, tq), 1) < lax.broadcasted_iota(I32, (tq, tq), 0)
    later_keys = _later_keys(tq)
    heads = range(N_HEADS)

    def block(j, first):
        start = pl.multiple_of(j * tq, tq)
        qs = [q_ref[0, h] for h in heads]
        ks = [k_ref[0, h, pl.ds(start, tq), :] for h in heads]
        vs = [v_ref[0, h, pl.ds(start, tq), :] for h in heads]
        rs = [jnp.zeros((tq, 1), F32) if first else r_sc[h] for h in heads]
        outs, rs = _stick_blocks(qs, ks, vs, rs, visible if first else None, later_keys)
        acc = jnp.concatenate(outs, axis=1)
        if first:
            acc_sc[...] = acc
        else:
            acc_sc[...] += acc
        for h in heads:
            r_sc[h] = rs[h]
        return _min_over(rs)

    rmin = block(i, True)
    lax.while_loop(lambda st: jnp.logical_and(st[0] >= 0, st[1] <= STOP_MASS),
                   lambda st: (st[0] - 1, block(st[0], False)), (i - 1, rmin))
    o_ref[...] = acc_sc[...].astype(o_ref.dtype)


def _sample_attn_kernel(q_ref, kn_ref, vn_ref, ck_ref, cv_ref, ck_hbm, cv_hbm, o_ref, r_sc, acc_sc, kbuf, vbuf, sem,
                        *, nkc):
    b = pl.program_id(0)
    ds = q_ref.shape[0]
    ck = ck_ref.shape[0] // N_HEADS
    heads = range(N_HEADS)

    def head_rows(ref, h, n):
        return ref[pl.ds(h, n, stride=N_HEADS), :].astype(BF16)

    qs = [q_ref[:, h * HEAD_DIM:(h + 1) * HEAD_DIM].astype(BF16) for h in heads]
    visible = lax.broadcasted_iota(I32, (ds, LANES), 1) < lax.broadcasted_iota(I32, (ds, LANES), 0)
    pad = jnp.zeros((LANES - ds, HEAD_DIM), BF16)
    outs, rs = _stick_blocks(qs, [jnp.concatenate([head_rows(kn_ref, h, ds), pad], axis=0) for h in heads],
                             [jnp.concatenate([head_rows(vn_ref, h, ds), pad], axis=0) for h in heads],
                             [jnp.zeros((ds, 1), F32)] * N_HEADS, visible, _later_keys(LANES))
    acc_sc[...] = jnp.concatenate(outs, axis=1)
    for h in heads:
        r_sc[h] = rs[h]

    def chunk(kref, vref):
        outs, rs = _stick_blocks(qs, [head_rows(kref, h, ck) for h in heads], [head_rows(vref, h, ck) for h in heads],
                                 [r_sc[h] for h in heads], None, _later_keys(ck))
        acc_sc[...] += jnp.concatenate(outs, axis=1)
        for h in heads:
            r_sc[h] = rs[h]
        return _min_over(rs)

    def older(st):
        c = st[0]
        rows = ck * N_HEADS
        start = pl.multiple_of((b * nkc + c) * rows, rows)
        cpk = pltpu.make_async_copy(ck_hbm.at[pl.ds(start, rows)], kbuf, sem.at[0])
        cpv = pltpu.make_async_copy(cv_hbm.at[pl.ds(start, rows)], vbuf, sem.at[1])
        cpk.start()
        cpv.start()
        cpk.wait()
        cpv.wait()
        return c - 1, chunk(kbuf, vbuf)

    lax.while_loop(lambda st: jnp.logical_and(st[0] >= 0, st[1] <= STOP_MASS), older,
                   (nkc - 2, chunk(ck_ref, cv_ref)))
    o_ref[...] = acc_sc[...].astype(o_ref.dtype)


def _final_kernel(pos_ref, h_ref, y_hbm, yp_ref, ysm_ref, ybuf, sem, *, npt):
    i = pl.program_id(0)
    y = h_ref[...] + _gathered_rows(pos_ref, y_hbm, ybuf, sem, h_ref.shape[0])

    @pl.when(i < npt)
    def _():
        yp_ref[...] = y

    @pl.when(i >= npt)
    def _():
        ysm_ref[...] = y


def _const_spec(shape):
    return pl.BlockSpec(shape, lambda *_: (0,) * len(shape))


def _router_operands(layer, ffn_norm_g, router_g_w, router_g_b, router_e_w, router_e_b):
    n_e = N_GROUPS * EXPERTS_PER_GROUP
    wrt = jnp.zeros((ROUTER_ROWS, D_MODEL), F32)
    wrt = wrt.at[:N_GROUPS].set(router_g_w[layer].T)
    wrt = wrt.at[EXPERT_ROW0:EXPERT_ROW0 + n_e].set(router_e_w[layer].transpose(0, 2, 1).reshape(n_e, D_MODEL))
    rb = jnp.zeros((ROUTER_ROWS,), F32)
    rb = rb.at[:N_GROUPS].set(router_g_b[layer])
    rb = rb.at[EXPERT_ROW0:EXPERT_ROW0 + n_e].set(router_e_b[layer].reshape(-1))
    wr_cols = jnp.zeros((D_MODEL, LANES), F32).at[:, :ROUTER_ROWS].set(wrt.T).astype(BF16)
    rb_cols = jnp.zeros((1, LANES), F32).at[0, :ROUTER_ROWS].set(rb)
    return (ffn_norm_g[layer][None, :], wrt, jnp.broadcast_to(rb[:, None], (ROUTER_ROWS, LANES))), (wr_cols, rb_cols)


def _moe(xs_rt, bucket, layer, route_cols, w_gate, w_up, w_down):
    nblk = bucket.shape[0]
    n_tok = nblk * LANES
    tm = EXPERT_TILE
    nr = -(-nblk // LANES) * LANES
    nt = -(-(n_tok + N_BUCKETS * (tm - 1)) // tm)
    assert nt <= LANES
    bk = jnp.pad(bucket.reshape(nblk, LANES), ((0, nr - nblk), (0, 0)), constant_values=N_BUCKETS)
    lane_row = jax.ShapeDtypeStruct((1, LANES), I32)
    pos, tile_bucket, pad_lo, pad_hi = pl.pallas_call(
        functools.partial(_rank_kernel, tm=tm, nt=nt),
        out_shape=(jax.ShapeDtypeStruct((nr, LANES), I32), lane_row, lane_row, lane_row),
        name=f"moe_rank_{layer}",
    )(bk)
    pos = pos[:nblk].reshape(n_tok)
    smem = pl.BlockSpec(memory_space=pltpu.SMEM)
    inv = pl.pallas_call(
        functools.partial(_inverse_kernel, n_tok=n_tok),
        out_shape=jax.ShapeDtypeStruct((nt * tm,), I32),
        in_specs=[smem, smem, smem],
        out_specs=smem,
        name=f"moe_inverse_{layer}",
    )(pos, pad_lo[0], pad_hi[0])
    tb = tile_bucket[0, :nt]
    n_used = jnp.sum((tb < N_BUCKETS).astype(I32))
    tsrc = jnp.minimum(jnp.arange(nt, dtype=I32), n_used - 1)
    tbc = jnp.minimum(tb[tsrc], N_BUCKETS - 1)
    pair_lo = jnp.array([p[0] for p in PAIRS], I32)
    pair_hi = jnp.array([p[1] for p in PAIRS], I32)
    grp = tbc // len(PAIRS)
    telo = grp * EXPERTS_PER_GROUP + pair_lo[tbc % len(PAIRS)]
    tehi = grp * EXPERTS_PER_GROUP + pair_hi[tbc % len(PAIRS)]

    def wspec(shape, which):
        return pl.BlockSpec((1, 1) + shape, lambda r, lo, hi, *_: (layer, (lo, hi)[which][r], 0, 0))

    up_shape = (D_MODEL, EXPERT_DIM)
    down_shape = (EXPERT_DIM, D_MODEL)
    wr_cols, rb_cols = route_cols
    y_sorted = pl.pallas_call(
        functools.partial(_expert_kernel, tm=tm),
        out_shape=jax.ShapeDtypeStruct((nt * tm * SLABS, LANES), F32),
        grid_spec=pltpu.PrefetchScalarGridSpec(
            num_scalar_prefetch=4, grid=(nt,),
            in_specs=[pl.BlockSpec(memory_space=pl.ANY),
                      pl.BlockSpec((D_MODEL, LANES), lambda r, *_: (0, 0)),
                      pl.BlockSpec((1, LANES), lambda r, *_: (0, 0)),
                      wspec(up_shape, 0), wspec(up_shape, 0), wspec(down_shape, 0),
                      wspec(up_shape, 1), wspec(up_shape, 1), wspec(down_shape, 1)],
            out_specs=pl.BlockSpec((tm * SLABS, LANES), lambda r, *_: (r, 0)),
            scratch_shapes=[pltpu.VMEM((2, tm * SLABS, LANES), F32), pltpu.VMEM((4,) + up_shape, BF16),
                            pltpu.VMEM((2,) + down_shape, BF16), pltpu.SemaphoreType.DMA((2,))]),
        compiler_params=_cparams("arbitrary"),
        name=f"moe_experts_{layer}",
    )(telo, tehi, n_used.reshape(1), inv, xs_rt, wr_cols, rb_cols, w_gate, w_up, w_down, w_gate, w_up, w_down)
    return y_sorted, pos


def kernel(x_prompt, x_sample, cache_k, cache_v, a_norm_g, a_w_in, a_v_norm_g, a_w_s, a_b_s, a_w_out, kv_norm_g, w_kv, k_norm_g, b_norm_g, b_w_q, q_norm_g, b_w_o, ffn_norm_g, router_g_w, router_g_b, router_e_w, router_e_b, moe_w_gate, moe_w_up, moe_w_down):
    bsz, seq, _ = x_prompt.shape
    dbsz, dseq, _ = x_sample.shape
    past = cache_k.shape[1]
    tp, ts = bsz * seq, dbsz * dseq
    n_tok = tp + ts
    tb = TOKEN_BLOCK
    gb = GMLP_BLOCK
    assert seq % ATTN_TILE == 0 and tp % KVQ_TILE == 0 and ts % KVQ_TILE == 0 and tp % tb == 0 and ts % tb == 0
    assert gb % dseq == 0 and dseq <= CHUNK and past % ATTN_TILE == 0
    nps, nss = tp // tb, ts // tb
    xp = x_prompt.reshape(tp, D_MODEL)
    xsm = x_sample.reshape(ts, D_MODEL)

    def pstep(i):
        return jnp.minimum(i, nps - 1)

    def sstep(i):
        return jnp.maximum(i - nps, 0)

    def rt_shape(n):
        return jax.ShapeDtypeStruct((n * SLABS, LANES), F32)

    def rt_spec(rows, index):
        return pl.BlockSpec((rows * SLABS, LANES), lambda i, *_: (index(i), 0))

    tok_spec = pl.BlockSpec((tb, D_MODEL), lambda i: (i, 0))
    route_in_specs = [_const_spec((1, D_MODEL)), _const_spec((ROUTER_ROWS, D_MODEL)),
                      _const_spec((ROUTER_ROWS, LANES))]
    route_out_shapes = (jax.ShapeDtypeStruct((n_tok, D_MODEL), F32), rt_shape(n_tok),
                        jax.ShapeDtypeStruct((n_tok // LANES, 1, LANES), I32))
    route_out_specs = [tok_spec, rt_spec(tb, lambda i: i), pl.BlockSpec((tb // LANES, 1, LANES), lambda i: (i, 0, 0))]

    pos_i = jnp.arange(gb)
    mask = (pos_i[None, :] // CHUNK) <= (pos_i[:, None] // CHUNK)
    w_prompt = jnp.where(mask, a_w_s[0], 0.0)
    rep = gb // dseq
    w_sample = jnp.einsum("ij,hts->hitjs", jnp.eye(rep, dtype=F32), w_prompt[:, :dseq, :dseq]).reshape(
        A_HEADS, gb, gb)
    ws_all = jnp.stack([w_prompt, w_sample]).astype(BF16)
    b_prompt = a_b_s[0]
    b_sample = jnp.tile(a_b_s[0][:, :dseq], (1, rep))
    bs_all = jnp.stack([jnp.repeat(b.T, A_HEAD_DIM, axis=1) for b in (b_prompt, b_sample)])
    route0, cols0 = _router_operands(0, ffn_norm_g, router_g_w, router_g_b, router_e_w, router_e_b)
    h1, xs1, bucket1, v_rows = pl.pallas_call(
        functools.partial(_a_layer_kernel, nps=nps),
        out_shape=route_out_shapes + (jax.ShapeDtypeStruct((ts, A_WIDTH), F32),),
        grid=(nps + nss,),
        in_specs=[pl.BlockSpec((tb, D_MODEL), lambda i: (pstep(i), 0)),
                  pl.BlockSpec((tb, D_MODEL), lambda i: (sstep(i), 0)),
                  _const_spec((1, D_MODEL)), _const_spec((D_MODEL, 2 * A_WIDTH)), _const_spec((1, A_WIDTH)),
                  pl.BlockSpec((1, A_HEADS, gb, gb), lambda i: (i // nps, 0, 0, 0)),
                  pl.BlockSpec((1, gb, A_WIDTH), lambda i: (i // nps, 0, 0)),
                  _const_spec((A_WIDTH, D_MODEL))] + route_in_specs,
        out_specs=route_out_specs + [pl.BlockSpec((tb, A_WIDTH), lambda i: (sstep(i), 0))],
        compiler_params=_cparams("arbitrary"),
        name="a_layer",
    )(xp, xsm, a_norm_g[0][None, :], a_w_in[0].astype(BF16), a_v_norm_g[0][None, :], ws_all, bs_all,
      a_w_out[0].astype(BF16), *route0)
    y1, pos1 = _moe(xs1, bucket1, 0, cols0, moe_w_gate, moe_w_up, moe_w_down)

    tk = KVQ_TILE
    npt, nst = tp // tk, ts // tk
    spt = seq // tk

    def ptile(i):
        return jnp.minimum(i, npt - 1)

    def stile(i):
        return jnp.maximum(i - npt, 0)

    row_spec = pl.BlockSpec((tk, D_MODEL), lambda i, *_: (i, 0))
    prow_spec = pl.BlockSpec((tk, D_MODEL), lambda i, *_: (ptile(i), 0))
    srow_spec = pl.BlockSpec((tk, D_MODEL), lambda i, *_: (stile(i), 0))
    head_spec = pl.BlockSpec((1, N_HEADS, tk, HEAD_DIM), lambda i, *_: (ptile(i) // spt, 0, ptile(i) % spt, 0))
    any_spec = pl.BlockSpec(memory_space=pl.ANY)
    gather_scratch = [pltpu.VMEM((2, tk * SLABS, LANES), F32), pltpu.SemaphoreType.DMA((2,))]
    f32_rows = lambda n: jax.ShapeDtypeStruct((n, D_MODEL), F32)
    head_major = jax.ShapeDtypeStruct((bsz, N_HEADS, seq, HEAD_DIM), BF16)
    h2, k_p, v_p, k_s, v_s, kb, vb, qb, q_s = pl.pallas_call(
        functools.partial(_kvq_kernel, npt=npt),
        out_shape=(f32_rows(n_tok), rt_shape(tp), rt_shape(tp), rt_shape(ts), rt_shape(ts),
                   head_major, head_major, head_major, f32_rows(ts)),
        grid_spec=pltpu.PrefetchScalarGridSpec(
            num_scalar_prefetch=1, grid=(npt + nst,),
            in_specs=[row_spec, any_spec, _const_spec((1, D_MODEL)),
                      _const_spec((D_MODEL, 2 * D_MODEL)), _const_spec((1, HEAD_DIM)), _const_spec((1, D_MODEL)),
                      _const_spec((D_MODEL, D_MODEL)), _const_spec((1, HEAD_DIM))],
            out_specs=[row_spec, rt_spec(tk, ptile), rt_spec(tk, ptile), rt_spec(tk, stile), rt_spec(tk, stile),
                       head_spec, head_spec, head_spec, srow_spec],
            scratch_shapes=gather_scratch),
        compiler_params=_cparams("arbitrary"),
        name="kvq",
    )(pos1, h1, y1, kv_norm_g[None, :], w_kv.astype(BF16), k_norm_g[None, :], b_norm_g[0][None, :],
      b_w_q[0].astype(BF16), q_norm_g[0][None, :])

    tq = ATTN_TILE
    nq = seq // tq
    qspec = pl.BlockSpec((1, N_HEADS, tq, HEAD_DIM), lambda b, i: (b, 0, i, 0))
    kvspec = pl.BlockSpec((1, N_HEADS, seq, HEAD_DIM), lambda b, i: (b, 0, 0, 0), pipeline_mode=pl.Buffered(1))
    o_p = pl.pallas_call(
        _attn_kernel,
        out_shape=jax.ShapeDtypeStruct((tp, D_MODEL), BF16),
        grid=(bsz, nq),
        in_specs=[qspec, kvspec, kvspec],
        out_specs=pl.BlockSpec((tq, D_MODEL), lambda b, i: (b * nq + i, 0)),
        scratch_shapes=[pltpu.VMEM((N_HEADS, tq, 1), F32), pltpu.VMEM((tq, D_MODEL), F32)],
        compiler_params=_cparams("arbitrary", "arbitrary"),
        name="attn_prompt",
    )(qb, kb, vb)

    ck = ATTN_TILE
    nkc = past // ck
    new_spec = pl.BlockSpec((dseq, D_MODEL), lambda b: (b, 0))
    new_rt_spec = rt_spec(dseq, lambda b: b)
    newest_spec = rt_spec(ck, lambda b: b * nkc + nkc - 1)
    cache_k2 = cache_k.reshape(dbsz * past * N_HEADS, HEAD_DIM)
    cache_v2 = cache_v.reshape(dbsz * past * N_HEADS, HEAD_DIM)
    chunk_buf = pltpu.VMEM((ck * N_HEADS, HEAD_DIM), F32)
    o_s = pl.pallas_call(
        functools.partial(_sample_attn_kernel, nkc=nkc),
        out_shape=jax.ShapeDtypeStruct((ts, D_MODEL), BF16),
        grid=(dbsz,),
        in_specs=[new_spec, new_rt_spec, new_rt_spec, newest_spec, newest_spec, any_spec, any_spec],
        out_specs=new_spec,
        scratch_shapes=[pltpu.VMEM((N_HEADS, dseq, 1), F32), pltpu.VMEM((dseq, D_MODEL), F32), chunk_buf, chunk_buf,
                        pltpu.SemaphoreType.DMA((2,))],
        compiler_params=_cparams("arbitrary"),
        name="attn_sample",
    )(q_s, k_s, v_s, cache_k2, cache_v2, cache_k2, cache_v2)

    route1, cols1 = _router_operands(1, ffn_norm_g, router_g_w, router_g_b, router_e_w, router_e_b)
    h3, xs2, bucket2 = pl.pallas_call(
        functools.partial(_oproj_kernel, nps=nps),
        out_shape=route_out_shapes,
        grid=(nps + nss,),
        in_specs=[pl.BlockSpec((tb, D_MODEL), lambda i: (pstep(i), 0)),
                  pl.BlockSpec((tb, D_MODEL), lambda i: (sstep(i), 0)),
                  tok_spec, _const_spec((D_MODEL, D_MODEL))] + route_in_specs,
        out_specs=route_out_specs,
        compiler_params=_cparams("arbitrary"),
        name="o_proj",
    )(o_p, o_s, h2, b_w_o[0].astype(BF16), *route1)
    y2, pos2 = _moe(xs2, bucket2, 1, cols1, moe_w_gate, moe_w_up, moe_w_down)

    y_p, y_s = pl.pallas_call(
        functools.partial(_final_kernel, npt=npt),
        out_shape=(f32_rows(tp), f32_rows(ts)),
        grid_spec=pltpu.PrefetchScalarGridSpec(
            num_scalar_prefetch=1, grid=(npt + nst,),
            in_specs=[row_spec, any_spec],
            out_specs=[prow_spec, srow_spec],
            scratch_shapes=gather_scratch),
        compiler_params=_cparams("arbitrary"),
        name="final_residual",
    )(pos2, h3, y2)

    kv_shape_p = (bsz, seq, N_HEADS, HEAD_DIM)
    kv_shape_s = (dbsz, dseq, N_HEADS, HEAD_DIM)
    return (y_p.reshape(bsz, seq, D_MODEL), y_s.reshape(dbsz, dseq, D_MODEL),
            k_p.reshape(kv_shape_p), v_p.reshape(kv_shape_p), k_s.reshape(kv_shape_s), v_s.reshape(kv_shape_s),
            v_rows.reshape(1, dbsz, dseq, A_WIDTH))
```

```python
import functools

import jax
import jax.numpy as jnp
from jax import lax
from jax.experimental import pallas as pl
from jax.experimental.pallas import tpu as pltpu

F32 = jnp.float32
BF16 = jnp.bfloat16
I32 = jnp.int32

D_MODEL = 1024
CHUNK = 64
GMLP_BLOCK = 128
A_WIDTH = 2 * D_MODEL
A_HEADS = 8
A_HEAD_DIM = A_WIDTH // A_HEADS
N_HEADS = 8
HEAD_DIM = D_MODEL // N_HEADS
N_GROUPS = 4
EXPERTS_PER_GROUP = 4
EXPERT_DIM = D_MODEL // 2
EPS = 1e-6

LANES = 128
SUBLANES = 8
MXU_WIDTH = 256
SLABS = D_MODEL // LANES
assert SLABS == SUBLANES and N_HEADS == SLABS and HEAD_DIM == LANES
PAIRS = ((0, 1), (0, 2), (0, 3), (1, 2), (1, 3), (2, 3))
N_BUCKETS = N_GROUPS * len(PAIRS)
ROUTER_ROWS = 32
EXPERT_ROW0 = 8

TOKEN_BLOCK = 2 * GMLP_BLOCK
EXPERT_TILE = 256
KVQ_TILE = 256
ATTN_TILE = 256
VMEM_LIMIT = 56 * 1024 * 1024
STOP_MASS = 105.0


def _cparams(*sem):
    return pltpu.CompilerParams(dimension_semantics=sem, vmem_limit_bytes=VMEM_LIMIT)


def _rms(x, g):
    ms = jnp.mean(x * x, axis=-1, keepdims=True)
    return x * lax.rsqrt(ms + EPS) * g


def _load_rows(ref, n):
    return jnp.concatenate([ref[pl.ds(c, n, stride=SLABS), :] for c in range(SLABS)], axis=1)


def _store_rows(ref, x):
    n = x.shape[0]
    for c in range(SLABS):
        ref[pl.ds(c, n, stride=SLABS), :] = x[:, c * LANES:(c + 1) * LANES]


def _row_copy(src_hbm, src_row, buf, slot, k, sem):
    src = src_row * SLABS if isinstance(src_row, int) else pl.multiple_of(src_row * SLABS, SLABS)
    return pltpu.make_async_copy(src_hbm.at[pl.ds(src, SLABS)], buf.at[slot, pl.ds(k * SLABS, SLABS)], sem.at[slot])


def _gather_step(idx_ref, src_hbm, buf, sem, rows, n_steps=None):
    i = pl.program_id(0)
    slot = i % 2
    if n_steps is None:
        n_steps = pl.num_programs(0)

    @pl.when(i == 0)
    def _():
        for k in range(rows):
            _row_copy(src_hbm, idx_ref[k], buf, 0, k, sem).start(priority=k % 2)

    @pl.when(i + 1 < n_steps)
    def _():
        for k in range(rows):
            _row_copy(src_hbm, idx_ref[(i + 1) * rows + k], buf, 1 - slot, k, sem).start(priority=k % 2)

    @pl.when(i < n_steps)
    def _():
        for k in range(rows):
            _row_copy(src_hbm, 0, buf, slot, k, sem).wait()

    return slot


def _first_argmax(vals):
    m = vals[0]
    for v in vals[1:]:
        m = jnp.maximum(m, v)
    idx = jnp.full(m.shape, len(vals) - 1, I32)
    for r in range(len(vals) - 2, -1, -1):
        idx = jnp.where(vals[r] == m, r, idx)
    return m, idx


def _route(xn, wrt_ref, rb_ref, xs_ref, bucket_ref):
    lt = lax.dot_general(wrt_ref[...], xn, (((1,), (1,)), ((), ())),
                         precision=lax.Precision.HIGHEST, preferred_element_type=F32) + rb_ref[:, 0:1]
    _, g_idx = _first_argmax([lt[r:r + 1, :] for r in range(N_GROUPS)])
    le = []
    for e in range(EXPERTS_PER_GROUP):
        row = EXPERT_ROW0 + EXPERTS_PER_GROUP * (N_GROUPS - 1) + e
        v = lt[row:row + 1, :]
        for g in range(N_GROUPS - 2, -1, -1):
            row = EXPERT_ROW0 + EXPERTS_PER_GROUP * g + e
            v = jnp.where(g_idx == g, lt[row:row + 1, :], v)
        le.append(v)
    _, i1 = _first_argmax(le)
    _, i2 = _first_argmax([jnp.where(i1 == e, -jnp.inf, le[e]) for e in range(EXPERTS_PER_GROUP)])
    lo = jnp.minimum(i1, i2)
    hi = jnp.maximum(i1, i2)
    pair = jnp.where(lo == 0, hi - 1, jnp.where(lo == 1, hi + 1, 5))
    bucket = g_idx * len(PAIRS) + pair
    for sb in range(xn.shape[0] // LANES):
        bucket_ref[sb] = bucket[:, sb * LANES:(sb + 1) * LANES]
    _store_rows(xs_ref, xn)


def _a_layer_kernel(xp_ref, xsm_ref, ag_ref, win_ref, vg_ref, ws_ref, bs_ref, wout_ref, fg_ref, wrt_ref, rb_ref,
                    h_ref, xs_ref, bucket_ref, vrow_ref, *, nps):
    i = pl.program_id(0)
    x = jnp.where(i < nps, xp_ref[...], xsm_ref[...])
    xn = _rms(x, ag_ref[...]).astype(BF16)
    n_col = 4
    cw = 2 * A_WIDTH // n_col
    z = [jax.nn.gelu(jnp.dot(xn, win_ref[:, c * cw:(c + 1) * cw], preferred_element_type=F32)) for c in range(n_col)]
    u = jnp.concatenate(z[:n_col // 2], axis=1)
    v = _rms(jnp.concatenate(z[n_col // 2:], axis=1), vg_ref[...])

    @pl.when(i >= nps)
    def _():
        vrow_ref[...] = v

    vb = v.astype(BF16)
    gate = jnp.concatenate([
        jnp.concatenate(
            [jnp.dot(ws_ref[0, h], vb[sb * GMLP_BLOCK:(sb + 1) * GMLP_BLOCK, h * A_HEAD_DIM:(h + 1) * A_HEAD_DIM],
                     preferred_element_type=F32) for h in range(A_HEADS)], axis=1) + bs_ref[0]
        for sb in range(x.shape[0] // GMLP_BLOCK)], axis=0)
    s = (u * gate).astype(BF16)
    h1 = x + jnp.dot(s, wout_ref[...], preferred_element_type=F32)
    h_ref[...] = h1
    _route(_rms(h1, fg_ref[...]), wrt_ref, rb_ref, xs_ref, bucket_ref)


def _oproj_kernel(op_ref, osm_ref, h_in_ref, wo_ref, fg_ref, wrt_ref, rb_ref, h_ref, xs_ref, bucket_ref, *, nps):
    i = pl.program_id(0)
    o = jnp.where(i < nps, op_ref[...], osm_ref[...])
    h3 = h_in_ref[...] + jnp.dot(o, wo_ref[...], preferred_element_type=F32)
    h_ref[...] = h3
    _route(_rms(h3, fg_ref[...]), wrt_ref, rb_ref, xs_ref, bucket_ref)


def _rank_kernel(b_ref, pos_ref, tb_ref, lo_ref, hi_ref, *, tm, nt):
    bk = b_ref[...]
    nr = bk.shape[0]
    r_i = lax.broadcasted_iota(I32, (LANES, LANES), 0)
    c_i = lax.broadcasted_iota(I32, (LANES, LANES), 1)
    upper = (r_i <= c_i).astype(BF16)
    rr = lax.broadcasted_iota(I32, (nr, nr), 0)
    cc = lax.broadcasted_iota(I32, (nr, nr), 1)
    before_rows = (cc < rr).astype(BF16)
    lane = lax.broadcasted_iota(I32, (1, LANES), 1)
    tile_start = lane.astype(F32) * tm
    pos = jnp.zeros((nr, LANES), F32)
    seg_start = jnp.zeros((1, LANES), F32)
    tile_bucket = jnp.zeros((1, LANES), I32)
    pad_lo = jnp.zeros((1, LANES), F32)
    pad_hi = jnp.zeros((1, LANES), F32)
    for b in range(N_BUCKETS):
        m = bk == b
        pref = jnp.dot(jnp.where(m, 1.0, 0.0).astype(BF16), upper, preferred_element_type=F32)
        rowtot = jnp.broadcast_to(pref[:, LANES - 1:LANES], (nr, LANES))
        before = jnp.dot(before_rows, rowtot.astype(BF16), preferred_element_type=F32)
        cnt = jnp.sum(rowtot, axis=0, keepdims=True)
        pos = pos + jnp.where(m, seg_start + before + pref - 1.0, 0.0)
        pad_lo = jnp.where(lane == b, seg_start + cnt, pad_lo)
        seg_start = seg_start + jnp.ceil(cnt / tm) * tm
        pad_hi = jnp.where(lane == b, seg_start, pad_hi)
        tile_bucket = tile_bucket + (seg_start <= tile_start).astype(I32)
    pad_lo = jnp.where(lane == N_BUCKETS, seg_start, pad_lo)
    pad_hi = jnp.where(lane == N_BUCKETS, float(nt * tm), pad_hi)
    pos_ref[...] = pos.astype(I32)
    tb_ref[...] = tile_bucket
    lo_ref[...] = pad_lo.astype(I32)
    hi_ref[...] = pad_hi.astype(I32)


def _inverse_kernel(pos_ref, lo_ref, hi_ref, inv_ref, *, n_tok):
    def clear(r, carry):
        inv_ref[r] = 0
        return carry

    def put(t, carry):
        inv_ref[pos_ref[t]] = t
        return carry

    for b in range(N_BUCKETS + 1):
        lax.fori_loop(lo_ref[b], hi_ref[b], clear, 0)
    lax.fori_loop(0, n_tok, put, 0, unroll=8)


def _dispatch_kernel(inv_ref, xs_hbm, out_ref, buf, sem):
    slot = _gather_step(inv_ref, xs_hbm, buf, sem, out_ref.shape[0] // SLABS)
    out_ref[...] = buf[slot]


def _expert_tile(r, x_rows_ref, telo_ref, tehi_ref, wr_ref, rb_ref, w_refs, y_ref, wup, wdown, tm):
    wgl_ref, wul_ref, wdl_ref, wgh_ref, wuh_ref, wdh_ref = w_refs
    if True:
        prev = jnp.maximum(r - 1, 0)

        @pl.when(jnp.logical_or(r == 0, telo_ref[r] != telo_ref[prev]))
        def _():
            wup[0] = wgl_ref[0, 0].astype(BF16)
            wup[1] = wul_ref[0, 0].astype(BF16)
            wdown[0] = wdl_ref[0, 0].astype(BF16)

        @pl.when(jnp.logical_or(r == 0, tehi_ref[r] != tehi_ref[prev]))
        def _():
            wup[2] = wgh_ref[0, 0].astype(BF16)
            wup[3] = wuh_ref[0, 0].astype(BF16)
            wdown[1] = wdh_ref[0, 0].astype(BF16)

        x = _load_rows(x_rows_ref, tm).astype(BF16)
        logits = jnp.dot(x, wr_ref[...], preferred_element_type=F32) + rb_ref[...]
        lane = lax.broadcasted_iota(I32, logits.shape, 1)
        grp = telo_ref[r] // EXPERTS_PER_GROUP
        is_group = lane < N_GROUPS
        m = jnp.max(jnp.where(is_group, logits, -jnp.inf), axis=1, keepdims=True)
        ex = jnp.exp(logits - m)
        p_g = (jnp.sum(jnp.where(lane == grp, ex, 0.0), axis=1, keepdims=True)
               / jnp.sum(jnp.where(is_group, ex, 0.0), axis=1, keepdims=True))
        l_lo = jnp.sum(jnp.where(lane == EXPERT_ROW0 + telo_ref[r], logits, 0.0), axis=1, keepdims=True)
        l_hi = jnp.sum(jnp.where(lane == EXPERT_ROW0 + tehi_ref[r], logits, 0.0), axis=1, keepdims=True)
        mm = jnp.maximum(l_lo, l_hi)
        e_lo = jnp.exp(l_lo - mm)
        e_hi = jnp.exp(l_hi - mm)
        gates = (p_g * (e_lo / (e_lo + e_hi)), p_g * (e_hi / (e_lo + e_hi)))

        y = None
        for which in range(2):
            a = jnp.dot(x, wup[2 * which], preferred_element_type=F32)
            b = jnp.dot(x, wup[2 * which + 1], preferred_element_type=F32)
            hid = (jax.nn.silu(a) * b * gates[which]).astype(BF16)
            part = jnp.dot(hid, wdown[which], preferred_element_type=F32)
            y = part if y is None else y + part
        _store_rows(y_ref, y)


def _expert_kernel_tiles(telo_ref, tehi_ref, nused_ref, xs_ref, wr_ref, rb_ref, wgl_ref, wul_ref, wdl_ref,
                         wgh_ref, wuh_ref, wdh_ref, y_ref, wup, wdown, *, tm):
    r = pl.program_id(0)

    @pl.when(r < nused_ref[0])
    def _():
        _expert_tile(r, xs_ref, telo_ref, tehi_ref, wr_ref, rb_ref,
                     (wgl_ref, wul_ref, wdl_ref, wgh_ref, wuh_ref, wdh_ref), y_ref, wup, wdown, tm)

    @pl.when(r >= nused_ref[0])
    def _():
        y_ref[...] = jnp.zeros_like(y_ref)


def _expert_kernel_gather(telo_ref, tehi_ref, nused_ref, inv_ref, xs_hbm, wr_ref, rb_ref, wgl_ref, wul_ref, wdl_ref,
                          wgh_ref, wuh_ref, wdh_ref, y_ref, xbuf, wup, wdown, gsem, *, tm):
    r = pl.program_id(0)
    slot = _gather_step(inv_ref, xs_hbm, xbuf, gsem, tm, n_steps=nused_ref[0])

    @pl.when(r < nused_ref[0])
    def _():
        _expert_tile(r, xbuf.at[slot], telo_ref, tehi_ref, wr_ref, rb_ref,
                     (wgl_ref, wul_ref, wdl_ref, wgh_ref, wuh_ref, wdh_ref), y_ref, wup, wdown, tm)

    @pl.when(r >= nused_ref[0])
    def _():
        y_ref[...] = jnp.zeros_like(y_ref)


def _head_rms(x, g):
    return jnp.concatenate([_rms(x[:, h * HEAD_DIM:(h + 1) * HEAD_DIM], g) for h in range(N_HEADS)], axis=1)


def _kvq_kernel(pos_ref, h_in_ref, y_hbm, kvg_ref, wkv_ref, kng_ref, bg_ref, wq_ref, qng_ref,
                h_ref, kp_ref, vp_ref, ksm_ref, vsm_ref, kb_ref, vb_ref, qb_ref, qsm_ref, ybuf, sem, *, npt):
    i = pl.program_id(0)
    tk = h_in_ref.shape[0]
    h2 = h_in_ref[...] + _load_rows(ybuf.at[_gather_step(pos_ref, y_hbm, ybuf, sem, tk)], tk)
    h_ref[...] = h2
    kv = jnp.dot(_rms(h2, kvg_ref[...]).astype(BF16), wkv_ref[...], preferred_element_type=F32)
    q = jnp.dot(_rms(h2, bg_ref[...]).astype(BF16), wq_ref[...], preferred_element_type=F32)
    k = _head_rms(kv[:, :D_MODEL], kng_ref[...])
    v = kv[:, D_MODEL:]
    q = _head_rms(q, qng_ref[...]) * (HEAD_DIM ** -0.5)

    @pl.when(i < npt)
    def _():
        _store_rows(kp_ref, k)
        _store_rows(vp_ref, v)
        for h in range(N_HEADS):
            sl = slice(h * HEAD_DIM, (h + 1) * HEAD_DIM)
            kb_ref[0, h] = k[:, sl].astype(BF16)
            vb_ref[0, h] = v[:, sl].astype(BF16)
            qb_ref[0, h] = q[:, sl].astype(BF16)

    @pl.when(i >= npt)
    def _():
        _store_rows(ksm_ref, k)
        _store_rows(vsm_ref, v)
        qsm_ref[...] = q


def _stick_blocks(qs, ks, vs, rs, visible, later_keys):
    n = len(qs)
    zs = [lax.dot_general(qs[h], ks[h], (((1,), (1,)), ((), ())), preferred_element_type=F32) for h in range(n)]
    sps = [jnp.maximum(z, 0.0) + jnp.log(1.0 + jnp.exp(-jnp.abs(z))) for z in zs]
    if visible is not None:
        sps = [jnp.where(visible, sp, 0.0) for sp in sps]
    his = [sp.astype(BF16) for sp in sps]
    los = [(sp - hi.astype(F32)).astype(BF16) for sp, hi in zip(sps, his)]
    cs = [jnp.dot(hi, later_keys, preferred_element_type=F32) + jnp.dot(lo, later_keys, preferred_element_type=F32)
          for hi, lo in zip(his, los)]
    ws = [jnp.exp(zs[h] - cs[h] - rs[h]) for h in range(n)]
    if visible is not None:
        ws = [jnp.where(visible, w, 0.0) for w in ws]
    outs = [jnp.dot(ws[h].astype(BF16), vs[h], preferred_element_type=F32) for h in range(n)]
    return outs, [rs[h] + cs[h][:, 0:1] for h in range(n)]


def _later_keys(n):
    rowi = lax.broadcasted_iota(I32, (n, n), 0)
    coli = lax.broadcasted_iota(I32, (n, n), 1)
    return (rowi >= coli).astype(BF16)


def _min_over(rs):
    m = rs[0]
    for r in rs[1:]:
        m = jnp.minimum(m, r)
    return jnp.min(m)


def _attn_kernel(q_ref, k_ref, v_ref, o_ref, r_sc, acc_sc):
    i = pl.program_id(1)
    tq = q_ref.shape[2]
    visible = lax.broadcasted_iota(I32, (tq, tq), 1) < lax.broadcasted_iota(I32, (tq, tq), 0)
    later_keys = _later_keys(tq)
    heads = range(N_HEADS)

    def block(j, first):
        start = pl.multiple_of(j * tq, tq)
        qs = [q_ref[0, h] for h in heads]
        ks = [k_ref[0, h, pl.ds(start, tq), :] for h in heads]
        vs = [v_ref[0, h, pl.ds(start, tq), :] for h in heads]
        rs = [jnp.zeros((tq, 1), F32) if first else r_sc[h] for h in heads]
        outs, rs = _stick_blocks(qs, ks, vs, rs, visible if first else None, later_keys)
        acc = jnp.concatenate(outs, axis=1)
        if first:
            acc_sc[...] = acc
        else:
            acc_sc[...] += acc
        for h in heads:
            r_sc[h] = rs[h]
        return _min_over(rs)

    rmin = block(i, True)
    lax.while_loop(lambda st: jnp.logical_and(st[0] >= 0, st[1] <= STOP_MASS),
                   lambda st: (st[0] - 1, block(st[0], False)), (i - 1, rmin))
    o_ref[...] = acc_sc[...].astype(o_ref.dtype)


def _sample_attn_kernel(q_ref, kn_ref, vn_ref, ck_ref, cv_ref, ck_hbm, cv_hbm, o_ref, r_sc, acc_sc, kbuf, vbuf, sem,
                        *, nkc):
    b = pl.program_id(0)
    ds = q_ref.shape[0]
    ck = ck_ref.shape[0] // N_HEADS
    heads = range(N_HEADS)

    def head_rows(ref, h, n):
        return ref[pl.ds(h, n, stride=N_HEADS), :].astype(BF16)

    qs = [q_ref[:, h * HEAD_DIM:(h + 1) * HEAD_DIM].astype(BF16) for h in heads]
    visible = lax.broadcasted_iota(I32, (ds, LANES), 1) < lax.broadcasted_iota(I32, (ds, LANES), 0)
    pad = jnp.zeros((LANES - ds, HEAD_DIM), BF16)
    outs, rs = _stick_blocks(qs, [jnp.concatenate([head_rows(kn_ref, h, ds), pad], axis=0) for h in heads],
                             [jnp.concatenate([head_rows(vn_ref, h, ds), pad], axis=0) for h in heads],
                             [jnp.zeros((ds, 1), F32)] * N_HEADS, visible, _later_keys(LANES))
    acc_sc[...] = jnp.concatenate(outs, axis=1)
    for h in heads:
        r_sc[h] = rs[h]

    def chunk(kref, vref):
        outs, rs = _stick_blocks(qs, [head_rows(kref, h, ck) for h in heads], [head_rows(vref, h, ck) for h in heads],
                                 [r_sc[h] for h in heads], None, _later_keys(ck))
        acc_sc[...] += jnp.concatenate(outs, axis=1)
        for h in heads:
            r_sc[h] = rs[h]
        return _min_over(rs)

    def older(st):
        c = st[0]
        rows = ck * N_HEADS
        start = pl.multiple_of((b * nkc + c) * rows, rows)
        cpk = pltpu.make_async_copy(ck_hbm.at[pl.ds(start, rows)], kbuf, sem.at[0])
        cpv = pltpu.make_async_copy(cv_hbm.at[pl.ds(start, rows)], vbuf, sem.at[1])
        cpk.start()
        cpv.start()
        cpk.wait()
        cpv.wait()
        return c - 1, chunk(kbuf, vbuf)

    lax.while_loop(lambda st: jnp.logical_and(st[0] >= 0, st[1] <= STOP_MASS), older,
                   (nkc - 2, chunk(ck_ref, cv_ref)))
    o_ref[...] = acc_sc[...].astype(o_ref.dtype)


def _final_kernel(pos_ref, h_ref, y_hbm, yp_ref, ysm_ref, ybuf, sem, *, npt):
    i = pl.program_id(0)
    tk = h_ref.shape[0]
    y = h_ref[...] + _load_rows(ybuf.at[_gather_step(pos_ref, y_hbm, ybuf, sem, tk)], tk)

    @pl.when(i < npt)
    def _():
        yp_ref[...] = y

    @pl.when(i >= npt)
    def _():
        ysm_ref[...] = y


def _const_spec(shape):
    return pl.BlockSpec(shape, lambda *_: (0,) * len(shape))


def _router_operands(layer, ffn_norm_g, router_g_w, router_g_b, router_e_w, router_e_b):
    n_e = N_GROUPS * EXPERTS_PER_GROUP
    wrt = jnp.zeros((ROUTER_ROWS, D_MODEL), F32)
    wrt = wrt.at[:N_GROUPS].set(router_g_w[layer].T)
    wrt = wrt.at[EXPERT_ROW0:EXPERT_ROW0 + n_e].set(router_e_w[layer].transpose(0, 2, 1).reshape(n_e, D_MODEL))
    rb = jnp.zeros((ROUTER_ROWS,), F32)
    rb = rb.at[:N_GROUPS].set(router_g_b[layer])
    rb = rb.at[EXPERT_ROW0:EXPERT_ROW0 + n_e].set(router_e_b[layer].reshape(-1))
    wr_cols = jnp.zeros((D_MODEL, LANES), F32).at[:, :ROUTER_ROWS].set(wrt.T).astype(BF16)
    rb_cols = jnp.zeros((1, LANES), F32).at[0, :ROUTER_ROWS].set(rb)
    return (ffn_norm_g[layer][None, :], wrt, jnp.broadcast_to(rb[:, None], (ROUTER_ROWS, LANES))), (wr_cols, rb_cols)


def _moe(xs_rt, bucket, layer, route_cols, w_gate, w_up, w_down, gather_in_kernel):
    nblk = bucket.shape[0]
    n_tok = nblk * LANES
    tm = EXPERT_TILE
    nr = -(-nblk // LANES) * LANES
    nt = -(-(n_tok + N_BUCKETS * (tm - 1)) // tm)
    assert nt <= LANES
    bk = jnp.pad(bucket.reshape(nblk, LANES), ((0, nr - nblk), (0, 0)), constant_values=N_BUCKETS)
    lane_row = jax.ShapeDtypeStruct((1, LANES), I32)
    pos, tile_bucket, pad_lo, pad_hi = pl.pallas_call(
        functools.partial(_rank_kernel, tm=tm, nt=nt),
        out_shape=(jax.ShapeDtypeStruct((nr, LANES), I32), lane_row, lane_row, lane_row),
        name=f"moe_rank_{layer}",
    )(bk)
    pos = pos[:nblk].reshape(n_tok)
    smem = pl.BlockSpec(memory_space=pltpu.SMEM)
    inv = pl.pallas_call(
        functools.partial(_inverse_kernel, n_tok=n_tok),
        out_shape=jax.ShapeDtypeStruct((nt * tm,), I32),
        in_specs=[smem, smem, smem],
        out_specs=smem,
        name=f"moe_inverse_{layer}",
    )(pos, pad_lo[0], pad_hi[0])
    tb = tile_bucket[0, :nt]
    n_used = jnp.sum((tb < N_BUCKETS).astype(I32))
    tsrc = jnp.minimum(jnp.arange(nt, dtype=I32), n_used - 1)
    tbc = jnp.minimum(tb[tsrc], N_BUCKETS - 1)
    pair_lo = jnp.array([p[0] for p in PAIRS], I32)
    pair_hi = jnp.array([p[1] for p in PAIRS], I32)
    grp = tbc // len(PAIRS)
    telo = grp * EXPERTS_PER_GROUP + pair_lo[tbc % len(PAIRS)]
    tehi = grp * EXPERTS_PER_GROUP + pair_hi[tbc % len(PAIRS)]

    def wspec(shape, which):
        return pl.BlockSpec((1, 1) + shape, lambda r, lo, hi, *_: (layer, (lo, hi)[which][r], 0, 0))

    up_shape = (D_MODEL, EXPERT_DIM)
    down_shape = (EXPERT_DIM, D_MODEL)
    wr_cols, rb_cols = route_cols
    tile_shape = (tm * SLABS, LANES)
    tile_buf = pltpu.VMEM((2,) + tile_shape, F32)
    dma2 = pltpu.SemaphoreType.DMA((2,))
    sorted_shape = jax.ShapeDtypeStruct((nt * tm * SLABS, LANES), F32)
    weight_specs = [pl.BlockSpec((D_MODEL, LANES), lambda r, *_: (0, 0)),
                    pl.BlockSpec((1, LANES), lambda r, *_: (0, 0)),
                    wspec(up_shape, 0), wspec(up_shape, 0), wspec(down_shape, 0),
                    wspec(up_shape, 1), wspec(up_shape, 1), wspec(down_shape, 1)]
    weight_bufs = [pltpu.VMEM((4,) + up_shape, BF16), pltpu.VMEM((2,) + down_shape, BF16)]
    weights = (wr_cols, rb_cols, w_gate, w_up, w_down, w_gate, w_up, w_down)
    n_used1 = n_used.reshape(1)
    if gather_in_kernel:
        y_sorted = pl.pallas_call(
            functools.partial(_expert_kernel_gather, tm=tm),
            out_shape=sorted_shape,
            grid_spec=pltpu.PrefetchScalarGridSpec(
                num_scalar_prefetch=4, grid=(nt,),
                in_specs=[pl.BlockSpec(memory_space=pl.ANY)] + weight_specs,
                out_specs=pl.BlockSpec(tile_shape, lambda r, *_: (r, 0)),
                scratch_shapes=[tile_buf] + weight_bufs + [dma2]),
            compiler_params=_cparams("arbitrary"),
            name=f"moe_experts_{layer}",
        )(telo, tehi, n_used1, inv, xs_rt, *weights)
    else:
        xs_sorted = pl.pallas_call(
            _dispatch_kernel,
            out_shape=sorted_shape,
            grid_spec=pltpu.PrefetchScalarGridSpec(
                num_scalar_prefetch=1, grid=(nt,),
                in_specs=[pl.BlockSpec(memory_space=pl.ANY)],
                out_specs=pl.BlockSpec(tile_shape, lambda r, *_: (r, 0)),
                scratch_shapes=[tile_buf, dma2]),
            compiler_params=_cparams("arbitrary"),
            name=f"moe_dispatch_{layer}",
        )(inv, xs_rt)
        y_sorted = pl.pallas_call(
            functools.partial(_expert_kernel_tiles, tm=tm),
            out_shape=sorted_shape,
            grid_spec=pltpu.PrefetchScalarGridSpec(
                num_scalar_prefetch=3, grid=(nt,),
                in_specs=[pl.BlockSpec(tile_shape, lambda r, lo, hi, nu: (jnp.minimum(r, nu[0] - 1), 0))]
                + weight_specs,
                out_specs=pl.BlockSpec(tile_shape, lambda r, *_: (r, 0)),
                scratch_shapes=weight_bufs),
            compiler_params=_cparams("arbitrary"),
            name=f"moe_experts_{layer}",
        )(telo, tehi, n_used1, xs_sorted, *weights)
    return y_sorted, pos


def kernel(x_prompt, x_sample, cache_k, cache_v, a_norm_g, a_w_in, a_v_norm_g, a_w_s, a_b_s, a_w_out, kv_norm_g, w_kv, k_norm_g, b_norm_g, b_w_q, q_norm_g, b_w_o, ffn_norm_g, router_g_w, router_g_b, router_e_w, router_e_b, moe_w_gate, moe_w_up, moe_w_down):
    bsz, seq, _ = x_prompt.shape
    dbsz, dseq, _ = x_sample.shape
    past = cache_k.shape[1]
    tp, ts = bsz * seq, dbsz * dseq
    n_tok = tp + ts
    tb = TOKEN_BLOCK
    gb = GMLP_BLOCK
    assert seq % ATTN_TILE == 0 and tp % KVQ_TILE == 0 and ts % KVQ_TILE == 0 and tp % tb == 0 and ts % tb == 0
    assert gb % dseq == 0 and dseq <= CHUNK and past % ATTN_TILE == 0
    nps, nss = tp // tb, ts // tb
    xp = x_prompt.reshape(tp, D_MODEL)
    xsm = x_sample.reshape(ts, D_MODEL)

    def pstep(i):
        return jnp.minimum(i, nps - 1)

    def sstep(i):
        return jnp.maximum(i - nps, 0)

    def rt_shape(n):
        return jax.ShapeDtypeStruct((n * SLABS, LANES), F32)

    def rt_spec(rows, index):
        return pl.BlockSpec((rows * SLABS, LANES), lambda i, *_: (index(i), 0))

    tok_spec = pl.BlockSpec((tb, D_MODEL), lambda i: (i, 0))
    route_in_specs = [_const_spec((1, D_MODEL)), _const_spec((ROUTER_ROWS, D_MODEL)),
                      _const_spec((ROUTER_ROWS, LANES))]
    route_out_shapes = (jax.ShapeDtypeStruct((n_tok, D_MODEL), F32), rt_shape(n_tok),
                        jax.ShapeDtypeStruct((n_tok // LANES, 1, LANES), I32))
    route_out_specs = [tok_spec, rt_spec(tb, lambda i: i), pl.BlockSpec((tb // LANES, 1, LANES), lambda i: (i, 0, 0))]

    pos_i = jnp.arange(gb)
    mask = (pos_i[None, :] // CHUNK) <= (pos_i[:, None] // CHUNK)
    w_prompt = jnp.where(mask, a_w_s[0], 0.0)
    rep = gb // dseq
    w_sample = jnp.einsum("ij,hts->hitjs", jnp.eye(rep, dtype=F32), w_prompt[:, :dseq, :dseq]).reshape(
        A_HEADS, gb, gb)
    ws_all = jnp.stack([w_prompt, w_sample]).astype(BF16)
    b_prompt = a_b_s[0]
    b_sample = jnp.tile(a_b_s[0][:, :dseq], (1, rep))
    bs_all = jnp.stack([jnp.repeat(b.T, A_HEAD_DIM, axis=1) for b in (b_prompt, b_sample)])
    route0, cols0 = _router_operands(0, ffn_norm_g, router_g_w, router_g_b, router_e_w, router_e_b)
    h1, xs1, bucket1, v_rows = pl.pallas_call(
        functools.partial(_a_layer_kernel, nps=nps),
        out_shape=route_out_shapes + (jax.ShapeDtypeStruct((ts, A_WIDTH), F32),),
        grid=(nps + nss,),
        in_specs=[pl.BlockSpec((tb, D_MODEL), lambda i: (pstep(i), 0)),
                  pl.BlockSpec((tb, D_MODEL), lambda i: (sstep(i), 0)),
                  _const_spec((1, D_MODEL)), _const_spec((D_MODEL, 2 * A_WIDTH)), _const_spec((1, A_WIDTH)),
                  pl.BlockSpec((1, A_HEADS, gb, gb), lambda i: (i // nps, 0, 0, 0)),
                  pl.BlockSpec((1, gb, A_WIDTH), lambda i: (i // nps, 0, 0)),
                  _const_spec((A_WIDTH, D_MODEL))] + route_in_specs,
        out_specs=route_out_specs + [pl.BlockSpec((tb, A_WIDTH), lambda i: (sstep(i), 0))],
        compiler_params=_cparams("arbitrary"),
        name="a_layer",
    )(xp, xsm, a_norm_g[0][None, :], a_w_in[0].astype(BF16), a_v_norm_g[0][None, :], ws_all, bs_all,
      a_w_out[0].astype(BF16), *route0)
    y1, pos1 = _moe(xs1, bucket1, 0, cols0, moe_w_gate, moe_w_up, moe_w_down, False)

    tk = KVQ_TILE
    npt, nst = tp // tk, ts // tk
    spt = seq // tk

    def ptile(i):
        return jnp.minimum(i, npt - 1)

    def stile(i):
        return jnp.maximum(i - npt, 0)

    row_spec = pl.BlockSpec((tk, D_MODEL), lambda i, *_: (i, 0))
    prow_spec = pl.BlockSpec((tk, D_MODEL), lambda i, *_: (ptile(i), 0))
    srow_spec = pl.BlockSpec((tk, D_MODEL), lambda i, *_: (stile(i), 0))
    head_spec = pl.BlockSpec((1, N_HEADS, tk, HEAD_DIM), lambda i, *_: (ptile(i) // spt, 0, ptile(i) % spt, 0))
    any_spec = pl.BlockSpec(memory_space=pl.ANY)
    gather_scratch = [pltpu.VMEM((2, tk * SLABS, LANES), F32), pltpu.SemaphoreType.DMA((2,))]
    f32_rows = lambda n: jax.ShapeDtypeStruct((n, D_MODEL), F32)
    head_major = jax.ShapeDtypeStruct((bsz, N_HEADS, seq, HEAD_DIM), BF16)
    h2, k_p, v_p, k_s, v_s, kb, vb, qb, q_s = pl.pallas_call(
        functools.partial(_kvq_kernel, npt=npt),
        out_shape=(f32_rows(n_tok), rt_shape(tp), rt_shape(tp), rt_shape(ts), rt_shape(ts),
                   head_major, head_major, head_major, f32_rows(ts)),
        grid_spec=pltpu.PrefetchScalarGridSpec(
            num_scalar_prefetch=1, grid=(npt + nst,),
            in_specs=[row_spec, any_spec, _const_spec((1, D_MODEL)),
                      _const_spec((D_MODEL, 2 * D_MODEL)), _const_spec((1, HEAD_DIM)), _const_spec((1, D_MODEL)),
                      _const_spec((D_MODEL, D_MODEL)), _const_spec((1, HEAD_DIM))],
            out_specs=[row_spec, rt_spec(tk, ptile), rt_spec(tk, ptile), rt_spec(tk, stile), rt_spec(tk, stile),
                       head_spec, head_spec, head_spec, srow_spec],
            scratch_shapes=gather_scratch),
        compiler_params=_cparams("arbitrary"),
        name="kvq",
    )(pos1, h1, y1, kv_norm_g[None, :], w_kv.astype(BF16), k_norm_g[None, :], b_norm_g[0][None, :],
      b_w_q[0].astype(BF16), q_norm_g[0][None, :])

    tq = ATTN_TILE
    nq = seq // tq
    qspec = pl.BlockSpec((1, N_HEADS, tq, HEAD_DIM), lambda b, i: (b, 0, i, 0))
    kvspec = pl.BlockSpec((1, N_HEADS, seq, HEAD_DIM), lambda b, i: (b, 0, 0, 0), pipeline_mode=pl.Buffered(1))
    o_p = pl.pallas_call(
        _attn_kernel,
        out_shape=jax.ShapeDtypeStruct((tp, D_MODEL), BF16),
        grid=(bsz, nq),
        in_specs=[qspec, kvspec, kvspec],
        out_specs=pl.BlockSpec((tq, D_MODEL), lambda b, i: (b * nq + i, 0)),
        scratch_shapes=[pltpu.VMEM((N_HEADS, tq, 1), F32), pltpu.VMEM((tq, D_MODEL), F32)],
        compiler_params=_cparams("arbitrary", "arbitrary"),
        name="attn_prompt",
    )(qb, kb, vb)

    ck = ATTN_TILE
    nkc = past // ck
    new_spec = pl.BlockSpec((dseq, D_MODEL), lambda b: (b, 0))
    new_rt_spec = rt_spec(dseq, lambda b: b)
    newest_spec = rt_spec(ck, lambda b: b * nkc + nkc - 1)
    cache_k2 = cache_k.reshape(dbsz * past * N_HEADS, HEAD_DIM)
    cache_v2 = cache_v.reshape(dbsz * past * N_HEADS, HEAD_DIM)
    chunk_buf = pltpu.VMEM((ck * N_HEADS, HEAD_DIM), F32)
    o_s = pl.pallas_call(
        functools.partial(_sample_attn_kernel, nkc=nkc),
        out_shape=jax.ShapeDtypeStruct((ts, D_MODEL), BF16),
        grid=(dbsz,),
        in_specs=[new_spec, new_rt_spec, new_rt_spec, newest_spec, newest_spec, any_spec, any_spec],
        out_specs=new_spec,
        scratch_shapes=[pltpu.VMEM((N_HEADS, dseq, 1), F32), pltpu.VMEM((dseq, D_MODEL), F32), chunk_buf, chunk_buf,
                        pltpu.SemaphoreType.DMA((2,))],
        compiler_params=_cparams("arbitrary"),
        name="attn_sample",
    )(q_s, k_s, v_s, cache_k2, cache_v2, cache_k2, cache_v2)

    route1, cols1 = _router_operands(1, ffn_norm_g, router_g_w, router_g_b, router_e_w, router_e_b)
    h3, xs2, bucket2 = pl.pallas_call(
        functools.partial(_oproj_kernel, nps=nps),
        out_shape=route_out_shapes,
        grid=(nps + nss,),
        in_specs=[pl.BlockSpec((tb, D_MODEL), lambda i: (pstep(i), 0)),
                  pl.BlockSpec((tb, D_MODEL), lambda i: (sstep(i), 0)),
                  tok_spec, _const_spec((D_MODEL, D_MODEL))] + route_in_specs,
        out_specs=route_out_specs,
        compiler_params=_cparams("arbitrary"),
        name="o_proj",
    )(o_p, o_s, h2, b_w_o[0].astype(BF16), *route1)
    y2, pos2 = _moe(xs2, bucket2, 1, cols1, moe_w_gate, moe_w_up, moe_w_down, True)

    y_p, y_s = pl.pallas_call(
        functools.partial(_final_kernel, npt=npt),
        out_shape=(f32_rows(tp), f32_rows(ts)),
        grid_spec=pltpu.PrefetchScalarGridSpec(
            num_scalar_prefetch=1, grid=(npt + nst,),
            in_specs=[row_spec, any_spec],
            out_specs=[prow_spec, srow_spec],
            scratch_shapes=gather_scratch),
        compiler_params=_cparams("arbitrary"),
        name="final_residual",
    )(pos2, h3, y2)

    kv_shape_p = (bsz, seq, N_HEADS, HEAD_DIM)
    kv_shape_s = (dbsz, dseq, N_HEADS, HEAD_DIM)
    return (y_p.reshape(bsz, seq, D_MODEL), y_s.reshape(dbsz, dseq, D_MODEL),
            k_p.reshape(kv_shape_p), v_p.reshape(kv_shape_p), k_s.reshape(kv_shape_s), v_s.reshape(kv_shape_s),
            v_rows.reshape(1, dbsz, dseq, A_WIDTH))
```

```python
import functools

import jax
import jax.numpy as jnp
from jax import lax
from jax.experimental import pallas as pl
from jax.experimental.pallas import tpu as pltpu

F32 = jnp.float32
BF16 = jnp.bfloat16
I32 = jnp.int32

D_MODEL = 1024
CHUNK = 64
GMLP_BLOCK = 128
A_WIDTH = 2 * D_MODEL
A_HEADS = 8
A_HEAD_DIM = A_WIDTH // A_HEADS
N_HEADS = 8
HEAD_DIM = D_MODEL // N_HEADS
N_GROUPS = 4
EXPERTS_PER_GROUP = 4
EXPERT_DIM = D_MODEL // 2
EPS = 1e-6

LANES = 128
SUBLANES = 8
MXU_WIDTH = 256
SLABS = D_MODEL // LANES
assert SLABS == SUBLANES and N_HEADS == SLABS and HEAD_DIM == LANES
PAIRS = ((0, 1), (0, 2), (0, 3), (1, 2), (1, 3), (2, 3))
N_BUCKETS = N_GROUPS * len(PAIRS)
ROUTER_ROWS = 32
EXPERT_ROW0 = 8

TOKEN_BLOCK = 2 * GMLP_BLOCK
EXPERT_TILE = 256
KVQ_TILE = 256
ATTN_TILE = 256
VMEM_LIMIT = 56 * 1024 * 1024
STOP_MASS = 105.0


def _cparams(*sem):
    return pltpu.CompilerParams(dimension_semantics=sem, vmem_limit_bytes=VMEM_LIMIT)


def _rms(x, g):
    ms = jnp.mean(x * x, axis=-1, keepdims=True)
    return x * lax.rsqrt(ms + EPS) * g


def _load_rows(ref, n):
    return jnp.concatenate([ref[pl.ds(c, n, stride=SLABS), :] for c in range(SLABS)], axis=1)


def _store_rows(ref, x):
    n = x.shape[0]
    for c in range(SLABS):
        ref[pl.ds(c, n, stride=SLABS), :] = x[:, c * LANES:(c + 1) * LANES]


def _row_copy(src_hbm, src_row, buf, slot, k, sem):
    src = src_row * SLABS if isinstance(src_row, int) else pl.multiple_of(src_row * SLABS, SLABS)
    return pltpu.make_async_copy(src_hbm.at[pl.ds(src, SLABS)], buf.at[slot, pl.ds(k * SLABS, SLABS)], sem.at[slot])


def _gather_step(idx_ref, src_hbm, buf, sem, rows, n_steps=None):
    i = pl.program_id(0)
    slot = i % 2
    if n_steps is None:
        n_steps = pl.num_programs(0)

    @pl.when(i == 0)
    def _():
        for k in range(rows):
            _row_copy(src_hbm, idx_ref[k], buf, 0, k, sem).start(priority=k % 2)

    @pl.when(i + 1 < n_steps)
    def _():
        for k in range(rows):
            _row_copy(src_hbm, idx_ref[(i + 1) * rows + k], buf, 1 - slot, k, sem).start(priority=k % 2)

    @pl.when(i < n_steps)
    def _():
        for k in range(rows):
            _row_copy(src_hbm, 0, buf, slot, k, sem).wait()

    return slot


def _first_argmax(vals):
    m = vals[0]
    for v in vals[1:]:
        m = jnp.maximum(m, v)
    idx = jnp.full(m.shape, len(vals) - 1, I32)
    for r in range(len(vals) - 2, -1, -1):
        idx = jnp.where(vals[r] == m, r, idx)
    return m, idx


def _route(xn, wrt_ref, rb_ref, xs_ref, bucket_ref):
    lt = lax.dot_general(wrt_ref[...], xn, (((1,), (1,)), ((), ())),
                         precision=lax.Precision.HIGHEST, preferred_element_type=F32) + rb_ref[:, 0:1]
    _, g_idx = _first_argmax([lt[r:r + 1, :] for r in range(N_GROUPS)])
    le = []
    for e in range(EXPERTS_PER_GROUP):
        row = EXPERT_ROW0 + EXPERTS_PER_GROUP * (N_GROUPS - 1) + e
        v = lt[row:row + 1, :]
        for g in range(N_GROUPS - 2, -1, -1):
            row = EXPERT_ROW0 + EXPERTS_PER_GROUP * g + e
            v = jnp.where(g_idx == g, lt[row:row + 1, :], v)
        le.append(v)
    _, i1 = _first_argmax(le)
    _, i2 = _first_argmax([jnp.where(i1 == e, -jnp.inf, le[e]) for e in range(EXPERTS_PER_GROUP)])
    lo = jnp.minimum(i1, i2)
    hi = jnp.maximum(i1, i2)
    pair = jnp.where(lo == 0, hi - 1, jnp.where(lo == 1, hi + 1, 5))
    bucket = g_idx * len(PAIRS) + pair
    for sb in range(xn.shape[0] // LANES):
        bucket_ref[sb] = bucket[:, sb * LANES:(sb + 1) * LANES]
    _store_rows(xs_ref, xn)


def _a_layer_kernel(xp_ref, xsm_ref, ag_ref, win_ref, vg_ref, ws_ref, bs_ref, wout_ref, fg_ref, wrt_ref, rb_ref,
                    h_ref, xs_ref, bucket_ref, vrow_ref, *, nps):
    i = pl.program_id(0)
    x = jnp.where(i < nps, xp_ref[...], xsm_ref[...])
    xn = _rms(x, ag_ref[...]).astype(BF16)
    n_col = 4
    cw = 2 * A_WIDTH // n_col
    z = [jax.nn.gelu(jnp.dot(xn, win_ref[:, c * cw:(c + 1) * cw], preferred_element_type=F32)) for c in range(n_col)]
    u = jnp.concatenate(z[:n_col // 2], axis=1)
    v = _rms(jnp.concatenate(z[n_col // 2:], axis=1), vg_ref[...])

    @pl.when(i >= nps)
    def _():
        vrow_ref[...] = v

    vb = v.astype(BF16)
    gate = jnp.concatenate([
        jnp.concatenate(
            [jnp.dot(ws_ref[0, h], vb[sb * GMLP_BLOCK:(sb + 1) * GMLP_BLOCK, h * A_HEAD_DIM:(h + 1) * A_HEAD_DIM],
                     preferred_element_type=F32) for h in range(A_HEADS)], axis=1) + bs_ref[0]
        for sb in range(x.shape[0] // GMLP_BLOCK)], axis=0)
    s = (u * gate).astype(BF16)
    h1 = x + jnp.dot(s, wout_ref[...], preferred_element_type=F32)
    h_ref[...] = h1
    _route(_rms(h1, fg_ref[...]), wrt_ref, rb_ref, xs_ref, bucket_ref)


def _oproj_kernel(op_ref, osm_ref, h_in_ref, wo_ref, fg_ref, wrt_ref, rb_ref, h_ref, xs_ref, bucket_ref, *, nps):
    i = pl.program_id(0)
    o = jnp.where(i < nps, op_ref[...], osm_ref[...])
    h3 = h_in_ref[...] + jnp.dot(o, wo_ref[...], preferred_element_type=F32)
    h_ref[...] = h3
    _route(_rms(h3, fg_ref[...]), wrt_ref, rb_ref, xs_ref, bucket_ref)


def _rank_kernel(b_ref, pos_ref, tb_ref, lo_ref, hi_ref, *, tm, nt):
    bk = b_ref[...]
    nr = bk.shape[0]
    r_i = lax.broadcasted_iota(I32, (LANES, LANES), 0)
    c_i = lax.broadcasted_iota(I32, (LANES, LANES), 1)
    upper = (r_i <= c_i).astype(BF16)
    rr = lax.broadcasted_iota(I32, (nr, nr), 0)
    cc = lax.broadcasted_iota(I32, (nr, nr), 1)
    before_rows = (cc < rr).astype(BF16)
    lane = lax.broadcasted_iota(I32, (1, LANES), 1)
    tile_start = lane.astype(F32) * tm
    pos = jnp.zeros((nr, LANES), F32)
    seg_start = jnp.zeros((1, LANES), F32)
    tile_bucket = jnp.zeros((1, LANES), I32)
    pad_lo = jnp.zeros((1, LANES), F32)
    pad_hi = jnp.zeros((1, LANES), F32)
    for b in range(N_BUCKETS):
        m = bk == b
        pref = jnp.dot(jnp.where(m, 1.0, 0.0).astype(BF16), upper, preferred_element_type=F32)
        rowtot = jnp.broadcast_to(pref[:, LANES - 1:LANES], (nr, LANES))
        before = jnp.dot(before_rows, rowtot.astype(BF16), preferred_element_type=F32)
        cnt = jnp.sum(rowtot, axis=0, keepdims=True)
        pos = pos + jnp.where(m, seg_start + before + pref - 1.0, 0.0)
        pad_lo = jnp.where(lane == b, seg_start + cnt, pad_lo)
        seg_start = seg_start + jnp.ceil(cnt / tm) * tm
        pad_hi = jnp.where(lane == b, seg_start, pad_hi)
        tile_bucket = tile_bucket + (seg_start <= tile_start).astype(I32)
    pad_lo = jnp.where(lane == N_BUCKETS, seg_start, pad_lo)
    pad_hi = jnp.where(lane == N_BUCKETS, float(nt * tm), pad_hi)
    pos_ref[...] = pos.astype(I32)
    tb_ref[...] = tile_bucket
    lo_ref[...] = pad_lo.astype(I32)
    hi_ref[...] = pad_hi.astype(I32)


def _inverse_kernel(pos_ref, lo_ref, hi_ref, inv_ref, *, n_tok):
    def clear(r, carry):
        inv_ref[r] = jnp.where(r < n_tok, r, r - n_tok)
        return carry

    def put(t, carry):
        inv_ref[pos_ref[t]] = t
        return carry

    for b in range(N_BUCKETS + 1):
        lax.fori_loop(lo_ref[b], hi_ref[b], clear, 0)
    lax.fori_loop(0, n_tok, put, 0, unroll=8)


def _dispatch_kernel(inv_ref, xs_hbm, out_ref, buf, sem):
    slot = _gather_step(inv_ref, xs_hbm, buf, sem, out_ref.shape[0] // SLABS)
    out_ref[...] = buf[slot]


def _expert_tile(r, x_rows_ref, telo_ref, tehi_ref, wr_ref, rb_ref, w_refs, y_ref, wup, wdown, tm):
    wgl_ref, wul_ref, wdl_ref, wgh_ref, wuh_ref, wdh_ref = w_refs
    if True:
        prev = jnp.maximum(r - 1, 0)

        @pl.when(jnp.logical_or(r == 0, telo_ref[r] != telo_ref[prev]))
        def _():
            wup[0] = wgl_ref[0, 0].astype(BF16)
            wup[1] = wul_ref[0, 0].astype(BF16)
            wdown[0] = wdl_ref[0, 0].astype(BF16)

        @pl.when(jnp.logical_or(r == 0, tehi_ref[r] != tehi_ref[prev]))
        def _():
            wup[2] = wgh_ref[0, 0].astype(BF16)
            wup[3] = wuh_ref[0, 0].astype(BF16)
            wdown[1] = wdh_ref[0, 0].astype(BF16)

        x = _load_rows(x_rows_ref, tm).astype(BF16)
        logits = jnp.dot(x, wr_ref[...], preferred_element_type=F32) + rb_ref[...]
        lane = lax.broadcasted_iota(I32, logits.shape, 1)
        grp = telo_ref[r] // EXPERTS_PER_GROUP
        is_group = lane < N_GROUPS
        m = jnp.max(jnp.where(is_group, logits, -jnp.inf), axis=1, keepdims=True)
        ex = jnp.exp(logits - m)
        p_g = (jnp.sum(jnp.where(lane == grp, ex, 0.0), axis=1, keepdims=True)
               / jnp.sum(jnp.where(is_group, ex, 0.0), axis=1, keepdims=True))
        l_lo = jnp.sum(jnp.where(lane == EXPERT_ROW0 + telo_ref[r], logits, 0.0), axis=1, keepdims=True)
        l_hi = jnp.sum(jnp.where(lane == EXPERT_ROW0 + tehi_ref[r], logits, 0.0), axis=1, keepdims=True)
        mm = jnp.maximum(l_lo, l_hi)
        e_lo = jnp.exp(l_lo - mm)
        e_hi = jnp.exp(l_hi - mm)
        gates = (p_g * (e_lo / (e_lo + e_hi)), p_g * (e_hi / (e_lo + e_hi)))

        y = None
        for which in range(2):
            a = jnp.dot(x, wup[2 * which], preferred_element_type=F32)
            b = jnp.dot(x, wup[2 * which + 1], preferred_element_type=F32)
            hid = (jax.nn.silu(a) * b * gates[which]).astype(BF16)
            part = jnp.dot(hid, wdown[which], preferred_element_type=F32)
            y = part if y is None else y + part
        _store_rows(y_ref, y)


def _expert_kernel_tiles(telo_ref, tehi_ref, nused_ref, xs_ref, wr_ref, rb_ref, wgl_ref, wul_ref, wdl_ref,
                         wgh_ref, wuh_ref, wdh_ref, y_ref, wup, wdown, *, tm):
    r = pl.program_id(0)

    @pl.when(r < nused_ref[0])
    def _():
        _expert_tile(r, xs_ref, telo_ref, tehi_ref, wr_ref, rb_ref,
                     (wgl_ref, wul_ref, wdl_ref, wgh_ref, wuh_ref, wdh_ref), y_ref, wup, wdown, tm)

    @pl.when(r >= nused_ref[0])
    def _():
        y_ref[...] = jnp.zeros_like(y_ref)


def _expert_kernel_gather(telo_ref, tehi_ref, nused_ref, inv_ref, xs_hbm, wr_ref, rb_ref, wgl_ref, wul_ref, wdl_ref,
                          wgh_ref, wuh_ref, wdh_ref, y_ref, xbuf, wup, wdown, gsem, *, tm):
    r = pl.program_id(0)
    slot = _gather_step(inv_ref, xs_hbm, xbuf, gsem, tm, n_steps=nused_ref[0])

    @pl.when(r < nused_ref[0])
    def _():
        _expert_tile(r, xbuf.at[slot], telo_ref, tehi_ref, wr_ref, rb_ref,
                     (wgl_ref, wul_ref, wdl_ref, wgh_ref, wuh_ref, wdh_ref), y_ref, wup, wdown, tm)

    @pl.when(r >= nused_ref[0])
    def _():
        y_ref[...] = jnp.zeros_like(y_ref)


def _head_rms(x, g):
    return jnp.concatenate([_rms(x[:, h * HEAD_DIM:(h + 1) * HEAD_DIM], g) for h in range(N_HEADS)], axis=1)


def _kvq_kernel(pos_ref, h_in_ref, y_hbm, kvg_ref, wkv_ref, kng_ref, bg_ref, wq_ref, qng_ref,
                h_ref, kp_ref, vp_ref, ksm_ref, vsm_ref, kb_ref, vb_ref, qb_ref, qsm_ref, ybuf, sem, *, npt):
    i = pl.program_id(0)
    tk = h_in_ref.shape[0]
    h2 = h_in_ref[...] + _load_rows(ybuf.at[_gather_step(pos_ref, y_hbm, ybuf, sem, tk)], tk)
    h_ref[...] = h2
    kv = jnp.dot(_rms(h2, kvg_ref[...]).astype(BF16), wkv_ref[...], preferred_element_type=F32)
    q = jnp.dot(_rms(h2, bg_ref[...]).astype(BF16), wq_ref[...], preferred_element_type=F32)
    k = _head_rms(kv[:, :D_MODEL], kng_ref[...])
    v = kv[:, D_MODEL:]
    q = _head_rms(q, qng_ref[...]) * (HEAD_DIM ** -0.5)

    @pl.when(i < npt)
    def _():
        _store_rows(kp_ref, k)
        _store_rows(vp_ref, v)
        for h in range(N_HEADS):
            sl = slice(h * HEAD_DIM, (h + 1) * HEAD_DIM)
            kb_ref[0, h] = k[:, sl].astype(BF16)
            vb_ref[0, h] = v[:, sl].astype(BF16)
            qb_ref[0, h] = q[:, sl].astype(BF16)

    @pl.when(i >= npt)
    def _():
        _store_rows(ksm_ref, k)
        _store_rows(vsm_ref, v)
        qsm_ref[...] = q


def _stick_blocks(qs, ks, vs, rs, visible, later_keys):
    n = len(qs)
    zs = [lax.dot_general(qs[h], ks[h], (((1,), (1,)), ((), ())), preferred_element_type=F32) for h in range(n)]
    sps = [jnp.maximum(z, 0.0) + jnp.log(1.0 + jnp.exp(-jnp.abs(z))) for z in zs]
    if visible is not None:
        sps = [jnp.where(visible, sp, 0.0) for sp in sps]
    his = [sp.astype(BF16) for sp in sps]
    los = [(sp - hi.astype(F32)).astype(BF16) for sp, hi in zip(sps, his)]
    cs = [jnp.dot(hi, later_keys, preferred_element_type=F32) + jnp.dot(lo, later_keys, preferred_element_type=F32)
          for hi, lo in zip(his, los)]
    ws = [jnp.exp(zs[h] - cs[h] - rs[h]) for h in range(n)]
    if visible is not None:
        ws = [jnp.where(visible, w, 0.0) for w in ws]
    outs = [jnp.dot(ws[h].astype(BF16), vs[h], preferred_element_type=F32) for h in range(n)]
    return outs, [rs[h] + cs[h][:, 0:1] for h in range(n)]


def _later_keys(n):
    rowi = lax.broadcasted_iota(I32, (n, n), 0)
    coli = lax.broadcasted_iota(I32, (n, n), 1)
    return (rowi >= coli).astype(BF16)


def _min_over(rs):
    m = rs[0]
    for r in rs[1:]:
        m = jnp.minimum(m, r)
    return jnp.min(m)


def _attn_kernel(q_ref, k_ref, v_ref, o_ref, r_sc, acc_sc):
    i = pl.program_id(1)
    tq = q_ref.shape[2]
    visible = lax.broadcasted_iota(I32, (tq, tq), 1) < lax.broadcasted_iota(I32, (tq, tq), 0)
    later_keys = _later_keys(tq)
    heads = range(N_HEADS)

    def block(j, first):
        start = pl.multiple_of(j * tq, tq)
        qs = [q_ref[0, h] for h in heads]
        ks = [k_ref[0, h, pl.ds(start, tq), :] for h in heads]
        vs = [v_ref[0, h, pl.ds(start, tq), :] for h in heads]
        rs = [jnp.zeros((tq, 1), F32) if first else r_sc[h] for h in heads]
        outs, rs = _stick_blocks(qs, ks, vs, rs, visible if first else None, later_keys)
        acc = jnp.concatenate(outs, axis=1)
        if first:
            acc_sc[...] = acc
        else:
            acc_sc[...] += acc
        for h in heads:
            r_sc[h] = rs[h]
        return _min_over(rs)

    rmin = block(i, True)
    lax.while_loop(lambda st: jnp.logical_and(st[0] >= 0, st[1] <= STOP_MASS),
                   lambda st: (st[0] - 1, block(st[0], False)), (i - 1, rmin))
    o_ref[...] = acc_sc[...].astype(o_ref.dtype)


def _sample_attn_kernel(q_ref, kn_ref, vn_ref, ck_ref, cv_ref, ck_hbm, cv_hbm, o_ref, r_sc, acc_sc, kbuf, vbuf, sem,
                        *, nkc):
    b = pl.program_id(0)
    ds = q_ref.shape[0]
    ck = ck_ref.shape[0] // N_HEADS
    heads = range(N_HEADS)

    def head_rows(ref, h, n):
        return ref[pl.ds(h, n, stride=N_HEADS), :].astype(BF16)

    qs = [q_ref[:, h * HEAD_DIM:(h + 1) * HEAD_DIM].astype(BF16) for h in heads]
    visible = lax.broadcasted_iota(I32, (ds, LANES), 1) < lax.broadcasted_iota(I32, (ds, LANES), 0)
    pad = jnp.zeros((LANES - ds, HEAD_DIM), BF16)
    outs, rs = _stick_blocks(qs, [jnp.concatenate([head_rows(kn_ref, h, ds), pad], axis=0) for h in heads],
                             [jnp.concatenate([head_rows(vn_ref, h, ds), pad], axis=0) for h in heads],
                             [jnp.zeros((ds, 1), F32)] * N_HEADS, visible, _later_keys(LANES))
    acc_sc[...] = jnp.concatenate(outs, axis=1)
    for h in heads:
        r_sc[h] = rs[h]

    def chunk(kref, vref):
        outs, rs = _stick_blocks(qs, [head_rows(kref, h, ck) for h in heads], [head_rows(vref, h, ck) for h in heads],
                                 [r_sc[h] for h in heads], None, _later_keys(ck))
        acc_sc[...] += jnp.concatenate(outs, axis=1)
        for h in heads:
            r_sc[h] = rs[h]
        return _min_over(rs)

    def older(st):
        c = st[0]
        rows = ck * N_HEADS
        start = pl.multiple_of((b * nkc + c) * rows, rows)
        cpk = pltpu.make_async_copy(ck_hbm.at[pl.ds(start, rows)], kbuf, sem.at[0])
        cpv = pltpu.make_async_copy(cv_hbm.at[pl.ds(start, rows)], vbuf, sem.at[1])
        cpk.start()
        cpv.start()
        cpk.wait()
        cpv.wait()
        return c - 1, chunk(kbuf, vbuf)

    lax.while_loop(lambda st: jnp.logical_and(st[0] >= 0, st[1] <= STOP_MASS), older,
                   (nkc - 2, chunk(ck_ref, cv_ref)))
    o_ref[...] = acc_sc[...].astype(o_ref.dtype)


def _final_kernel(pos_ref, h_ref, y_hbm, yp_ref, ysm_ref, ybuf, sem, *, npt):
    i = pl.program_id(0)
    tk = h_ref.shape[0]
    y = h_ref[...] + _load_rows(ybuf.at[_gather_step(pos_ref, y_hbm, ybuf, sem, tk)], tk)

    @pl.when(i < npt)
    def _():
        yp_ref[...] = y

    @pl.when(i >= npt)
    def _():
        ysm_ref[...] = y


def _const_spec(shape):
    return pl.BlockSpec(shape, lambda *_: (0,) * len(shape))


def _router_operands(layer, ffn_norm_g, router_g_w, router_g_b, router_e_w, router_e_b):
    n_e = N_GROUPS * EXPERTS_PER_GROUP
    wrt = jnp.zeros((ROUTER_ROWS, D_MODEL), F32)
    wrt = wrt.at[:N_GROUPS].set(router_g_w[layer].T)
    wrt = wrt.at[EXPERT_ROW0:EXPERT_ROW0 + n_e].set(router_e_w[layer].transpose(0, 2, 1).reshape(n_e, D_MODEL))
    rb = jnp.zeros((ROUTER_ROWS,), F32)
    rb = rb.at[:N_GROUPS].set(router_g_b[layer])
    rb = rb.at[EXPERT_ROW0:EXPERT_ROW0 + n_e].set(router_e_b[layer].reshape(-1))
    wr_cols = jnp.zeros((D_MODEL, LANES), F32).at[:, :ROUTER_ROWS].set(wrt.T).astype(BF16)
    rb_cols = jnp.zeros((1, LANES), F32).at[0, :ROUTER_ROWS].set(rb)
    return (ffn_norm_g[layer][None, :], wrt, jnp.broadcast_to(rb[:, None], (ROUTER_ROWS, LANES))), (wr_cols, rb_cols)


def _moe(xs_rt, bucket, layer, route_cols, w_gate, w_up, w_down, gather_in_kernel):
    nblk = bucket.shape[0]
    n_tok = nblk * LANES
    tm = EXPERT_TILE
    nr = -(-nblk // LANES) * LANES
    nt = -(-(n_tok + N_BUCKETS * (tm - 1)) // tm)
    assert nt <= LANES
    bk = jnp.pad(bucket.reshape(nblk, LANES), ((0, nr - nblk), (0, 0)), constant_values=N_BUCKETS)
    lane_row = jax.ShapeDtypeStruct((1, LANES), I32)
    pos, tile_bucket, pad_lo, pad_hi = pl.pallas_call(
        functools.partial(_rank_kernel, tm=tm, nt=nt),
        out_shape=(jax.ShapeDtypeStruct((nr, LANES), I32), lane_row, lane_row, lane_row),
        name=f"moe_rank_{layer}",
    )(bk)
    pos = pos[:nblk].reshape(n_tok)
    smem = pl.BlockSpec(memory_space=pltpu.SMEM)
    inv = pl.pallas_call(
        functools.partial(_inverse_kernel, n_tok=n_tok),
        out_shape=jax.ShapeDtypeStruct((nt * tm,), I32),
        in_specs=[smem, smem, smem],
        out_specs=smem,
        name=f"moe_inverse_{layer}",
    )(pos, pad_lo[0], pad_hi[0])
    tb = tile_bucket[0, :nt]
    n_used = jnp.sum((tb < N_BUCKETS).astype(I32))
    tsrc = jnp.minimum(jnp.arange(nt, dtype=I32), n_used - 1)
    tbc = jnp.minimum(tb[tsrc], N_BUCKETS - 1)
    pair_lo = jnp.array([p[0] for p in PAIRS], I32)
    pair_hi = jnp.array([p[1] for p in PAIRS], I32)
    grp = tbc // len(PAIRS)
    telo = grp * EXPERTS_PER_GROUP + pair_lo[tbc % len(PAIRS)]
    tehi = grp * EXPERTS_PER_GROUP + pair_hi[tbc % len(PAIRS)]

    def wspec(shape, which):
        return pl.BlockSpec((1, 1) + shape, lambda r, lo, hi, *_: (layer, (lo, hi)[which][r], 0, 0))

    up_shape = (D_MODEL, EXPERT_DIM)
    down_shape = (EXPERT_DIM, D_MODEL)
    wr_cols, rb_cols = route_cols
    tile_shape = (tm * SLABS, LANES)
    tile_buf = pltpu.VMEM((2,) + tile_shape, F32)
    dma2 = pltpu.SemaphoreType.DMA((2,))
    sorted_shape = jax.ShapeDtypeStruct((nt * tm * SLABS, LANES), F32)
    weight_specs = [pl.BlockSpec((D_MODEL, LANES), lambda r, *_: (0, 0)),
                    pl.BlockSpec((1, LANES), lambda r, *_: (0, 0)),
                    wspec(up_shape, 0), wspec(up_shape, 0), wspec(down_shape, 0),
                    wspec(up_shape, 1), wspec(up_shape, 1), wspec(down_shape, 1)]
    weight_bufs = [pltpu.VMEM((4,) + up_shape, BF16), pltpu.VMEM((2,) + down_shape, BF16)]
    weights = (wr_cols, rb_cols, w_gate, w_up, w_down, w_gate, w_up, w_down)
    n_used1 = n_used.reshape(1)
    if gather_in_kernel:
        y_sorted = pl.pallas_call(
            functools.partial(_expert_kernel_gather, tm=tm),
            out_shape=sorted_shape,
            grid_spec=pltpu.PrefetchScalarGridSpec(
                num_scalar_prefetch=4, grid=(nt,),
                in_specs=[pl.BlockSpec(memory_space=pl.ANY)] + weight_specs,
                out_specs=pl.BlockSpec(tile_shape, lambda r, *_: (r, 0)),
                scratch_shapes=[tile_buf] + weight_bufs + [dma2]),
            compiler_params=_cparams("arbitrary"),
            name=f"moe_experts_{layer}",
        )(telo, tehi, n_used1, inv, xs_rt, *weights)
    else:
        xs_sorted = pl.pallas_call(
            _dispatch_kernel,
            out_shape=sorted_shape,
            grid_spec=pltpu.PrefetchScalarGridSpec(
                num_scalar_prefetch=1, grid=(nt,),
                in_specs=[pl.BlockSpec(memory_space=pl.ANY)],
                out_specs=pl.BlockSpec(tile_shape, lambda r, *_: (r, 0)),
                scratch_shapes=[tile_buf, dma2]),
            compiler_params=_cparams("arbitrary"),
            name=f"moe_dispatch_{layer}",
        )(inv, xs_rt)
        y_sorted = pl.pallas_call(
            functools.partial(_expert_kernel_tiles, tm=tm),
            out_shape=sorted_shape,
            grid_spec=pltpu.PrefetchScalarGridSpec(
                num_scalar_prefetch=3, grid=(nt,),
                in_specs=[pl.BlockSpec(tile_shape, lambda r, lo, hi, nu: (jnp.minimum(r, nu[0] - 1), 0))]
                + weight_specs,
                out_specs=pl.BlockSpec(tile_shape, lambda r, *_: (r, 0)),
                scratch_shapes=weight_bufs),
            compiler_params=_cparams("arbitrary"),
            name=f"moe_experts_{layer}",
        )(telo, tehi, n_used1, xs_sorted, *weights)
    return y_sorted, pos


def kernel(x_prompt, x_sample, cache_k, cache_v, a_norm_g, a_w_in, a_v_norm_g, a_w_s, a_b_s, a_w_out, kv_norm_g, w_kv, k_norm_g, b_norm_g, b_w_q, q_norm_g, b_w_o, ffn_norm_g, router_g_w, router_g_b, router_e_w, router_e_b, moe_w_gate, moe_w_up, moe_w_down):
    bsz, seq, _ = x_prompt.shape
    dbsz, dseq, _ = x_sample.shape
    past = cache_k.shape[1]
    tp, ts = bsz * seq, dbsz * dseq
    n_tok = tp + ts
    tb = TOKEN_BLOCK
    gb = GMLP_BLOCK
    assert seq % ATTN_TILE == 0 and tp % KVQ_TILE == 0 and ts % KVQ_TILE == 0 and tp % tb == 0 and ts % tb == 0
    assert gb % dseq == 0 and dseq <= CHUNK and past % ATTN_TILE == 0
    nps, nss = tp // tb, ts // tb
    xp = x_prompt.reshape(tp, D_MODEL)
    xsm = x_sample.reshape(ts, D_MODEL)

    def pstep(i):
        return jnp.minimum(i, nps - 1)

    def sstep(i):
        return jnp.maximum(i - nps, 0)

    def rt_shape(n):
        return jax.ShapeDtypeStruct((n * SLABS, LANES), F32)

    def rt_spec(rows, index):
        return pl.BlockSpec((rows * SLABS, LANES), lambda i, *_: (index(i), 0))

    tok_spec = pl.BlockSpec((tb, D_MODEL), lambda i: (i, 0))
    route_in_specs = [_const_spec((1, D_MODEL)), _const_spec((ROUTER_ROWS, D_MODEL)),
                      _const_spec((ROUTER_ROWS, LANES))]
    route_out_shapes = (jax.ShapeDtypeStruct((n_tok, D_MODEL), F32), rt_shape(n_tok),
                        jax.ShapeDtypeStruct((n_tok // LANES, 1, LANES), I32))
    route_out_specs = [tok_spec, rt_spec(tb, lambda i: i), pl.BlockSpec((tb // LANES, 1, LANES), lambda i: (i, 0, 0))]

    pos_i = jnp.arange(gb)
    mask = (pos_i[None, :] // CHUNK) <= (pos_i[:, None] // CHUNK)
    w_prompt = jnp.where(mask, a_w_s[0], 0.0)
    rep = gb // dseq
    w_sample = jnp.einsum("ij,hts->hitjs", jnp.eye(rep, dtype=F32), w_prompt[:, :dseq, :dseq]).reshape(
        A_HEADS, gb, gb)
    ws_all = jnp.stack([w_prompt, w_sample]).astype(BF16)
    b_prompt = a_b_s[0]
    b_sample = jnp.tile(a_b_s[0][:, :dseq], (1, rep))
    bs_all = jnp.stack([jnp.repeat(b.T, A_HEAD_DIM, axis=1) for b in (b_prompt, b_sample)])
    route0, cols0 = _router_operands(0, ffn_norm_g, router_g_w, router_g_b, router_e_w, router_e_b)
    h1, xs1, bucket1, v_rows = pl.pallas_call(
        functools.partial(_a_layer_kernel, nps=nps),
        out_shape=route_out_shapes + (jax.ShapeDtypeStruct((ts, A_WIDTH), F32),),
        grid=(nps + nss,),
        in_specs=[pl.BlockSpec((tb, D_MODEL), lambda i: (pstep(i), 0)),
                  pl.BlockSpec((tb, D_MODEL), lambda i: (sstep(i), 0)),
                  _const_spec((1, D_MODEL)), _const_spec((D_MODEL, 2 * A_WIDTH)), _const_spec((1, A_WIDTH)),
                  pl.BlockSpec((1, A_HEADS, gb, gb), lambda i: (i // nps, 0, 0, 0)),
                  pl.BlockSpec((1, gb, A_WIDTH), lambda i: (i // nps, 0, 0)),
                  _const_spec((A_WIDTH, D_MODEL))] + route_in_specs,
        out_specs=route_out_specs + [pl.BlockSpec((tb, A_WIDTH), lambda i: (sstep(i), 0))],
        compiler_params=_cparams("arbitrary"),
        name="a_layer",
    )(xp, xsm, a_norm_g[0][None, :], a_w_in[0].astype(BF16), a_v_norm_g[0][None, :], ws_all, bs_all,
      a_w_out[0].astype(BF16), *route0)
    y1, pos1 = _moe(xs1, bucket1, 0, cols0, moe_w_gate, moe_w_up, moe_w_down, False)

    tk = KVQ_TILE
    npt, nst = tp // tk, ts // tk
    spt = seq // tk

    def ptile(i):
        return jnp.minimum(i, npt - 1)

    def stile(i):
        return jnp.maximum(i - npt, 0)

    row_spec = pl.BlockSpec((tk, D_MODEL), lambda i, *_: (i, 0))
    prow_spec = pl.BlockSpec((tk, D_MODEL), lambda i, *_: (ptile(i), 0))
    srow_spec = pl.BlockSpec((tk, D_MODEL), lambda i, *_: (stile(i), 0))
    head_spec = pl.BlockSpec((1, N_HEADS, tk, HEAD_DIM), lambda i, *_: (ptile(i) // spt, 0, ptile(i) % spt, 0))
    any_spec = pl.BlockSpec(memory_space=pl.ANY)
    gather_scratch = [pltpu.VMEM((2, tk * SLABS, LANES), F32), pltpu.SemaphoreType.DMA((2,))]
    f32_rows = lambda n: jax.ShapeDtypeStruct((n, D_MODEL), F32)
    head_major = jax.ShapeDtypeStruct((bsz, N_HEADS, seq, HEAD_DIM), BF16)
    h2, k_p, v_p, k_s, v_s, kb, vb, qb, q_s = pl.pallas_call(
        functools.partial(_kvq_kernel, npt=npt),
        out_shape=(f32_rows(n_tok), rt_shape(tp), rt_shape(tp), rt_shape(ts), rt_shape(ts),
                   head_major, head_major, head_major, f32_rows(ts)),
        grid_spec=pltpu.PrefetchScalarGridSpec(
            num_scalar_prefetch=1, grid=(npt + nst,),
            in_specs=[row_spec, any_spec, _const_spec((1, D_MODEL)),
                      _const_spec((D_MODEL, 2 * D_MODEL)), _const_spec((1, HEAD_DIM)), _const_spec((1, D_MODEL)),
                      _const_spec((D_MODEL, D_MODEL)), _const_spec((1, HEAD_DIM))],
            out_specs=[row_spec, rt_spec(tk, ptile), rt_spec(tk, ptile), rt_spec(tk, stile), rt_spec(tk, stile),
                       head_spec, head_spec, head_spec, srow_spec],
            scratch_shapes=gather_scratch),
        compiler_params=_cparams("arbitrary"),
        name="kvq",
    )(pos1, h1, y1, kv_norm_g[None, :], w_kv.astype(BF16), k_norm_g[None, :], b_norm_g[0][None, :],
      b_w_q[0].astype(BF16), q_norm_g[0][None, :])

    tq = ATTN_TILE
    nq = seq // tq
    qspec = pl.BlockSpec((1, N_HEADS, tq, HEAD_DIM), lambda b, i: (b, 0, i, 0))
    kvspec = pl.BlockSpec((1, N_HEADS, seq, HEAD_DIM), lambda b, i: (b, 0, 0, 0), pipeline_mode=pl.Buffered(1))
    o_p = pl.pallas_call(
        _attn_kernel,
        out_shape=jax.ShapeDtypeStruct((tp, D_MODEL), BF16),
        grid=(bsz, nq),
        in_specs=[qspec, kvspec, kvspec],
        out_specs=pl.BlockSpec((tq, D_MODEL), lambda b, i: (b * nq + i, 0)),
        scratch_shapes=[pltpu.VMEM((N_HEADS, tq, 1), F32), pltpu.VMEM((tq, D_MODEL), F32)],
        compiler_params=_cparams("arbitrary", "arbitrary"),
        name="attn_prompt",
    )(qb, kb, vb)

    ck = ATTN_TILE
    nkc = past // ck
    new_spec = pl.BlockSpec((dseq, D_MODEL), lambda b: (b, 0))
    new_rt_spec = rt_spec(dseq, lambda b: b)
    newest_spec = rt_spec(ck, lambda b: b * nkc + nkc - 1)
    cache_k2 = cache_k.reshape(dbsz * past * N_HEADS, HEAD_DIM)
    cache_v2 = cache_v.reshape(dbsz * past * N_HEADS, HEAD_DIM)
    chunk_buf = pltpu.VMEM((ck * N_HEADS, HEAD_DIM), F32)
    o_s = pl.pallas_call(
        functools.partial(_sample_attn_kernel, nkc=nkc),
        out_shape=jax.ShapeDtypeStruct((ts, D_MODEL), BF16),
        grid=(dbsz,),
        in_specs=[new_spec, new_rt_spec, new_rt_spec, newest_spec, newest_spec, any_spec, any_spec],
        out_specs=new_spec,
        scratch_shapes=[pltpu.VMEM((N_HEADS, dseq, 1), F32), pltpu.VMEM((dseq, D_MODEL), F32), chunk_buf, chunk_buf,
                        pltpu.SemaphoreType.DMA((2,))],
        compiler_params=_cparams("arbitrary"),
        name="attn_sample",
    )(q_s, k_s, v_s, cache_k2, cache_v2, cache_k2, cache_v2)

    route1, cols1 = _router_operands(1, ffn_norm_g, router_g_w, router_g_b, router_e_w, router_e_b)
    h3, xs2, bucket2 = pl.pallas_call(
        functools.partial(_oproj_kernel, nps=nps),
        out_shape=route_out_shapes,
        grid=(nps + nss,),
        in_specs=[pl.BlockSpec((tb, D_MODEL), lambda i: (pstep(i), 0)),
                  pl.BlockSpec((tb, D_MODEL), lambda i: (sstep(i), 0)),
                  tok_spec, _const_spec((D_MODEL, D_MODEL))] + route_in_specs,
        out_specs=route_out_specs,
        compiler_params=_cparams("arbitrary"),
        name="o_proj",
    )(o_p, o_s, h2, b_w_o[0].astype(BF16), *route1)
    y2, pos2 = _moe(xs2, bucket2, 1, cols1, moe_w_gate, moe_w_up, moe_w_down, True)

    y_p, y_s = pl.pallas_call(
        functools.partial(_final_kernel, npt=npt),
        out_shape=(f32_rows(tp), f32_rows(ts)),
        grid_spec=pltpu.PrefetchScalarGridSpec(
            num_scalar_prefetch=1, grid=(npt + nst,),
            in_specs=[row_spec, any_spec],
            out_specs=[prow_spec, srow_spec],
            scratch_shapes=gather_scratch),
        compiler_params=_cparams("arbitrary"),
        name="final_residual",
    )(pos2, h3, y2)

    kv_shape_p = (bsz, seq, N_HEADS, HEAD_DIM)
    kv_shape_s = (dbsz, dseq, N_HEADS, HEAD_DIM)
    return (y_p.reshape(bsz, seq, D_MODEL), y_s.reshape(dbsz, dseq, D_MODEL),
            k_p.reshape(kv_shape_p), v_p.reshape(kv_shape_p), k_s.reshape(kv_shape_s), v_s.reshape(kv_shape_s),
            v_rows.reshape(1, dbsz, dseq, A_WIDTH))
```

```python
import functools

import jax
import jax.numpy as jnp
from jax import lax
from jax.experimental import pallas as pl
from jax.experimental.pallas import tpu as pltpu

F32 = jnp.float32
BF16 = jnp.bfloat16
I32 = jnp.int32

D_MODEL = 1024
CHUNK = 64
GMLP_BLOCK = 128
A_WIDTH = 2 * D_MODEL
A_HEADS = 8
A_HEAD_DIM = A_WIDTH // A_HEADS
N_HEADS = 8
HEAD_DIM = D_MODEL // N_HEADS
N_GROUPS = 4
EXPERTS_PER_GROUP = 4
EXPERT_DIM = D_MODEL // 2
EPS = 1e-6

LANES = 128
SUBLANES = 8
MXU_WIDTH = 256
SLABS = D_MODEL // LANES
assert SLABS == SUBLANES and N_HEADS == SLABS and HEAD_DIM == LANES
PAIRS = ((0, 1), (0, 2), (0, 3), (1, 2), (1, 3), (2, 3))
N_BUCKETS = N_GROUPS * len(PAIRS)
ROUTER_ROWS = 32
EXPERT_ROW0 = 8

TOKEN_BLOCK = 2 * GMLP_BLOCK
EXPERT_TILE = 256
KVQ_TILE = 256
ATTN_TILE = 256
VMEM_LIMIT = 56 * 1024 * 1024
STOP_MASS = 105.0


def _cparams(*sem):
    return pltpu.CompilerParams(dimension_semantics=sem, vmem_limit_bytes=VMEM_LIMIT)


def _rms(x, g):
    ms = jnp.mean(x * x, axis=-1, keepdims=True)
    return x * lax.rsqrt(ms + EPS) * g


def _load_rows(ref, n):
    return jnp.concatenate([ref[pl.ds(c, n, stride=SLABS), :] for c in range(SLABS)], axis=1)


def _store_rows(ref, x):
    n = x.shape[0]
    for c in range(SLABS):
        ref[pl.ds(c, n, stride=SLABS), :] = x[:, c * LANES:(c + 1) * LANES]


def _row_copy(src_hbm, src_row, buf, slot, k, sem):
    src = src_row * SLABS if isinstance(src_row, int) else pl.multiple_of(src_row * SLABS, SLABS)
    return pltpu.make_async_copy(src_hbm.at[pl.ds(src, SLABS)], buf.at[slot, pl.ds(k * SLABS, SLABS)], sem.at[slot])


def _gather_step(idx_ref, src_hbm, buf, sem, rows, n_steps=None):
    i = pl.program_id(0)
    slot = i % 2
    if n_steps is None:
        n_steps = pl.num_programs(0)

    @pl.when(i == 0)
    def _():
        for k in range(rows):
            _row_copy(src_hbm, idx_ref[k], buf, 0, k, sem).start(priority=k % 2)

    @pl.when(i + 1 < n_steps)
    def _():
        for k in range(rows):
            _row_copy(src_hbm, idx_ref[(i + 1) * rows + k], buf, 1 - slot, k, sem).start(priority=k % 2)

    @pl.when(i < n_steps)
    def _():
        for k in range(rows):
            _row_copy(src_hbm, 0, buf, slot, k, sem).wait()

    return slot


def _first_argmax(vals):
    m = vals[0]
    for v in vals[1:]:
        m = jnp.maximum(m, v)
    idx = jnp.full(m.shape, len(vals) - 1, I32)
    for r in range(len(vals) - 2, -1, -1):
        idx = jnp.where(vals[r] == m, r, idx)
    return m, idx


def _route(xn, wrt_ref, rb_ref, xs_ref, bucket_ref):
    lt = lax.dot_general(wrt_ref[...], xn, (((1,), (1,)), ((), ())),
                         precision=lax.Precision.HIGHEST, preferred_element_type=F32) + rb_ref[:, 0:1]
    _, g_idx = _first_argmax([lt[r:r + 1, :] for r in range(N_GROUPS)])
    le = []
    for e in range(EXPERTS_PER_GROUP):
        row = EXPERT_ROW0 + EXPERTS_PER_GROUP * (N_GROUPS - 1) + e
        v = lt[row:row + 1, :]
        for g in range(N_GROUPS - 2, -1, -1):
            row = EXPERT_ROW0 + EXPERTS_PER_GROUP * g + e
            v = jnp.where(g_idx == g, lt[row:row + 1, :], v)
        le.append(v)
    _, i1 = _first_argmax(le)
    _, i2 = _first_argmax([jnp.where(i1 == e, -jnp.inf, le[e]) for e in range(EXPERTS_PER_GROUP)])
    lo = jnp.minimum(i1, i2)
    hi = jnp.maximum(i1, i2)
    pair = jnp.where(lo == 0, hi - 1, jnp.where(lo == 1, hi + 1, 5))
    bucket = g_idx * len(PAIRS) + pair
    for sb in range(xn.shape[0] // LANES):
        bucket_ref[sb] = bucket[:, sb * LANES:(sb + 1) * LANES]
    _store_rows(xs_ref, xn)


def _a_layer_kernel(xp_ref, xsm_ref, ag_ref, win_ref, vg_ref, ws_ref, bs_ref, wout_ref, fg_ref, wrt_ref, rb_ref,
                    h_ref, xs_ref, bucket_ref, vrow_ref, *, nps):
    i = pl.program_id(0)
    x = jnp.where(i < nps, xp_ref[...], xsm_ref[...])
    xn = _rms(x, ag_ref[...]).astype(BF16)
    n_col = 4
    cw = 2 * A_WIDTH // n_col
    z = [jax.nn.gelu(jnp.dot(xn, win_ref[:, c * cw:(c + 1) * cw], preferred_element_type=F32)) for c in range(n_col)]
    u = jnp.concatenate(z[:n_col // 2], axis=1)
    v = _rms(jnp.concatenate(z[n_col // 2:], axis=1), vg_ref[...])

    @pl.when(i >= nps)
    def _():
        vrow_ref[...] = v

    vb = v.astype(BF16)
    gate = jnp.concatenate([
        jnp.concatenate(
            [jnp.dot(ws_ref[0, h], vb[sb * GMLP_BLOCK:(sb + 1) * GMLP_BLOCK, h * A_HEAD_DIM:(h + 1) * A_HEAD_DIM],
                     preferred_element_type=F32) for h in range(A_HEADS)], axis=1) + bs_ref[0]
        for sb in range(x.shape[0] // GMLP_BLOCK)], axis=0)
    s = (u * gate).astype(BF16)
    h1 = x + jnp.dot(s, wout_ref[...], preferred_element_type=F32)
    h_ref[...] = h1
    _route(_rms(h1, fg_ref[...]), wrt_ref, rb_ref, xs_ref, bucket_ref)


def _oproj_kernel(op_ref, osm_ref, h_in_ref, wo_ref, fg_ref, wrt_ref, rb_ref, h_ref, xs_ref, bucket_ref, *, nps):
    i = pl.program_id(0)
    o = jnp.where(i < nps, op_ref[...], osm_ref[...])
    h3 = h_in_ref[...] + jnp.dot(o, wo_ref[...], preferred_element_type=F32)
    h_ref[...] = h3
    _route(_rms(h3, fg_ref[...]), wrt_ref, rb_ref, xs_ref, bucket_ref)


def _rank_kernel(b_ref, pos_ref, tb_ref, lo_ref, hi_ref, *, tm, nt):
    bk = b_ref[...]
    nr = bk.shape[0]
    r_i = lax.broadcasted_iota(I32, (LANES, LANES), 0)
    c_i = lax.broadcasted_iota(I32, (LANES, LANES), 1)
    upper = (r_i <= c_i).astype(BF16)
    rr = lax.broadcasted_iota(I32, (nr, nr), 0)
    cc = lax.broadcasted_iota(I32, (nr, nr), 1)
    before_rows = (cc < rr).astype(BF16)
    lane = lax.broadcasted_iota(I32, (1, LANES), 1)
    tile_start = lane.astype(F32) * tm
    pos = jnp.zeros((nr, LANES), F32)
    seg_start = jnp.zeros((1, LANES), F32)
    tile_bucket = jnp.zeros((1, LANES), I32)
    pad_lo = jnp.zeros((1, LANES), F32)
    pad_hi = jnp.zeros((1, LANES), F32)
    for b in range(N_BUCKETS):
        m = bk == b
        pref = jnp.dot(jnp.where(m, 1.0, 0.0).astype(BF16), upper, preferred_element_type=F32)
        rowtot = jnp.broadcast_to(pref[:, LANES - 1:LANES], (nr, LANES))
        before = jnp.dot(before_rows, rowtot.astype(BF16), preferred_element_type=F32)
        cnt = jnp.sum(rowtot, axis=0, keepdims=True)
        pos = pos + jnp.where(m, seg_start + before + pref - 1.0, 0.0)
        pad_lo = jnp.where(lane == b, seg_start + cnt, pad_lo)
        seg_start = seg_start + jnp.ceil(cnt / tm) * tm
        pad_hi = jnp.where(lane == b, seg_start, pad_hi)
        tile_bucket = tile_bucket + (seg_start <= tile_start).astype(I32)
    pad_lo = jnp.where(lane == N_BUCKETS, seg_start, pad_lo)
    pad_hi = jnp.where(lane == N_BUCKETS, float(nt * tm), pad_hi)
    pos_ref[...] = pos.astype(I32)
    tb_ref[...] = tile_bucket
    lo_ref[...] = pad_lo.astype(I32)
    hi_ref[...] = pad_hi.astype(I32)


def _inverse_kernel(pos_ref, lo_ref, hi_ref, inv_ref, *, n_tok):
    batch = 8
    assert n_tok % batch == 0

    def clear(r, carry):
        inv_ref[r] = lax.rem(r, n_tok)
        return carry

    def put(g, carry):
        base = g * batch
        dst = [pos_ref[base + j] for j in range(batch)]
        for j in range(batch):
            inv_ref[dst[j]] = base + j
        return carry

    for b in range(N_BUCKETS + 1):
        lax.fori_loop(lo_ref[b], hi_ref[b], clear, 0)
    lax.fori_loop(0, n_tok // batch, put, 0)


def _expert_kernel(telo_ref, tehi_ref, nused_ref, inv_ref, xs_hbm, wr_ref, rb_ref, wgl_ref, wul_ref, wdl_ref,
                   wgh_ref, wuh_ref, wdh_ref, y_ref, xbuf, wup, wdown, gsem, *, tm):
    r = pl.program_id(0)
    n_used = nused_ref[0]
    slot = _gather_step(inv_ref, xs_hbm, xbuf, gsem, tm, n_steps=n_used)

    @pl.when(r < n_used)
    def _():
        prev = jnp.maximum(r - 1, 0)

        @pl.when(jnp.logical_or(r == 0, telo_ref[r] != telo_ref[prev]))
        def _():
            wup[0] = wgl_ref[0, 0].astype(BF16)
            wup[1] = wul_ref[0, 0].astype(BF16)
            wdown[0] = wdl_ref[0, 0].astype(BF16)

        @pl.when(jnp.logical_or(r == 0, tehi_ref[r] != tehi_ref[prev]))
        def _():
            wup[2] = wgh_ref[0, 0].astype(BF16)
            wup[3] = wuh_ref[0, 0].astype(BF16)
            wdown[1] = wdh_ref[0, 0].astype(BF16)

        x = _load_rows(xbuf.at[slot], tm).astype(BF16)
        logits = jnp.dot(x, wr_ref[...], preferred_element_type=F32) + rb_ref[...]
        lane = lax.broadcasted_iota(I32, logits.shape, 1)
        grp = telo_ref[r] // EXPERTS_PER_GROUP
        is_group = lane < N_GROUPS
        m = jnp.max(jnp.where(is_group, logits, -jnp.inf), axis=1, keepdims=True)
        ex = jnp.exp(logits - m)
        p_g = (jnp.sum(jnp.where(lane == grp, ex, 0.0), axis=1, keepdims=True)
               / jnp.sum(jnp.where(is_group, ex, 0.0), axis=1, keepdims=True))
        l_lo = jnp.sum(jnp.where(lane == EXPERT_ROW0 + telo_ref[r], logits, 0.0), axis=1, keepdims=True)
        l_hi = jnp.sum(jnp.where(lane == EXPERT_ROW0 + tehi_ref[r], logits, 0.0), axis=1, keepdims=True)
        mm = jnp.maximum(l_lo, l_hi)
        e_lo = jnp.exp(l_lo - mm)
        e_hi = jnp.exp(l_hi - mm)
        gates = (p_g * (e_lo / (e_lo + e_hi)), p_g * (e_hi / (e_lo + e_hi)))

        y = None
        for which in range(2):
            a = jnp.dot(x, wup[2 * which], preferred_element_type=F32)
            b = jnp.dot(x, wup[2 * which + 1], preferred_element_type=F32)
            hid = (jax.nn.silu(a) * b * gates[which]).astype(BF16)
            part = jnp.dot(hid, wdown[which], preferred_element_type=F32)
            y = part if y is None else y + part
        _store_rows(y_ref, y)

    @pl.when(r >= n_used)
    def _():
        y_ref[...] = jnp.zeros_like(y_ref)


def _head_rms(x, g):
    return jnp.concatenate([_rms(x[:, h * HEAD_DIM:(h + 1) * HEAD_DIM], g) for h in range(N_HEADS)], axis=1)


def _kvq_kernel(pos_ref, h_in_ref, y_hbm, kvg_ref, wkv_ref, kng_ref, bg_ref, wq_ref, qng_ref,
                h_ref, kp_ref, vp_ref, ksm_ref, vsm_ref, kb_ref, vb_ref, qb_ref, qsm_ref, ybuf, sem, *, npt):
    i = pl.program_id(0)
    tk = h_in_ref.shape[0]
    h2 = h_in_ref[...] + _load_rows(ybuf.at[_gather_step(pos_ref, y_hbm, ybuf, sem, tk)], tk)
    h_ref[...] = h2
    kv = jnp.dot(_rms(h2, kvg_ref[...]).astype(BF16), wkv_ref[...], preferred_element_type=F32)
    q = jnp.dot(_rms(h2, bg_ref[...]).astype(BF16), wq_ref[...], preferred_element_type=F32)
    k = _head_rms(kv[:, :D_MODEL], kng_ref[...])
    v = kv[:, D_MODEL:]
    q = _head_rms(q, qng_ref[...]) * (HEAD_DIM ** -0.5)

    @pl.when(i < npt)
    def _():
        _store_rows(kp_ref, k)
        _store_rows(vp_ref, v)
        for h in range(N_HEADS):
            sl = slice(h * HEAD_DIM, (h + 1) * HEAD_DIM)
            kb_ref[0, h] = k[:, sl].astype(BF16)
            vb_ref[0, h] = v[:, sl].astype(BF16)
            qb_ref[0, h] = q[:, sl].astype(BF16)

    @pl.when(i >= npt)
    def _():
        _store_rows(ksm_ref, k)
        _store_rows(vsm_ref, v)
        qsm_ref[...] = q


def _stick_blocks(qs, ks, vs, rs, visible, later_keys):
    n = len(qs)
    zs = [lax.dot_general(qs[h], ks[h], (((1,), (1,)), ((), ())), preferred_element_type=F32) for h in range(n)]
    sps = [jnp.maximum(z, 0.0) + jnp.log(1.0 + jnp.exp(-jnp.abs(z))) for z in zs]
    if visible is not None:
        sps = [jnp.where(visible, sp, 0.0) for sp in sps]
    his = [sp.astype(BF16) for sp in sps]
    los = [(sp - hi.astype(F32)).astype(BF16) for sp, hi in zip(sps, his)]
    cs = [jnp.dot(hi, later_keys, preferred_element_type=F32) + jnp.dot(lo, later_keys, preferred_element_type=F32)
          for hi, lo in zip(his, los)]
    ws = [jnp.exp(zs[h] - cs[h] - rs[h]) for h in range(n)]
    if visible is not None:
        ws = [jnp.where(visible, w, 0.0) for w in ws]
    outs = [jnp.dot(ws[h].astype(BF16), vs[h], preferred_element_type=F32) for h in range(n)]
    return outs, [rs[h] + cs[h][:, 0:1] for h in range(n)]


def _later_keys(n):
    rowi = lax.broadcasted_iota(I32, (n, n), 0)
    coli = lax.broadcasted_iota(I32, (n, n), 1)
    return (rowi >= coli).astype(BF16)


def _min_over(rs):
    m = rs[0]
    for r in rs[1:]:
        m = jnp.minimum(m, r)
    return jnp.min(m)


def _attn_kernel(q_ref, k_ref, v_ref, o_ref, r_sc, acc_sc):
    i = pl.program_id(1)
    tq = q_ref.shape[2]
    visible = lax.broadcasted_iota(I32, (tq, tq), 1) < lax.broadcasted_iota(I32, (tq, tq), 0)
    later_keys = _later_keys(tq)
    heads = range(N_HEADS)

    def block(j, first):
        start = pl.multiple_of(j * tq, tq)
        qs = [q_ref[0, h] for h in heads]
        ks = [k_ref[0, h, pl.ds(start, tq), :] for h in heads]
        vs = [v_ref[0, h, pl.ds(start, tq), :] for h in heads]
        rs = [jnp.zeros((tq, 1), F32) if first else r_sc[h] for h in heads]
        outs, rs = _stick_blocks(qs, ks, vs, rs, visible if first else None, later_keys)
        acc = jnp.concatenate(outs, axis=1)
        if first:
            acc_sc[...] = acc
        else:
            acc_sc[...] += acc
        for h in heads:
            r_sc[h] = rs[h]
        return _min_over(rs)

    rmin = block(i, True)
    lax.while_loop(lambda st: jnp.logical_and(st[0] >= 0, st[1] <= STOP_MASS),
                   lambda st: (st[0] - 1, block(st[0], False)), (i - 1, rmin))
    o_ref[...] = acc_sc[...].astype(o_ref.dtype)


def _sample_attn_kernel(q_ref, kn_ref, vn_ref, ck_ref, cv_ref, ck_hbm, cv_hbm, o_ref, r_sc, acc_sc, kbuf, vbuf, sem,
                        *, nkc):
    b = pl.program_id(0)
    ds = q_ref.shape[0]
    ck = ck_ref.shape[0] // N_HEADS
    heads = range(N_HEADS)

    def head_rows(ref, h, n):
        return ref[pl.ds(h, n, stride=N_HEADS), :].astype(BF16)

    qs = [q_ref[:, h * HEAD_DIM:(h + 1) * HEAD_DIM].astype(BF16) for h in heads]
    visible = lax.broadcasted_iota(I32, (ds, LANES), 1) < lax.broadcasted_iota(I32, (ds, LANES), 0)
    pad = jnp.zeros((LANES - ds, HEAD_DIM), BF16)
    outs, rs = _stick_blocks(qs, [jnp.concatenate([head_rows(kn_ref, h, ds), pad], axis=0) for h in heads],
                             [jnp.concatenate([head_rows(vn_ref, h, ds), pad], axis=0) for h in heads],
                             [jnp.zeros((ds, 1), F32)] * N_HEADS, visible, _later_keys(LANES))
    acc_sc[...] = jnp.concatenate(outs, axis=1)
    for h in heads:
        r_sc[h] = rs[h]

    def chunk(kref, vref):
        outs, rs = _stick_blocks(qs, [head_rows(kref, h, ck) for h in heads], [head_rows(vref, h, ck) for h in heads],
                                 [r_sc[h] for h in heads], None, _later_keys(ck))
        acc_sc[...] += jnp.concatenate(outs, axis=1)
        for h in heads:
            r_sc[h] = rs[h]
        return _min_over(rs)

    def older(st):
        c = st[0]
        rows = ck * N_HEADS
        start = pl.multiple_of((b * nkc + c) * rows, rows)
        cpk = pltpu.make_async_copy(ck_hbm.at[pl.ds(start, rows)], kbuf, sem.at[0])
        cpv = pltpu.make_async_copy(cv_hbm.at[pl.ds(start, rows)], vbuf, sem.at[1])
        cpk.start()
        cpv.start()
        cpk.wait()
        cpv.wait()
        return c - 1, chunk(kbuf, vbuf)

    lax.while_loop(lambda st: jnp.logical_and(st[0] >= 0, st[1] <= STOP_MASS), older,
                   (nkc - 2, chunk(ck_ref, cv_ref)))
    o_ref[...] = acc_sc[...].astype(o_ref.dtype)


def _final_kernel(pos_ref, h_ref, y_hbm, yp_ref, ysm_ref, ybuf, sem, *, npt):
    i = pl.program_id(0)
    tk = h_ref.shape[0]
    y = h_ref[...] + _load_rows(ybuf.at[_gather_step(pos_ref, y_hbm, ybuf, sem, tk)], tk)

    @pl.when(i < npt)
    def _():
        yp_ref[...] = y

    @pl.when(i >= npt)
    def _():
        ysm_ref[...] = y


def _const_spec(shape):
    return pl.BlockSpec(shape, lambda *_: (0,) * len(shape))


def _router_operands(layer, ffn_norm_g, router_g_w, router_g_b, router_e_w, router_e_b):
    n_e = N_GROUPS * EXPERTS_PER_GROUP
    wrt = jnp.zeros((ROUTER_ROWS, D_MODEL), F32)
    wrt = wrt.at[:N_GROUPS].set(router_g_w[layer].T)
    wrt = wrt.at[EXPERT_ROW0:EXPERT_ROW0 + n_e].set(router_e_w[layer].transpose(0, 2, 1).reshape(n_e, D_MODEL))
    rb = jnp.zeros((ROUTER_ROWS,), F32)
    rb = rb.at[:N_GROUPS].set(router_g_b[layer])
    rb = rb.at[EXPERT_ROW0:EXPERT_ROW0 + n_e].set(router_e_b[layer].reshape(-1))
    wr_cols = jnp.zeros((D_MODEL, LANES), F32).at[:, :ROUTER_ROWS].set(wrt.T).astype(BF16)
    rb_cols = jnp.zeros((1, LANES), F32).at[0, :ROUTER_ROWS].set(rb)
    return (ffn_norm_g[layer][None, :], wrt, jnp.broadcast_to(rb[:, None], (ROUTER_ROWS, LANES))), (wr_cols, rb_cols)


def _moe(xs_rt, bucket, layer, route_cols, w_gate, w_up, w_down):
    nblk = bucket.shape[0]
    n_tok = nblk * LANES
    tm = EXPERT_TILE
    nr = -(-nblk // LANES) * LANES
    nt = -(-(n_tok + N_BUCKETS * (tm - 1)) // tm)
    assert nt <= LANES
    bk = jnp.pad(bucket.reshape(nblk, LANES), ((0, nr - nblk), (0, 0)), constant_values=N_BUCKETS)
    lane_row = jax.ShapeDtypeStruct((1, LANES), I32)
    pos, tile_bucket, pad_lo, pad_hi = pl.pallas_call(
        functools.partial(_rank_kernel, tm=tm, nt=nt),
        out_shape=(jax.ShapeDtypeStruct((nr, LANES), I32), lane_row, lane_row, lane_row),
        name=f"moe_rank_{layer}",
    )(bk)
    pos = pos[:nblk].reshape(n_tok)
    smem = pl.BlockSpec(memory_space=pltpu.SMEM)
    inv = pl.pallas_call(
        functools.partial(_inverse_kernel, n_tok=n_tok),
        out_shape=jax.ShapeDtypeStruct((nt * tm,), I32),
        in_specs=[smem, smem, smem],
        out_specs=smem,
        name=f"moe_inverse_{layer}",
    )(pos, pad_lo[0], pad_hi[0])
    tb = tile_bucket[0, :nt]
    n_used = jnp.sum((tb < N_BUCKETS).astype(I32))
    tsrc = jnp.minimum(jnp.arange(nt, dtype=I32), n_used - 1)
    tbc = jnp.minimum(tb[tsrc], N_BUCKETS - 1)
    pair_lo = jnp.array([p[0] for p in PAIRS], I32)
    pair_hi = jnp.array([p[1] for p in PAIRS], I32)
    grp = tbc // len(PAIRS)
    telo = grp * EXPERTS_PER_GROUP + pair_lo[tbc % len(PAIRS)]
    tehi = grp * EXPERTS_PER_GROUP + pair_hi[tbc % len(PAIRS)]

    def wspec(shape, which):
        return pl.BlockSpec((1, 1) + shape, lambda r, lo, hi, *_: (layer, (lo, hi)[which][r], 0, 0))

    up_shape = (D_MODEL, EXPERT_DIM)
    down_shape = (EXPERT_DIM, D_MODEL)
    wr_cols, rb_cols = route_cols
    tile_shape = (tm * SLABS, LANES)
    y_sorted = pl.pallas_call(
        functools.partial(_expert_kernel, tm=tm),
        out_shape=jax.ShapeDtypeStruct((nt * tm * SLABS, LANES), F32),
        grid_spec=pltpu.PrefetchScalarGridSpec(
            num_scalar_prefetch=4, grid=(nt,),
            in_specs=[pl.BlockSpec(memory_space=pl.ANY),
                      pl.BlockSpec((D_MODEL, LANES), lambda r, *_: (0, 0)),
                      pl.BlockSpec((1, LANES), lambda r, *_: (0, 0)),
                      wspec(up_shape, 0), wspec(up_shape, 0), wspec(down_shape, 0),
                      wspec(up_shape, 1), wspec(up_shape, 1), wspec(down_shape, 1)],
            out_specs=pl.BlockSpec(tile_shape, lambda r, *_: (r, 0)),
            scratch_shapes=[pltpu.VMEM((2,) + tile_shape, F32), pltpu.VMEM((4,) + up_shape, BF16),
                            pltpu.VMEM((2,) + down_shape, BF16), pltpu.SemaphoreType.DMA((2,))]),
        compiler_params=_cparams("arbitrary"),
        name=f"moe_experts_{layer}",
    )(telo, tehi, n_used.reshape(1), inv, xs_rt, wr_cols, rb_cols, w_gate, w_up, w_down, w_gate, w_up, w_down)
    return y_sorted, pos


def kernel(x_prompt, x_sample, cache_k, cache_v, a_norm_g, a_w_in, a_v_norm_g, a_w_s, a_b_s, a_w_out, kv_norm_g, w_kv, k_norm_g, b_norm_g, b_w_q, q_norm_g, b_w_o, ffn_norm_g, router_g_w, router_g_b, router_e_w, router_e_b, moe_w_gate, moe_w_up, moe_w_down):
    bsz, seq, _ = x_prompt.shape
    dbsz, dseq, _ = x_sample.shape
    past = cache_k.shape[1]
    tp, ts = bsz * seq, dbsz * dseq
    n_tok = tp + ts
    tb = TOKEN_BLOCK
    gb = GMLP_BLOCK
    assert seq % ATTN_TILE == 0 and tp % KVQ_TILE == 0 and ts % KVQ_TILE == 0 and tp % tb == 0 and ts % tb == 0
    assert gb % dseq == 0 and dseq <= CHUNK and past % ATTN_TILE == 0
    nps, nss = tp // tb, ts // tb
    xp = x_prompt.reshape(tp, D_MODEL)
    xsm = x_sample.reshape(ts, D_MODEL)

    def pstep(i):
        return jnp.minimum(i, nps - 1)

    def sstep(i):
        return jnp.maximum(i - nps, 0)

    def rt_shape(n):
        return jax.ShapeDtypeStruct((n * SLABS, LANES), F32)

    def rt_spec(rows, index):
        return pl.BlockSpec((rows * SLABS, LANES), lambda i, *_: (index(i), 0))

    tok_spec = pl.BlockSpec((tb, D_MODEL), lambda i: (i, 0))
    route_in_specs = [_const_spec((1, D_MODEL)), _const_spec((ROUTER_ROWS, D_MODEL)),
                      _const_spec((ROUTER_ROWS, LANES))]
    route_out_shapes = (jax.ShapeDtypeStruct((n_tok, D_MODEL), F32), rt_shape(n_tok),
                        jax.ShapeDtypeStruct((n_tok // LANES, 1, LANES), I32))
    route_out_specs = [tok_spec, rt_spec(tb, lambda i: i), pl.BlockSpec((tb // LANES, 1, LANES), lambda i: (i, 0, 0))]

    pos_i = jnp.arange(gb)
    mask = (pos_i[None, :] // CHUNK) <= (pos_i[:, None] // CHUNK)
    w_prompt = jnp.where(mask, a_w_s[0], 0.0)
    rep = gb // dseq
    w_sample = jnp.einsum("ij,hts->hitjs", jnp.eye(rep, dtype=F32), w_prompt[:, :dseq, :dseq]).reshape(
        A_HEADS, gb, gb)
    ws_all = jnp.stack([w_prompt, w_sample]).astype(BF16)
    b_prompt = a_b_s[0]
    b_sample = jnp.tile(a_b_s[0][:, :dseq], (1, rep))
    bs_all = jnp.stack([jnp.repeat(b.T, A_HEAD_DIM, axis=1) for b in (b_prompt, b_sample)])
    route0, cols0 = _router_operands(0, ffn_norm_g, router_g_w, router_g_b, router_e_w, router_e_b)
    h1, xs1, bucket1, v_rows = pl.pallas_call(
        functools.partial(_a_layer_kernel, nps=nps),
        out_shape=route_out_shapes + (jax.ShapeDtypeStruct((ts, A_WIDTH), F32),),
        grid=(nps + nss,),
        in_specs=[pl.BlockSpec((tb, D_MODEL), lambda i: (pstep(i), 0)),
                  pl.BlockSpec((tb, D_MODEL), lambda i: (sstep(i), 0)),
                  _const_spec((1, D_MODEL)), _const_spec((D_MODEL, 2 * A_WIDTH)), _const_spec((1, A_WIDTH)),
                  pl.BlockSpec((1, A_HEADS, gb, gb), lambda i: (i // nps, 0, 0, 0)),
                  pl.BlockSpec((1, gb, A_WIDTH), lambda i: (i // nps, 0, 0)),
                  _const_spec((A_WIDTH, D_MODEL))] + route_in_specs,
        out_specs=route_out_specs + [pl.BlockSpec((tb, A_WIDTH), lambda i: (sstep(i), 0))],
        compiler_params=_cparams("arbitrary"),
        name="a_layer",
    )(xp, xsm, a_norm_g[0][None, :], a_w_in[0].astype(BF16), a_v_norm_g[0][None, :], ws_all, bs_all,
      a_w_out[0].astype(BF16), *route0)
    y1, pos1 = _moe(xs1, bucket1, 0, cols0, moe_w_gate, moe_w_up, moe_w_down)

    tk = KVQ_TILE
    npt, nst = tp // tk, ts // tk
    spt = seq // tk

    def ptile(i):
        return jnp.minimum(i, npt - 1)

    def stile(i):
        return jnp.maximum(i - npt, 0)

    row_spec = pl.BlockSpec((tk, D_MODEL), lambda i, *_: (i, 0))
    prow_spec = pl.BlockSpec((tk, D_MODEL), lambda i, *_: (ptile(i), 0))
    srow_spec = pl.BlockSpec((tk, D_MODEL), lambda i, *_: (stile(i), 0))
    head_spec = pl.BlockSpec((1, N_HEADS, tk, HEAD_DIM), lambda i, *_: (ptile(i) // spt, 0, ptile(i) % spt, 0))
    any_spec = pl.BlockSpec(memory_space=pl.ANY)
    gather_scratch = [pltpu.VMEM((2, tk * SLABS, LANES), F32), pltpu.SemaphoreType.DMA((2,))]
    f32_rows = lambda n: jax.ShapeDtypeStruct((n, D_MODEL), F32)
    head_major = jax.ShapeDtypeStruct((bsz, N_HEADS, seq, HEAD_DIM), BF16)
    h2, k_p, v_p, k_s, v_s, kb, vb, qb, q_s = pl.pallas_call(
        functools.partial(_kvq_kernel, npt=npt),
        out_shape=(f32_rows(n_tok), rt_shape(tp), rt_shape(tp), rt_shape(ts), rt_shape(ts),
                   head_major, head_major, head_major, f32_rows(ts)),
        grid_spec=pltpu.PrefetchScalarGridSpec(
            num_scalar_prefetch=1, grid=(npt + nst,),
            in_specs=[row_spec, any_spec, _const_spec((1, D_MODEL)),
                      _const_spec((D_MODEL, 2 * D_MODEL)), _const_spec((1, HEAD_DIM)), _const_spec((1, D_MODEL)),
                      _const_spec((D_MODEL, D_MODEL)), _const_spec((1, HEAD_DIM))],
            out_specs=[row_spec, rt_spec(tk, ptile), rt_spec(tk, ptile), rt_spec(tk, stile), rt_spec(tk, stile),
                       head_spec, head_spec, head_spec, srow_spec],
            scratch_shapes=gather_scratch),
        compiler_params=_cparams("arbitrary"),
        name="kvq",
    )(pos1, h1, y1, kv_norm_g[None, :], w_kv.astype(BF16), k_norm_g[None, :], b_norm_g[0][None, :],
      b_w_q[0].astype(BF16), q_norm_g[0][None, :])

    tq = ATTN_TILE
    nq = seq // tq
    qspec = pl.BlockSpec((1, N_HEADS, tq, HEAD_DIM), lambda b, i: (b, 0, i, 0))
    kvspec = pl.BlockSpec((1, N_HEADS, seq, HEAD_DIM), lambda b, i: (b, 0, 0, 0), pipeline_mode=pl.Buffered(1))
    o_p = pl.pallas_call(
        _attn_kernel,
        out_shape=jax.ShapeDtypeStruct((tp, D_MODEL), BF16),
        grid=(bsz, nq),
        in_specs=[qspec, kvspec, kvspec],
        out_specs=pl.BlockSpec((tq, D_MODEL), lambda b, i: (b * nq + i, 0)),
        scratch_shapes=[pltpu.VMEM((N_HEADS, tq, 1), F32), pltpu.VMEM((tq, D_MODEL), F32)],
        compiler_params=_cparams("arbitrary", "arbitrary"),
        name="attn_prompt",
    )(qb, kb, vb)

    ck = ATTN_TILE
    nkc = past // ck
    new_spec = pl.BlockSpec((dseq, D_MODEL), lambda b: (b, 0))
    new_rt_spec = rt_spec(dseq, lambda b: b)
    newest_spec = rt_spec(ck, lambda b: b * nkc + nkc - 1)
    cache_k2 = cache_k.reshape(dbsz * past * N_HEADS, HEAD_DIM)
    cache_v2 = cache_v.reshape(dbsz * past * N_HEADS, HEAD_DIM)
    chunk_buf = pltpu.VMEM((ck * N_HEADS, HEAD_DIM), F32)
    o_s = pl.pallas_call(
        functools.partial(_sample_attn_kernel, nkc=nkc),
        out_shape=jax.ShapeDtypeStruct((ts, D_MODEL), BF16),
        grid=(dbsz,),
        in_specs=[new_spec, new_rt_spec, new_rt_spec, newest_spec, newest_spec, any_spec, any_spec],
        out_specs=new_spec,
        scratch_shapes=[pltpu.VMEM((N_HEADS, dseq, 1), F32), pltpu.VMEM((dseq, D_MODEL), F32), chunk_buf, chunk_buf,
                        pltpu.SemaphoreType.DMA((2,))],
        compiler_params=_cparams("arbitrary"),
        name="attn_sample",
    )(q_s, k_s, v_s, cache_k2, cache_v2, cache_k2, cache_v2)

    route1, cols1 = _router_operands(1, ffn_norm_g, router_g_w, router_g_b, router_e_w, router_e_b)
    h3, xs2, bucket2 = pl.pallas_call(
        functools.partial(_oproj_kernel, nps=nps),
        out_shape=route_out_shapes,
        grid=(nps + nss,),
        in_specs=[pl.BlockSpec((tb, D_MODEL), lambda i: (pstep(i), 0)),
                  pl.BlockSpec((tb, D_MODEL), lambda i: (sstep(i), 0)),
                  tok_spec, _const_spec((D_MODEL, D_MODEL))] + route_in_specs,
        out_specs=route_out_specs,
        compiler_params=_cparams("arbitrary"),
        name="o_proj",
    )(o_p, o_s, h2, b_w_o[0].astype(BF16), *route1)
    y2, pos2 = _moe(xs2, bucket2, 1, cols1, moe_w_gate, moe_w_up, moe_w_down)

    y_p, y_s = pl.pallas_call(
        functools.partial(_final_kernel, npt=npt),
        out_shape=(f32_rows(tp), f32_rows(ts)),
        grid_spec=pltpu.PrefetchScalarGridSpec(
            num_scalar_prefetch=1, grid=(npt + nst,),
            in_specs=[row_spec, any_spec],
            out_specs=[prow_spec, srow_spec],
            scratch_shapes=gather_scratch),
        compiler_params=_cparams("arbitrary"),
        name="final_residual",
    )(pos2, h3, y2)

    kv_shape_p = (bsz, seq, N_HEADS, HEAD_DIM)
    kv_shape_s = (dbsz, dseq, N_HEADS, HEAD_DIM)
    return (y_p.reshape(bsz, seq, D_MODEL), y_s.reshape(dbsz, dseq, D_MODEL),
            k_p.reshape(kv_shape_p), v_p.reshape(kv_shape_p), k_s.reshape(kv_shape_s), v_s.reshape(kv_shape_s),
            v_rows.reshape(1, dbsz, dseq, A_WIDTH))
```

```python
import functools

import jax
import jax.numpy as jnp
from jax import lax
from jax.experimental import pallas as pl
from jax.experimental.pallas import tpu as pltpu

F32 = jnp.float32
BF16 = jnp.bfloat16
I32 = jnp.int32

D_MODEL = 1024
CHUNK = 64
GMLP_BLOCK = 128
A_WIDTH = 2 * D_MODEL
A_HEADS = 8
A_HEAD_DIM = A_WIDTH // A_HEADS
N_HEADS = 8
HEAD_DIM = D_MODEL // N_HEADS
N_GROUPS = 4
EXPERTS_PER_GROUP = 4
EXPERT_DIM = D_MODEL // 2
EPS = 1e-6

LANES = 128
SUBLANES = 8
MXU_WIDTH = 256
SLABS = D_MODEL // LANES
assert SLABS == SUBLANES and N_HEADS == SLABS and HEAD_DIM == LANES
PAIRS = ((0, 1), (0, 2), (0, 3), (1, 2), (1, 3), (2, 3))
N_BUCKETS = N_GROUPS * len(PAIRS)
ROUTER_ROWS = 32
EXPERT_ROW0 = 8

MIXER_BLOCK = 2 * GMLP_BLOCK
STREAM_BLOCK = 512
EXPERT_TILE = 256
ATTN_TILE = 256
VMEM_LIMIT = 56 * 1024 * 1024
STOP_MASS = 105.0


def _cparams(*sem):
    return pltpu.CompilerParams(dimension_semantics=sem, vmem_limit_bytes=VMEM_LIMIT)


def _rms(x, g):
    ms = jnp.mean(x * x, axis=-1, keepdims=True)
    return x * lax.rsqrt(ms + EPS) * g


def _load_rows(ref, n):
    return jnp.concatenate([ref[pl.ds(c, n, stride=SLABS), :] for c in range(SLABS)], axis=1)


def _store_rows(ref, x):
    n = x.shape[0]
    for c in range(SLABS):
        ref[pl.ds(c, n, stride=SLABS), :] = x[:, c * LANES:(c + 1) * LANES]


def _row_copy(src_hbm, src_row, buf, slot, k, sem):
    src = src_row * SLABS if isinstance(src_row, int) else pl.multiple_of(src_row * SLABS, SLABS)
    return pltpu.make_async_copy(src_hbm.at[pl.ds(src, SLABS)], buf.at[slot, pl.ds(k * SLABS, SLABS)], sem.at[slot])


def _gather_step(idx_ref, src_hbm, buf, sem, rows, n_steps=None):
    i = pl.program_id(0)
    slot = i % 2
    if n_steps is None:
        n_steps = pl.num_programs(0)

    @pl.when(i == 0)
    def _():
        for k in range(rows):
            _row_copy(src_hbm, idx_ref[k], buf, 0, k, sem).start(priority=k % 2)

    @pl.when(i + 1 < n_steps)
    def _():
        for k in range(rows):
            _row_copy(src_hbm, idx_ref[(i + 1) * rows + k], buf, 1 - slot, k, sem).start(priority=k % 2)

    @pl.when(i < n_steps)
    def _():
        for k in range(rows):
            _row_copy(src_hbm, 0, buf, slot, k, sem).wait()

    return slot


def _first_argmax(vals):
    m = vals[0]
    for v in vals[1:]:
        m = jnp.maximum(m, v)
    idx = jnp.full(m.shape, len(vals) - 1, I32)
    for r in range(len(vals) - 2, -1, -1):
        idx = jnp.where(vals[r] == m, r, idx)
    return m, idx


def _route(xn, wrt_ref, rb_ref, xs_ref, bucket_ref):
    lt = lax.dot_general(wrt_ref[...], xn, (((1,), (1,)), ((), ())),
                         precision=lax.Precision.HIGHEST, preferred_element_type=F32) + rb_ref[:, 0:1]
    _, g_idx = _first_argmax([lt[r:r + 1, :] for r in range(N_GROUPS)])
    le = []
    for e in range(EXPERTS_PER_GROUP):
        row = EXPERT_ROW0 + EXPERTS_PER_GROUP * (N_GROUPS - 1) + e
        v = lt[row:row + 1, :]
        for g in range(N_GROUPS - 2, -1, -1):
            row = EXPERT_ROW0 + EXPERTS_PER_GROUP * g + e
            v = jnp.where(g_idx == g, lt[row:row + 1, :], v)
        le.append(v)
    _, i1 = _first_argmax(le)
    _, i2 = _first_argmax([jnp.where(i1 == e, -jnp.inf, le[e]) for e in range(EXPERTS_PER_GROUP)])
    lo = jnp.minimum(i1, i2)
    hi = jnp.maximum(i1, i2)
    pair = jnp.where(lo == 0, hi - 1, jnp.where(lo == 1, hi + 1, 5))
    bucket = g_idx * len(PAIRS) + pair
    for sb in range(xn.shape[0] // LANES):
        bucket_ref[sb] = bucket[:, sb * LANES:(sb + 1) * LANES]
    _store_rows(xs_ref, xn)


def _a_layer_kernel(xp_ref, xsm_ref, ag_ref, win_ref, vg_ref, ws_ref, bs_ref, wout_ref, fg_ref, wrt_ref, rb_ref,
                    h_ref, xs_ref, bucket_ref, vrow_ref, *, nps):
    i = pl.program_id(0)
    x = jnp.where(i < nps, xp_ref[...], xsm_ref[...])
    xn = _rms(x, ag_ref[...]).astype(BF16)
    n_col = 4
    cw = 2 * A_WIDTH // n_col
    z = [jax.nn.gelu(jnp.dot(xn, win_ref[:, c * cw:(c + 1) * cw], preferred_element_type=F32)) for c in range(n_col)]
    u = jnp.concatenate(z[:n_col // 2], axis=1)
    v = _rms(jnp.concatenate(z[n_col // 2:], axis=1), vg_ref[...])

    @pl.when(i >= nps)
    def _():
        vrow_ref[...] = v

    vb = v.astype(BF16)
    gate = jnp.concatenate([
        jnp.concatenate(
            [jnp.dot(ws_ref[0, h], vb[sb * GMLP_BLOCK:(sb + 1) * GMLP_BLOCK, h * A_HEAD_DIM:(h + 1) * A_HEAD_DIM],
                     preferred_element_type=F32) for h in range(A_HEADS)], axis=1) + bs_ref[0]
        for sb in range(x.shape[0] // GMLP_BLOCK)], axis=0)
    s = (u * gate).astype(BF16)
    h1 = x + jnp.dot(s, wout_ref[...], preferred_element_type=F32)
    h_ref[...] = h1
    _route(_rms(h1, fg_ref[...]), wrt_ref, rb_ref, xs_ref, bucket_ref)


def _oproj_kernel(op_ref, osm_ref, h_in_ref, wo_ref, fg_ref, wrt_ref, rb_ref, h_ref, xs_ref, bucket_ref, *, nps):
    i = pl.program_id(0)
    o = jnp.where(i < nps, op_ref[...], osm_ref[...])
    h3 = h_in_ref[...] + jnp.dot(o, wo_ref[...], preferred_element_type=F32)
    h_ref[...] = h3
    _route(_rms(h3, fg_ref[...]), wrt_ref, rb_ref, xs_ref, bucket_ref)


def _rank_kernel(b_ref, pos_ref, tb_ref, lo_ref, hi_ref, *, tm, nt):
    bk = b_ref[...]
    nr = bk.shape[0]
    r_i = lax.broadcasted_iota(I32, (LANES, LANES), 0)
    c_i = lax.broadcasted_iota(I32, (LANES, LANES), 1)
    upper = (r_i <= c_i).astype(BF16)
    rr = lax.broadcasted_iota(I32, (nr, nr), 0)
    cc = lax.broadcasted_iota(I32, (nr, nr), 1)
    before_rows = (cc < rr).astype(BF16)
    lane = lax.broadcasted_iota(I32, (1, LANES), 1)
    tile_start = lane.astype(F32) * tm
    pos = jnp.zeros((nr, LANES), F32)
    seg_start = jnp.zeros((1, LANES), F32)
    tile_bucket = jnp.zeros((1, LANES), I32)
    pad_lo = jnp.zeros((1, LANES), F32)
    pad_hi = jnp.zeros((1, LANES), F32)
    for b in range(N_BUCKETS):
        m = bk == b
        pref = jnp.dot(jnp.where(m, 1.0, 0.0).astype(BF16), upper, preferred_element_type=F32)
        rowtot = jnp.broadcast_to(pref[:, LANES - 1:LANES], (nr, LANES))
        before = jnp.dot(before_rows, rowtot.astype(BF16), preferred_element_type=F32)
        cnt = jnp.sum(rowtot, axis=0, keepdims=True)
        pos = pos + jnp.where(m, seg_start + before + pref - 1.0, 0.0)
        pad_lo = jnp.where(lane == b, seg_start + cnt, pad_lo)
        seg_start = seg_start + jnp.ceil(cnt / tm) * tm
        pad_hi = jnp.where(lane == b, seg_start, pad_hi)
        tile_bucket = tile_bucket + (seg_start <= tile_start).astype(I32)
    pad_lo = jnp.where(lane == N_BUCKETS, seg_start, pad_lo)
    pad_hi = jnp.where(lane == N_BUCKETS, float(nt * tm), pad_hi)
    pos_ref[...] = pos.astype(I32)
    tb_ref[...] = tile_bucket
    lo_ref[...] = pad_lo.astype(I32)
    hi_ref[...] = pad_hi.astype(I32)


def _inverse_kernel(pos_ref, lo_ref, hi_ref, inv_ref, *, n_tok):
    batch = 8
    assert n_tok % batch == 0

    def clear(r, v):
        inv_ref[r] = v
        return jnp.where(v + 1 == n_tok, 0, v + 1)

    def put(g, carry):
        base = g * batch
        dst = [pos_ref[base + j] for j in range(batch)]
        for j in range(batch):
            inv_ref[dst[j]] = base + j
        return carry

    for b in range(N_BUCKETS + 1):
        lax.fori_loop(lo_ref[b], hi_ref[b], clear, lax.rem(lo_ref[b], n_tok))
    lax.fori_loop(0, n_tok // batch, put, 0)


def _expert_kernel(telo_ref, tehi_ref, nused_ref, inv_ref, xs_hbm, wr_ref, rb_ref, wgl_ref, wul_ref, wdl_ref,
                   wgh_ref, wuh_ref, wdh_ref, y_ref, xbuf, wup, wdown, gsem, *, tm):
    r = pl.program_id(0)
    n_used = nused_ref[0]
    slot = _gather_step(inv_ref, xs_hbm, xbuf, gsem, tm, n_steps=n_used)

    @pl.when(r < n_used)
    def _():
        prev = jnp.maximum(r - 1, 0)

        @pl.when(jnp.logical_or(r == 0, telo_ref[r] != telo_ref[prev]))
        def _():
            wup[0] = wgl_ref[0, 0].astype(BF16)
            wup[1] = wul_ref[0, 0].astype(BF16)
            wdown[0] = wdl_ref[0, 0].astype(BF16)

        @pl.when(jnp.logical_or(r == 0, tehi_ref[r] != tehi_ref[prev]))
        def _():
            wup[2] = wgh_ref[0, 0].astype(BF16)
            wup[3] = wuh_ref[0, 0].astype(BF16)
            wdown[1] = wdh_ref[0, 0].astype(BF16)

        x = _load_rows(xbuf.at[slot], tm).astype(BF16)
        logits = jnp.dot(x, wr_ref[...], preferred_element_type=F32) + rb_ref[...]
        lane = lax.broadcasted_iota(I32, logits.shape, 1)
        grp = telo_ref[r] // EXPERTS_PER_GROUP
        is_group = lane < N_GROUPS
        m = jnp.max(jnp.where(is_group, logits, -jnp.inf), axis=1, keepdims=True)
        ex = jnp.exp(logits - m)
        p_g = (jnp.sum(jnp.where(lane == grp, ex, 0.0), axis=1, keepdims=True)
               / jnp.sum(jnp.where(is_group, ex, 0.0), axis=1, keepdims=True))
        l_lo = jnp.sum(jnp.where(lane == EXPERT_ROW0 + telo_ref[r], logits, 0.0), axis=1, keepdims=True)
        l_hi = jnp.sum(jnp.where(lane == EXPERT_ROW0 + tehi_ref[r], logits, 0.0), axis=1, keepdims=True)
        mm = jnp.maximum(l_lo, l_hi)
        e_lo = jnp.exp(l_lo - mm)
        e_hi = jnp.exp(l_hi - mm)
        gates = (p_g * (e_lo / (e_lo + e_hi)), p_g * (e_hi / (e_lo + e_hi)))

        y = None
        for which in range(2):
            a = jnp.dot(x, wup[2 * which], preferred_element_type=F32)
            b = jnp.dot(x, wup[2 * which + 1], preferred_element_type=F32)
            hid = (jax.nn.silu(a) * b * gates[which]).astype(BF16)
            part = jnp.dot(hid, wdown[which], preferred_element_type=F32)
            y = part if y is None else y + part
        _store_rows(y_ref, y)

    @pl.when(r >= n_used)
    def _():
        y_ref[...] = jnp.zeros_like(y_ref)


def _head_rms(x, g):
    return jnp.concatenate([_rms(x[:, h * HEAD_DIM:(h + 1) * HEAD_DIM], g) for h in range(N_HEADS)], axis=1)


def _kvq_kernel(pos_ref, h_in_ref, y_hbm, kvg_ref, wkv_ref, kng_ref, bg_ref, wq_ref, qng_ref,
                h_ref, kp_ref, vp_ref, ksm_ref, vsm_ref, kb_ref, vb_ref, qb_ref, qsm_ref, ybuf, sem, *, npt):
    i = pl.program_id(0)
    tk = h_in_ref.shape[0]
    h2 = h_in_ref[...] + _load_rows(ybuf.at[_gather_step(pos_ref, y_hbm, ybuf, sem, tk)], tk)
    h_ref[...] = h2
    kv = jnp.dot(_rms(h2, kvg_ref[...]).astype(BF16), wkv_ref[...], preferred_element_type=F32)
    q = jnp.dot(_rms(h2, bg_ref[...]).astype(BF16), wq_ref[...], preferred_element_type=F32)
    k = _head_rms(kv[:, :D_MODEL], kng_ref[...])
    v = kv[:, D_MODEL:]
    q = _head_rms(q, qng_ref[...]) * (HEAD_DIM ** -0.5)

    @pl.when(i < npt)
    def _():
        _store_rows(kp_ref, k)
        _store_rows(vp_ref, v)
        for h in range(N_HEADS):
            sl = slice(h * HEAD_DIM, (h + 1) * HEAD_DIM)
            kb_ref[0, h] = k[:, sl].astype(BF16)
            vb_ref[0, h] = v[:, sl].astype(BF16)
            qb_ref[0, h] = q[:, sl].astype(BF16)

    @pl.when(i >= npt)
    def _():
        _store_rows(ksm_ref, k)
        _store_rows(vsm_ref, v)
        qsm_ref[...] = q


def _stick_blocks(qs, ks, vs, rs, visible, later_keys):
    n = len(qs)
    zs = [lax.dot_general(qs[h], ks[h], (((1,), (1,)), ((), ())), preferred_element_type=F32) for h in range(n)]
    sps = [jnp.maximum(z, 0.0) + jnp.log(1.0 + jnp.exp(-jnp.abs(z))) for z in zs]
    if visible is not None:
        sps = [jnp.where(visible, sp, 0.0) for sp in sps]
    his = [sp.astype(BF16) for sp in sps]
    los = [(sp - hi.astype(F32)).astype(BF16) for sp, hi in zip(sps, his)]
    cs = [jnp.dot(hi, later_keys, preferred_element_type=F32) + jnp.dot(lo, later_keys, preferred_element_type=F32)
          for hi, lo in zip(his, los)]
    ws = [jnp.exp(zs[h] - cs[h] - rs[h]) for h in range(n)]
    if visible is not None:
        ws = [jnp.where(visible, w, 0.0) for w in ws]
    outs = [jnp.dot(ws[h].astype(BF16), vs[h], preferred_element_type=F32) for h in range(n)]
    return outs, [rs[h] + cs[h][:, 0:1] for h in range(n)]


def _later_keys(n):
    rowi = lax.broadcasted_iota(I32, (n, n), 0)
    coli = lax.broadcasted_iota(I32, (n, n), 1)
    return (rowi >= coli).astype(BF16)


def _min_over(rs):
    m = rs[0]
    for r in rs[1:]:
        m = jnp.minimum(m, r)
    return jnp.min(m)


def _attn_kernel(q_ref, k_ref, v_ref, o_ref, r_sc, acc_sc):
    i = pl.program_id(1)
    tq = q_ref.shape[2]
    visible = lax.broadcasted_iota(I32, (tq, tq), 1) < lax.broadcasted_iota(I32, (tq, tq), 0)
    later_keys = _later_keys(tq)
    heads = range(N_HEADS)

    def block(j, first):
        start = pl.multiple_of(j * tq, tq)
        qs = [q_ref[0, h] for h in heads]
        ks = [k_ref[0, h, pl.ds(start, tq), :] for h in heads]
        vs = [v_ref[0, h, pl.ds(start, tq), :] for h in heads]
        rs = [jnp.zeros((tq, 1), F32) if first else r_sc[h] for h in heads]
        outs, rs = _stick_blocks(qs, ks, vs, rs, visible if first else None, later_keys)
        acc = jnp.concatenate(outs, axis=1)
        if first:
            acc_sc[...] = acc
        else:
            acc_sc[...] += acc
        for h in heads:
            r_sc[h] = rs[h]
        return _min_over(rs)

    rmin = block(i, True)
    lax.while_loop(lambda st: jnp.logical_and(st[0] >= 0, st[1] <= STOP_MASS),
                   lambda st: (st[0] - 1, block(st[0], False)), (i - 1, rmin))
    o_ref[...] = acc_sc[...].astype(o_ref.dtype)


def _sample_attn_kernel(q_ref, kn_ref, vn_ref, ck_ref, cv_ref, ck_hbm, cv_hbm, o_ref, r_sc, acc_sc, kbuf, vbuf, sem,
                        *, nkc):
    b = pl.program_id(0)
    ds = q_ref.shape[0]
    ck = ck_ref.shape[0] // N_HEADS
    heads = range(N_HEADS)

    def head_rows(ref, h, n):
        return ref[pl.ds(h, n, stride=N_HEADS), :].astype(BF16)

    qs = [q_ref[:, h * HEAD_DIM:(h + 1) * HEAD_DIM].astype(BF16) for h in heads]
    visible = lax.broadcasted_iota(I32, (ds, LANES), 1) < lax.broadcasted_iota(I32, (ds, LANES), 0)
    pad = jnp.zeros((LANES - ds, HEAD_DIM), BF16)
    outs, rs = _stick_blocks(qs, [jnp.concatenate([head_rows(kn_ref, h, ds), pad], axis=0) for h in heads],
                             [jnp.concatenate([head_rows(vn_ref, h, ds), pad], axis=0) for h in heads],
                             [jnp.zeros((ds, 1), F32)] * N_HEADS, visible, _later_keys(LANES))
    acc_sc[...] = jnp.concatenate(outs, axis=1)
    for h in heads:
        r_sc[h] = rs[h]

    def chunk(kref, vref):
        outs, rs = _stick_blocks(qs, [head_rows(kref, h, ck) for h in heads], [head_rows(vref, h, ck) for h in heads],
                                 [r_sc[h] for h in heads], None, _later_keys(ck))
        acc_sc[...] += jnp.concatenate(outs, axis=1)
        for h in heads:
            r_sc[h] = rs[h]
        return _min_over(rs)

    def older(st):
        c = st[0]
        rows = ck * N_HEADS
        start = pl.multiple_of((b * nkc + c) * rows, rows)
        cpk = pltpu.make_async_copy(ck_hbm.at[pl.ds(start, rows)], kbuf, sem.at[0])
        cpv = pltpu.make_async_copy(cv_hbm.at[pl.ds(start, rows)], vbuf, sem.at[1])
        cpk.start()
        cpv.start()
        cpk.wait()
        cpv.wait()
        return c - 1, chunk(kbuf, vbuf)

    lax.while_loop(lambda st: jnp.logical_and(st[0] >= 0, st[1] <= STOP_MASS), older,
                   (nkc - 2, chunk(ck_ref, cv_ref)))
    o_ref[...] = acc_sc[...].astype(o_ref.dtype)


def _final_kernel(pos_ref, h_ref, y_hbm, yp_ref, ysm_ref, ybuf, sem, *, npt):
    i = pl.program_id(0)
    tk = h_ref.shape[0]
    y = h_ref[...] + _load_rows(ybuf.at[_gather_step(pos_ref, y_hbm, ybuf, sem, tk)], tk)

    @pl.when(i < npt)
    def _():
        yp_ref[...] = y

    @pl.when(i >= npt)
    def _():
        ysm_ref[...] = y


def _const_spec(shape):
    return pl.BlockSpec(shape, lambda *_: (0,) * len(shape))


def _router_operands(layer, ffn_norm_g, router_g_w, router_g_b, router_e_w, router_e_b):
    n_e = N_GROUPS * EXPERTS_PER_GROUP
    wrt = jnp.zeros((ROUTER_ROWS, D_MODEL), F32)
    wrt = wrt.at[:N_GROUPS].set(router_g_w[layer].T)
    wrt = wrt.at[EXPERT_ROW0:EXPERT_ROW0 + n_e].set(router_e_w[layer].transpose(0, 2, 1).reshape(n_e, D_MODEL))
    rb = jnp.zeros((ROUTER_ROWS,), F32)
    rb = rb.at[:N_GROUPS].set(router_g_b[layer])
    rb = rb.at[EXPERT_ROW0:EXPERT_ROW0 + n_e].set(router_e_b[layer].reshape(-1))
    wr_cols = jnp.zeros((D_MODEL, LANES), F32).at[:, :ROUTER_ROWS].set(wrt.T).astype(BF16)
    rb_cols = jnp.zeros((1, LANES), F32).at[0, :ROUTER_ROWS].set(rb)
    return (ffn_norm_g[layer][None, :], wrt, jnp.broadcast_to(rb[:, None], (ROUTER_ROWS, LANES))), (wr_cols, rb_cols)


def _moe(xs_rt, bucket, layer, route_cols, w_gate, w_up, w_down):
    nblk = bucket.shape[0]
    n_tok = nblk * LANES
    tm = EXPERT_TILE
    nr = -(-nblk // LANES) * LANES
    nt = -(-(n_tok + N_BUCKETS * (tm - 1)) // tm)
    assert nt <= LANES
    bk = jnp.pad(bucket.reshape(nblk, LANES), ((0, nr - nblk), (0, 0)), constant_values=N_BUCKETS)
    lane_row = jax.ShapeDtypeStruct((1, LANES), I32)
    pos, tile_bucket, pad_lo, pad_hi = pl.pallas_call(
        functools.partial(_rank_kernel, tm=tm, nt=nt),
        out_shape=(jax.ShapeDtypeStruct((nr, LANES), I32), lane_row, lane_row, lane_row),
        name=f"moe_rank_{layer}",
    )(bk)
    pos = pos[:nblk].reshape(n_tok)
    smem = pl.BlockSpec(memory_space=pltpu.SMEM)
    inv = pl.pallas_call(
        functools.partial(_inverse_kernel, n_tok=n_tok),
        out_shape=jax.ShapeDtypeStruct((nt * tm,), I32),
        in_specs=[smem, smem, smem],
        out_specs=smem,
        name=f"moe_inverse_{layer}",
    )(pos, pad_lo[0], pad_hi[0])
    tb = tile_bucket[0, :nt]
    n_used = jnp.sum((tb < N_BUCKETS).astype(I32))
    tsrc = jnp.minimum(jnp.arange(nt, dtype=I32), n_used - 1)
    tbc = jnp.minimum(tb[tsrc], N_BUCKETS - 1)
    pair_lo = jnp.array([p[0] for p in PAIRS], I32)
    pair_hi = jnp.array([p[1] for p in PAIRS], I32)
    grp = tbc // len(PAIRS)
    telo = grp * EXPERTS_PER_GROUP + pair_lo[tbc % len(PAIRS)]
    tehi = grp * EXPERTS_PER_GROUP + pair_hi[tbc % len(PAIRS)]

    def wspec(shape, which):
        return pl.BlockSpec((1, 1) + shape, lambda r, lo, hi, *_: (layer, (lo, hi)[which][r], 0, 0))

    up_shape = (D_MODEL, EXPERT_DIM)
    down_shape = (EXPERT_DIM, D_MODEL)
    wr_cols, rb_cols = route_cols
    tile_shape = (tm * SLABS, LANES)
    y_sorted = pl.pallas_call(
        functools.partial(_expert_kernel, tm=tm),
        out_shape=jax.ShapeDtypeStruct((nt * tm * SLABS, LANES), F32),
        grid_spec=pltpu.PrefetchScalarGridSpec(
            num_scalar_prefetch=4, grid=(nt,),
            in_specs=[pl.BlockSpec(memory_space=pl.ANY),
                      pl.BlockSpec((D_MODEL, LANES), lambda r, *_: (0, 0)),
                      pl.BlockSpec((1, LANES), lambda r, *_: (0, 0)),
                      wspec(up_shape, 0), wspec(up_shape, 0), wspec(down_shape, 0),
                      wspec(up_shape, 1), wspec(up_shape, 1), wspec(down_shape, 1)],
            out_specs=pl.BlockSpec(tile_shape, lambda r, *_: (r, 0)),
            scratch_shapes=[pltpu.VMEM((2,) + tile_shape, F32), pltpu.VMEM((4,) + up_shape, BF16),
                            pltpu.VMEM((2,) + down_shape, BF16), pltpu.SemaphoreType.DMA((2,))]),
        compiler_params=_cparams("arbitrary"),
        name=f"moe_experts_{layer}",
    )(telo, tehi, n_used.reshape(1), inv, xs_rt, wr_cols, rb_cols, w_gate, w_up, w_down, w_gate, w_up, w_down)
    return y_sorted, pos


def kernel(x_prompt, x_sample, cache_k, cache_v, a_norm_g, a_w_in, a_v_norm_g, a_w_s, a_b_s, a_w_out, kv_norm_g, w_kv, k_norm_g, b_norm_g, b_w_q, q_norm_g, b_w_o, ffn_norm_g, router_g_w, router_g_b, router_e_w, router_e_b, moe_w_gate, moe_w_up, moe_w_down):
    bsz, seq, _ = x_prompt.shape
    dbsz, dseq, _ = x_sample.shape
    past = cache_k.shape[1]
    tp, ts = bsz * seq, dbsz * dseq
    tk = STREAM_BLOCK
    tsp = -(-ts // tk) * tk
    n_tok = tp + tsp
    tb = MIXER_BLOCK
    gb = GMLP_BLOCK
    assert seq % ATTN_TILE == 0 and tp % tk == 0 and tk % tb == 0 and tk % ATTN_TILE == 0
    assert gb % dseq == 0 and dseq <= CHUNK and past % ATTN_TILE == 0
    nps, nss = tp // tb, tsp // tb
    npt, nst = tp // tk, tsp // tk
    xp = x_prompt.reshape(tp, D_MODEL)
    xsm = jnp.pad(x_sample.reshape(ts, D_MODEL), ((0, tsp - ts), (0, 0)))

    def rt_shape(n):
        return jax.ShapeDtypeStruct((n * SLABS, LANES), F32)

    def rt_spec(rows, index):
        return pl.BlockSpec((rows * SLABS, LANES), lambda i, *_: (index(i), 0))

    def route_specs(rows, n_prompt_steps):
        def pstep(i):
            return jnp.minimum(i, n_prompt_steps - 1)

        def sstep(i):
            return jnp.maximum(i - n_prompt_steps, 0)

        out_specs = [pl.BlockSpec((rows, D_MODEL), lambda i: (i, 0)), rt_spec(rows, lambda i: i),
                     pl.BlockSpec((rows // LANES, 1, LANES), lambda i: (i, 0, 0))]
        return pstep, sstep, out_specs

    route_in_specs = [_const_spec((1, D_MODEL)), _const_spec((ROUTER_ROWS, D_MODEL)),
                      _const_spec((ROUTER_ROWS, LANES))]
    route_out_shapes = (jax.ShapeDtypeStruct((n_tok, D_MODEL), F32), rt_shape(n_tok),
                        jax.ShapeDtypeStruct((n_tok // LANES, 1, LANES), I32))
    pstep, sstep, route_out_specs = route_specs(tb, nps)

    pos_i = jnp.arange(gb)
    mask = (pos_i[None, :] // CHUNK) <= (pos_i[:, None] // CHUNK)
    w_prompt = jnp.where(mask, a_w_s[0], 0.0)
    rep = gb // dseq
    w_sample = jnp.einsum("ij,hts->hitjs", jnp.eye(rep, dtype=F32), w_prompt[:, :dseq, :dseq]).reshape(
        A_HEADS, gb, gb)
    ws_all = jnp.stack([w_prompt, w_sample]).astype(BF16)
    b_prompt = a_b_s[0]
    b_sample = jnp.tile(a_b_s[0][:, :dseq], (1, rep))
    bs_all = jnp.stack([jnp.repeat(b.T, A_HEAD_DIM, axis=1) for b in (b_prompt, b_sample)])
    route0, cols0 = _router_operands(0, ffn_norm_g, router_g_w, router_g_b, router_e_w, router_e_b)
    h1, xs1, bucket1, v_rows = pl.pallas_call(
        functools.partial(_a_layer_kernel, nps=nps),
        out_shape=route_out_shapes + (jax.ShapeDtypeStruct((tsp, A_WIDTH), F32),),
        grid=(nps + nss,),
        in_specs=[pl.BlockSpec((tb, D_MODEL), lambda i: (pstep(i), 0)),
                  pl.BlockSpec((tb, D_MODEL), lambda i: (sstep(i), 0)),
                  _const_spec((1, D_MODEL)), _const_spec((D_MODEL, 2 * A_WIDTH)), _const_spec((1, A_WIDTH)),
                  pl.BlockSpec((1, A_HEADS, gb, gb), lambda i: (i // nps, 0, 0, 0)),
                  pl.BlockSpec((1, gb, A_WIDTH), lambda i: (i // nps, 0, 0)),
                  _const_spec((A_WIDTH, D_MODEL))] + route_in_specs,
        out_specs=route_out_specs + [pl.BlockSpec((tb, A_WIDTH), lambda i: (sstep(i), 0))],
        compiler_params=_cparams("arbitrary"),
        name="a_layer",
    )(xp, xsm, a_norm_g[0][None, :], a_w_in[0].astype(BF16), a_v_norm_g[0][None, :], ws_all, bs_all,
      a_w_out[0].astype(BF16), *route0)
    y1, pos1 = _moe(xs1, bucket1, 0, cols0, moe_w_gate, moe_w_up, moe_w_down)

    spt = seq // tk

    def ptile(i):
        return jnp.minimum(i, npt - 1)

    def stile(i):
        return jnp.maximum(i - npt, 0)

    row_spec = pl.BlockSpec((tk, D_MODEL), lambda i, *_: (i, 0))
    prow_spec = pl.BlockSpec((tk, D_MODEL), lambda i, *_: (ptile(i), 0))
    srow_spec = pl.BlockSpec((tk, D_MODEL), lambda i, *_: (stile(i), 0))
    head_spec = pl.BlockSpec((1, N_HEADS, tk, HEAD_DIM), lambda i, *_: (ptile(i) // spt, 0, ptile(i) % spt, 0))
    any_spec = pl.BlockSpec(memory_space=pl.ANY)
    gather_scratch = [pltpu.VMEM((2, tk * SLABS, LANES), F32), pltpu.SemaphoreType.DMA((2,))]
    f32_rows = lambda n: jax.ShapeDtypeStruct((n, D_MODEL), F32)
    head_major = jax.ShapeDtypeStruct((bsz, N_HEADS, seq, HEAD_DIM), BF16)
    h2, k_p, v_p, k_s, v_s, kb, vb, qb, q_s = pl.pallas_call(
        functools.partial(_kvq_kernel, npt=npt),
        out_shape=(f32_rows(n_tok), rt_shape(tp), rt_shape(tp), rt_shape(tsp), rt_shape(tsp),
                   head_major, head_major, head_major, f32_rows(tsp)),
        grid_spec=pltpu.PrefetchScalarGridSpec(
            num_scalar_prefetch=1, grid=(npt + nst,),
            in_specs=[row_spec, any_spec, _const_spec((1, D_MODEL)),
                      _const_spec((D_MODEL, 2 * D_MODEL)), _const_spec((1, HEAD_DIM)), _const_spec((1, D_MODEL)),
                      _const_spec((D_MODEL, D_MODEL)), _const_spec((1, HEAD_DIM))],
            out_specs=[row_spec, rt_spec(tk, ptile), rt_spec(tk, ptile), rt_spec(tk, stile), rt_spec(tk, stile),
                       head_spec, head_spec, head_spec, srow_spec],
            scratch_shapes=gather_scratch),
        compiler_params=_cparams("arbitrary"),
        name="kvq",
    )(pos1, h1, y1, kv_norm_g[None, :], w_kv.astype(BF16), k_norm_g[None, :], b_norm_g[0][None, :],
      b_w_q[0].astype(BF16), q_norm_g[0][None, :])

    tq = ATTN_TILE
    nq = seq // tq
    qspec = pl.BlockSpec((1, N_HEADS, tq, HEAD_DIM), lambda b, i: (b, 0, i, 0))
    kvspec = pl.BlockSpec((1, N_HEADS, seq, HEAD_DIM), lambda b, i: (b, 0, 0, 0), pipeline_mode=pl.Buffered(1))
    o_p = pl.pallas_call(
        _attn_kernel,
        out_shape=jax.ShapeDtypeStruct((tp, D_MODEL), BF16),
        grid=(bsz, nq),
        in_specs=[qspec, kvspec, kvspec],
        out_specs=pl.BlockSpec((tq, D_MODEL), lambda b, i: (b * nq + i, 0)),
        scratch_shapes=[pltpu.VMEM((N_HEADS, tq, 1), F32), pltpu.VMEM((tq, D_MODEL), F32)],
        compiler_params=_cparams("arbitrary", "arbitrary"),
        name="attn_prompt",
    )(qb, kb, vb)

    ck = ATTN_TILE
    nkc = past // ck
    new_spec = pl.BlockSpec((dseq, D_MODEL), lambda b: (b, 0))
    new_rt_spec = rt_spec(dseq, lambda b: b)
    newest_spec = rt_spec(ck, lambda b: b * nkc + nkc - 1)
    cache_k2 = cache_k.reshape(dbsz * past * N_HEADS, HEAD_DIM)
    cache_v2 = cache_v.reshape(dbsz * past * N_HEADS, HEAD_DIM)
    chunk_buf = pltpu.VMEM((ck * N_HEADS, HEAD_DIM), F32)
    o_s = pl.pallas_call(
        functools.partial(_sample_attn_kernel, nkc=nkc),
        out_shape=jax.ShapeDtypeStruct((ts, D_MODEL), BF16),
        grid=(dbsz,),
        in_specs=[new_spec, new_rt_spec, new_rt_spec, newest_spec, newest_spec, any_spec, any_spec],
        out_specs=new_spec,
        scratch_shapes=[pltpu.VMEM((N_HEADS, dseq, 1), F32), pltpu.VMEM((dseq, D_MODEL), F32), chunk_buf, chunk_buf,
                        pltpu.SemaphoreType.DMA((2,))],
        compiler_params=_cparams("arbitrary"),
        name="attn_sample",
    )(q_s, k_s, v_s, cache_k2, cache_v2, cache_k2, cache_v2)
    o_s = jnp.pad(o_s, ((0, tsp - ts), (0, 0)))

    route1, cols1 = _router_operands(1, ffn_norm_g, router_g_w, router_g_b, router_e_w, router_e_b)
    pstep, sstep, route_out_specs = route_specs(tk, npt)
    h3, xs2, bucket2 = pl.pallas_call(
        functools.partial(_oproj_kernel, nps=npt),
        out_shape=route_out_shapes,
        grid=(npt + nst,),
        in_specs=[pl.BlockSpec((tk, D_MODEL), lambda i: (pstep(i), 0)),
                  pl.BlockSpec((tk, D_MODEL), lambda i: (sstep(i), 0)),
                  pl.BlockSpec((tk, D_MODEL), lambda i: (i, 0)), _const_spec((D_MODEL, D_MODEL))] + route_in_specs,
        out_specs=route_out_specs,
        compiler_params=_cparams("arbitrary"),
        name="o_proj",
    )(o_p, o_s, h2, b_w_o[0].astype(BF16), *route1)
    y2, pos2 = _moe(xs2, bucket2, 1, cols1, moe_w_gate, moe_w_up, moe_w_down)

    y_p, y_s = pl.pallas_call(
        functools.partial(_final_kernel, npt=npt),
        out_shape=(f32_rows(tp), f32_rows(tsp)),
        grid_spec=pltpu.PrefetchScalarGridSpec(
            num_scalar_prefetch=1, grid=(npt + nst,),
            in_specs=[row_spec, any_spec],
            out_specs=[prow_spec, srow_spec],
            scratch_shapes=gather_scratch),
        compiler_params=_cparams("arbitrary"),
        name="final_residual",
    )(pos2, h3, y2)

    kv_shape_p = (bsz, seq, N_HEADS, HEAD_DIM)
    kv_shape_s = (dbsz, dseq, N_HEADS, HEAD_DIM)
    k_s = k_s[:ts * SLABS].reshape(kv_shape_s)
    v_s = v_s[:ts * SLABS].reshape(kv_shape_s)
    return (y_p.reshape(bsz, seq, D_MODEL), y_s[:ts].reshape(dbsz, dseq, D_MODEL),
            k_p.reshape(kv_shape_p), v_p.reshape(kv_shape_p), k_s, v_s,
            v_rows[:ts].reshape(1, dbsz, dseq, A_WIDTH))
```

```python
import functools

import jax
import jax.numpy as jnp
from jax import lax
from jax.experimental import pallas as pl
from jax.experimental.pallas import tpu as pltpu

F32 = jnp.float32
BF16 = jnp.bfloat16
I32 = jnp.int32

D_MODEL = 1024
CHUNK = 64
GMLP_BLOCK = 128
A_WIDTH = 2 * D_MODEL
A_HEADS = 8
A_HEAD_DIM = A_WIDTH // A_HEADS
N_HEADS = 8
HEAD_DIM = D_MODEL // N_HEADS
N_GROUPS = 4
EXPERTS_PER_GROUP = 4
EXPERT_DIM = D_MODEL // 2
EPS = 1e-6

LANES = 128
SUBLANES = 8
MXU_WIDTH = 256
SLABS = D_MODEL // LANES
assert SLABS == SUBLANES and N_HEADS == SLABS and HEAD_DIM == LANES
PAIRS = ((0, 1), (0, 2), (0, 3), (1, 2), (1, 3), (2, 3))
N_BUCKETS = N_GROUPS * len(PAIRS)
ROUTER_ROWS = 32
EXPERT_ROW0 = 8

MIXER_BLOCK = 4 * GMLP_BLOCK
STREAM_BLOCK = 512
EXPERT_TILE = 256
ATTN_TILE = 256
VMEM_LIMIT = 56 * 1024 * 1024
STOP_MASS = 105.0


def _cparams(*sem):
    return pltpu.CompilerParams(dimension_semantics=sem, vmem_limit_bytes=VMEM_LIMIT)


def _rms(x, g):
    ms = jnp.mean(x * x, axis=-1, keepdims=True)
    return x * lax.rsqrt(ms + EPS) * g


def _load_rows(ref, n):
    return jnp.concatenate([ref[pl.ds(c, n, stride=SLABS), :] for c in range(SLABS)], axis=1)


def _store_rows(ref, x):
    n = x.shape[0]
    for c in range(SLABS):
        ref[pl.ds(c, n, stride=SLABS), :] = x[:, c * LANES:(c + 1) * LANES]


def _row_copy(src_hbm, src_row, buf, slot, k, sem):
    src = src_row * SLABS if isinstance(src_row, int) else pl.multiple_of(src_row * SLABS, SLABS)
    return pltpu.make_async_copy(src_hbm.at[pl.ds(src, SLABS)], buf.at[slot, pl.ds(k * SLABS, SLABS)], sem.at[slot])


def _gather_step(idx_ref, src_hbm, buf, sem, rows, n_steps=None):
    i = pl.program_id(0)
    slot = i % 2
    if n_steps is None:
        n_steps = pl.num_programs(0)

    @pl.when(i == 0)
    def _():
        for k in range(rows):
            _row_copy(src_hbm, idx_ref[k], buf, 0, k, sem).start(priority=k % 2)

    @pl.when(i + 1 < n_steps)
    def _():
        for k in range(rows):
            _row_copy(src_hbm, idx_ref[(i + 1) * rows + k], buf, 1 - slot, k, sem).start(priority=k % 2)

    @pl.when(i < n_steps)
    def _():
        for k in range(rows):
            _row_copy(src_hbm, 0, buf, slot, k, sem).wait()

    return slot


def _first_argmax(vals):
    m = vals[0]
    for v in vals[1:]:
        m = jnp.maximum(m, v)
    idx = jnp.full(m.shape, len(vals) - 1, I32)
    for r in range(len(vals) - 2, -1, -1):
        idx = jnp.where(vals[r] == m, r, idx)
    return m, idx


def _route(xn, wrt_ref, rb_ref, xs_ref, bucket_ref):
    lt = lax.dot_general(wrt_ref[...], xn, (((1,), (1,)), ((), ())),
                         precision=lax.Precision.HIGHEST, preferred_element_type=F32) + rb_ref[:, 0:1]
    _, g_idx = _first_argmax([lt[r:r + 1, :] for r in range(N_GROUPS)])
    le = []
    for e in range(EXPERTS_PER_GROUP):
        row = EXPERT_ROW0 + EXPERTS_PER_GROUP * (N_GROUPS - 1) + e
        v = lt[row:row + 1, :]
        for g in range(N_GROUPS - 2, -1, -1):
            row = EXPERT_ROW0 + EXPERTS_PER_GROUP * g + e
            v = jnp.where(g_idx == g, lt[row:row + 1, :], v)
        le.append(v)
    _, i1 = _first_argmax(le)
    _, i2 = _first_argmax([jnp.where(i1 == e, -jnp.inf, le[e]) for e in range(EXPERTS_PER_GROUP)])
    lo = jnp.minimum(i1, i2)
    hi = jnp.maximum(i1, i2)
    pair = jnp.where(lo == 0, hi - 1, jnp.where(lo == 1, hi + 1, 5))
    bucket = g_idx * len(PAIRS) + pair
    for sb in range(xn.shape[0] // LANES):
        bucket_ref[sb] = bucket[:, sb * LANES:(sb + 1) * LANES]
    _store_rows(xs_ref, xn)


def _a_layer_kernel(xp_ref, xsm_ref, ag_ref, win_ref, vg_ref, ws_ref, bs_ref, wout_ref, fg_ref, wrt_ref, rb_ref,
                    h_ref, xs_ref, bucket_ref, vrow_ref, *, nps):
    i = pl.program_id(0)
    x = jnp.where(i < nps, xp_ref[...], xsm_ref[...])
    xn = _rms(x, ag_ref[...]).astype(BF16)
    n_col = 4
    cw = 2 * A_WIDTH // n_col
    z = [jax.nn.gelu(jnp.dot(xn, win_ref[:, c * cw:(c + 1) * cw], preferred_element_type=F32)) for c in range(n_col)]
    u = jnp.concatenate(z[:n_col // 2], axis=1)
    v = _rms(jnp.concatenate(z[n_col // 2:], axis=1), vg_ref[...])

    @pl.when(i >= nps)
    def _():
        vrow_ref[...] = v

    vb = v.astype(BF16)
    gate = jnp.concatenate([
        jnp.concatenate(
            [jnp.dot(ws_ref[0, h], vb[sb * GMLP_BLOCK:(sb + 1) * GMLP_BLOCK, h * A_HEAD_DIM:(h + 1) * A_HEAD_DIM],
                     preferred_element_type=F32) for h in range(A_HEADS)], axis=1) + bs_ref[0]
        for sb in range(x.shape[0] // GMLP_BLOCK)], axis=0)
    s = (u * gate).astype(BF16)
    h1 = x + jnp.dot(s, wout_ref[...], preferred_element_type=F32)
    h_ref[...] = h1
    _route(_rms(h1, fg_ref[...]), wrt_ref, rb_ref, xs_ref, bucket_ref)


def _oproj_kernel(op_ref, osm_ref, h_in_ref, wo_ref, fg_ref, wrt_ref, rb_ref, h_ref, xs_ref, bucket_ref, *, nps):
    i = pl.program_id(0)
    o = jnp.where(i < nps, op_ref[...], osm_ref[...])
    h3 = h_in_ref[...] + jnp.dot(o, wo_ref[...], preferred_element_type=F32)
    h_ref[...] = h3
    _route(_rms(h3, fg_ref[...]), wrt_ref, rb_ref, xs_ref, bucket_ref)


def _rank_kernel(b_ref, pos_ref, tb_ref, lo_ref, hi_ref, *, tm, nt):
    bk = b_ref[...]
    nr = bk.shape[0]
    r_i = lax.broadcasted_iota(I32, (LANES, LANES), 0)
    c_i = lax.broadcasted_iota(I32, (LANES, LANES), 1)
    upper = (r_i <= c_i).astype(BF16)
    rr = lax.broadcasted_iota(I32, (nr, nr), 0)
    cc = lax.broadcasted_iota(I32, (nr, nr), 1)
    before_rows = (cc < rr).astype(BF16)
    lane = lax.broadcasted_iota(I32, (1, LANES), 1)
    tile_start = lane.astype(F32) * tm
    pos = jnp.zeros((nr, LANES), F32)
    seg_start = jnp.zeros((1, LANES), F32)
    tile_bucket = jnp.zeros((1, LANES), I32)
    pad_lo = jnp.zeros((1, LANES), F32)
    pad_hi = jnp.zeros((1, LANES), F32)
    for b in range(N_BUCKETS):
        m = bk == b
        pref = jnp.dot(jnp.where(m, 1.0, 0.0).astype(BF16), upper, preferred_element_type=F32)
        rowtot = jnp.broadcast_to(pref[:, LANES - 1:LANES], (nr, LANES))
        before = jnp.dot(before_rows, rowtot.astype(BF16), preferred_element_type=F32)
        cnt = jnp.sum(rowtot, axis=0, keepdims=True)
        pos = pos + jnp.where(m, seg_start + before + pref - 1.0, 0.0)
        pad_lo = jnp.where(lane == b, seg_start + cnt, pad_lo)
        seg_start = seg_start + jnp.ceil(cnt / tm) * tm
        pad_hi = jnp.where(lane == b, seg_start, pad_hi)
        tile_bucket = tile_bucket + (seg_start <= tile_start).astype(I32)
    pad_lo = jnp.where(lane == N_BUCKETS, seg_start, pad_lo)
    pad_hi = jnp.where(lane == N_BUCKETS, float(nt * tm), pad_hi)
    pos_ref[...] = pos.astype(I32)
    tb_ref[...] = tile_bucket
    lo_ref[...] = pad_lo.astype(I32)
    hi_ref[...] = pad_hi.astype(I32)


def _build_inverse(pos_ref, lo_ref, hi_ref, inv_ref):
    n_tok = pos_ref.shape[0]
    batch = 8
    assert n_tok % batch == 0

    def clear(r, v):
        inv_ref[r] = v
        return jnp.where(v + 1 == n_tok, 0, v + 1)

    def put(g, carry):
        base = g * batch
        dst = [pos_ref[base + j] for j in range(batch)]
        for j in range(batch):
            inv_ref[dst[j]] = base + j
        return carry

    for b in range(N_BUCKETS + 1):
        lax.fori_loop(lo_ref[b], hi_ref[b], clear, lax.rem(lo_ref[b], n_tok))
    lax.fori_loop(0, n_tok // batch, put, 0)


def _expert_kernel(telo_ref, tehi_ref, nused_ref, pos_ref, lo_ref, hi_ref, xs_hbm, wr_ref, rb_ref, wgl_ref, wul_ref,
                   wdl_ref, wgh_ref, wuh_ref, wdh_ref, y_ref, xbuf, wup, wdown, inv_ref, gsem, *, tm):
    r = pl.program_id(0)
    n_used = nused_ref[0]

    @pl.when(r == 0)
    def _():
        _build_inverse(pos_ref, lo_ref, hi_ref, inv_ref)

    slot = _gather_step(inv_ref, xs_hbm, xbuf, gsem, tm, n_steps=n_used)

    @pl.when(r < n_used)
    def _():
        prev = jnp.maximum(r - 1, 0)

        @pl.when(jnp.logical_or(r == 0, telo_ref[r] != telo_ref[prev]))
        def _():
            wup[0] = wgl_ref[0, 0].astype(BF16)
            wup[1] = wul_ref[0, 0].astype(BF16)
            wdown[0] = wdl_ref[0, 0].astype(BF16)

        @pl.when(jnp.logical_or(r == 0, tehi_ref[r] != tehi_ref[prev]))
        def _():
            wup[2] = wgh_ref[0, 0].astype(BF16)
            wup[3] = wuh_ref[0, 0].astype(BF16)
            wdown[1] = wdh_ref[0, 0].astype(BF16)

        x = _load_rows(xbuf.at[slot], tm).astype(BF16)
        logits = jnp.dot(x, wr_ref[...], preferred_element_type=F32) + rb_ref[...]
        lane = lax.broadcasted_iota(I32, logits.shape, 1)
        grp = telo_ref[r] // EXPERTS_PER_GROUP
        is_group = lane < N_GROUPS
        m = jnp.max(jnp.where(is_group, logits, -jnp.inf), axis=1, keepdims=True)
        ex = jnp.exp(logits - m)
        p_g = (jnp.sum(jnp.where(lane == grp, ex, 0.0), axis=1, keepdims=True)
               / jnp.sum(jnp.where(is_group, ex, 0.0), axis=1, keepdims=True))
        l_lo = jnp.sum(jnp.where(lane == EXPERT_ROW0 + telo_ref[r], logits, 0.0), axis=1, keepdims=True)
        l_hi = jnp.sum(jnp.where(lane == EXPERT_ROW0 + tehi_ref[r], logits, 0.0), axis=1, keepdims=True)
        mm = jnp.maximum(l_lo, l_hi)
        e_lo = jnp.exp(l_lo - mm)
        e_hi = jnp.exp(l_hi - mm)
        gates = (p_g * (e_lo / (e_lo + e_hi)), p_g * (e_hi / (e_lo + e_hi)))

        y = None
        for which in range(2):
            a = jnp.dot(x, wup[2 * which], preferred_element_type=F32)
            b = jnp.dot(x, wup[2 * which + 1], preferred_element_type=F32)
            hid = (jax.nn.silu(a) * b * gates[which]).astype(BF16)
            part = jnp.dot(hid, wdown[which], preferred_element_type=F32)
            y = part if y is None else y + part
        _store_rows(y_ref, y)

    @pl.when(r >= n_used)
    def _():
        y_ref[...] = jnp.zeros_like(y_ref)


def _head_rms(x, g):
    return jnp.concatenate([_rms(x[:, h * HEAD_DIM:(h + 1) * HEAD_DIM], g) for h in range(N_HEADS)], axis=1)


def _kvq_kernel(pos_ref, h_in_ref, y_hbm, kvg_ref, wkv_ref, kng_ref, bg_ref, wq_ref, qng_ref,
                h_ref, kp_ref, vp_ref, ksm_ref, vsm_ref, kb_ref, vb_ref, qb_ref, qsm_ref, ybuf, sem, *, npt):
    i = pl.program_id(0)
    tk = h_in_ref.shape[0]
    h2 = h_in_ref[...] + _load_rows(ybuf.at[_gather_step(pos_ref, y_hbm, ybuf, sem, tk)], tk)
    h_ref[...] = h2
    kv = jnp.dot(_rms(h2, kvg_ref[...]).astype(BF16), wkv_ref[...], preferred_element_type=F32)
    q = jnp.dot(_rms(h2, bg_ref[...]).astype(BF16), wq_ref[...], preferred_element_type=F32)
    k = _head_rms(kv[:, :D_MODEL], kng_ref[...])
    v = kv[:, D_MODEL:]
    q = _head_rms(q, qng_ref[...]) * (HEAD_DIM ** -0.5)

    @pl.when(i < npt)
    def _():
        _store_rows(kp_ref, k)
        _store_rows(vp_ref, v)
        for h in range(N_HEADS):
            sl = slice(h * HEAD_DIM, (h + 1) * HEAD_DIM)
            kb_ref[0, h] = k[:, sl].astype(BF16)
            vb_ref[0, h] = v[:, sl].astype(BF16)
            qb_ref[0, h] = q[:, sl].astype(BF16)

    @pl.when(i >= npt)
    def _():
        _store_rows(ksm_ref, k)
        _store_rows(vsm_ref, v)
        qsm_ref[...] = q


def _stick_blocks(qs, ks, vs, rs, visible, later_keys):
    n = len(qs)
    zs = [lax.dot_general(qs[h], ks[h], (((1,), (1,)), ((), ())), preferred_element_type=F32) for h in range(n)]
    sps = [jnp.maximum(z, 0.0) + jnp.log(1.0 + jnp.exp(-jnp.abs(z))) for z in zs]
    if visible is not None:
        sps = [jnp.where(visible, sp, 0.0) for sp in sps]
    his = [sp.astype(BF16) for sp in sps]
    los = [(sp - hi.astype(F32)).astype(BF16) for sp, hi in zip(sps, his)]
    cs = [jnp.dot(hi, later_keys, preferred_element_type=F32) + jnp.dot(lo, later_keys, preferred_element_type=F32)
          for hi, lo in zip(his, los)]
    ws = [jnp.exp(zs[h] - cs[h] - rs[h]) for h in range(n)]
    if visible is not None:
        ws = [jnp.where(visible, w, 0.0) for w in ws]
    outs = [jnp.dot(ws[h].astype(BF16), vs[h], preferred_element_type=F32) for h in range(n)]
    return outs, [rs[h] + cs[h][:, 0:1] for h in range(n)]


def _later_keys(n):
    rowi = lax.broadcasted_iota(I32, (n, n), 0)
    coli = lax.broadcasted_iota(I32, (n, n), 1)
    return (rowi >= coli).astype(BF16)


def _min_over(rs):
    m = rs[0]
    for r in rs[1:]:
        m = jnp.minimum(m, r)
    return jnp.min(m)


def _attn_kernel(q_ref, k_ref, v_ref, o_ref, r_sc, acc_sc):
    i = pl.program_id(1)
    tq = q_ref.shape[2]
    visible = lax.broadcasted_iota(I32, (tq, tq), 1) < lax.broadcasted_iota(I32, (tq, tq), 0)
    later_keys = _later_keys(tq)
    heads = range(N_HEADS)

    def block(j, first):
        start = pl.multiple_of(j * tq, tq)
        qs = [q_ref[0, h] for h in heads]
        ks = [k_ref[0, h, pl.ds(start, tq), :] for h in heads]
        vs = [v_ref[0, h, pl.ds(start, tq), :] for h in heads]
        rs = [jnp.zeros((tq, 1), F32) if first else r_sc[h] for h in heads]
        outs, rs = _stick_blocks(qs, ks, vs, rs, visible if first else None, later_keys)
        acc = jnp.concatenate(outs, axis=1)
        if first:
            acc_sc[...] = acc
        else:
            acc_sc[...] += acc
        for h in heads:
            r_sc[h] = rs[h]
        return _min_over(rs)

    rmin = block(i, True)
    lax.while_loop(lambda st: jnp.logical_and(st[0] >= 0, st[1] <= STOP_MASS),
                   lambda st: (st[0] - 1, block(st[0], False)), (i - 1, rmin))
    o_ref[...] = acc_sc[...].astype(o_ref.dtype)


def _sample_attn_kernel(q_ref, kn_ref, vn_ref, ck_ref, cv_ref, ck_hbm, cv_hbm, o_ref, r_sc, acc_sc, kbuf, vbuf, sem,
                        *, nkc):
    b = pl.program_id(0)
    ds = q_ref.shape[0]
    ck = ck_ref.shape[0] // N_HEADS
    heads = range(N_HEADS)

    def head_rows(ref, h, n):
        return ref[pl.ds(h, n, stride=N_HEADS), :].astype(BF16)

    qs = [q_ref[:, h * HEAD_DIM:(h + 1) * HEAD_DIM].astype(BF16) for h in heads]
    visible = lax.broadcasted_iota(I32, (ds, LANES), 1) < lax.broadcasted_iota(I32, (ds, LANES), 0)
    pad = jnp.zeros((LANES - ds, HEAD_DIM), BF16)
    outs, rs = _stick_blocks(qs, [jnp.concatenate([head_rows(kn_ref, h, ds), pad], axis=0) for h in heads],
                             [jnp.concatenate([head_rows(vn_ref, h, ds), pad], axis=0) for h in heads],
                             [jnp.zeros((ds, 1), F32)] * N_HEADS, visible, _later_keys(LANES))
    acc_sc[...] = jnp.concatenate(outs, axis=1)
    for h in heads:
        r_sc[h] = rs[h]

    def chunk(kref, vref):
        outs, rs = _stick_blocks(qs, [head_rows(kref, h, ck) for h in heads], [head_rows(vref, h, ck) for h in heads],
                                 [r_sc[h] for h in heads], None, _later_keys(ck))
        acc_sc[...] += jnp.concatenate(outs, axis=1)
        for h in heads:
            r_sc[h] = rs[h]
        return _min_over(rs)

    def older(st):
        c = st[0]
        rows = ck * N_HEADS
        start = pl.multiple_of((b * nkc + c) * rows, rows)
        cpk = pltpu.make_async_copy(ck_hbm.at[pl.ds(start, rows)], kbuf, sem.at[0])
        cpv = pltpu.make_async_copy(cv_hbm.at[pl.ds(start, rows)], vbuf, sem.at[1])
        cpk.start()
        cpv.start()
        cpk.wait()
        cpv.wait()
        return c - 1, chunk(kbuf, vbuf)

    lax.while_loop(lambda st: jnp.logical_and(st[0] >= 0, st[1] <= STOP_MASS), older,
                   (nkc - 2, chunk(ck_ref, cv_ref)))
    o_ref[...] = acc_sc[...].astype(o_ref.dtype)


def _final_kernel(pos_ref, h_ref, y_hbm, yp_ref, ysm_ref, ybuf, sem, *, npt):
    i = pl.program_id(0)
    tk = h_ref.shape[0]
    y = h_ref[...] + _load_rows(ybuf.at[_gather_step(pos_ref, y_hbm, ybuf, sem, tk)], tk)

    @pl.when(i < npt)
    def _():
        yp_ref[...] = y

    @pl.when(i >= npt)
    def _():
        ysm_ref[...] = y


def _const_spec(shape):
    return pl.BlockSpec(shape, lambda *_: (0,) * len(shape))


def _router_operands(layer, ffn_norm_g, router_g_w, router_g_b, router_e_w, router_e_b):
    n_e = N_GROUPS * EXPERTS_PER_GROUP
    wrt = jnp.zeros((ROUTER_ROWS, D_MODEL), F32)
    wrt = wrt.at[:N_GROUPS].set(router_g_w[layer].T)
    wrt = wrt.at[EXPERT_ROW0:EXPERT_ROW0 + n_e].set(router_e_w[layer].transpose(0, 2, 1).reshape(n_e, D_MODEL))
    rb = jnp.zeros((ROUTER_ROWS,), F32)
    rb = rb.at[:N_GROUPS].set(router_g_b[layer])
    rb = rb.at[EXPERT_ROW0:EXPERT_ROW0 + n_e].set(router_e_b[layer].reshape(-1))
    wr_cols = jnp.zeros((D_MODEL, LANES), F32).at[:, :ROUTER_ROWS].set(wrt.T).astype(BF16)
    rb_cols = jnp.zeros((1, LANES), F32).at[0, :ROUTER_ROWS].set(rb)
    return (ffn_norm_g[layer][None, :], wrt, jnp.broadcast_to(rb[:, None], (ROUTER_ROWS, LANES))), (wr_cols, rb_cols)


def _moe(xs_rt, bucket, layer, route_cols, w_gate, w_up, w_down):
    nblk = bucket.shape[0]
    n_tok = nblk * LANES
    tm = EXPERT_TILE
    nr = -(-nblk // LANES) * LANES
    nt = -(-(n_tok + N_BUCKETS * (tm - 1)) // tm)
    assert nt <= LANES
    bk = jnp.pad(bucket.reshape(nblk, LANES), ((0, nr - nblk), (0, 0)), constant_values=N_BUCKETS)
    lane_row = jax.ShapeDtypeStruct((1, LANES), I32)
    pos, tile_bucket, pad_lo, pad_hi = pl.pallas_call(
        functools.partial(_rank_kernel, tm=tm, nt=nt),
        out_shape=(jax.ShapeDtypeStruct((nr, LANES), I32), lane_row, lane_row, lane_row),
        name=f"moe_rank_{layer}",
    )(bk)
    pos = pos[:nblk].reshape(n_tok)
    tb = tile_bucket[0, :nt]
    n_used = jnp.sum((tb < N_BUCKETS).astype(I32))
    tsrc = jnp.minimum(jnp.arange(nt, dtype=I32), n_used - 1)
    tbc = jnp.minimum(tb[tsrc], N_BUCKETS - 1)
    pair_lo = jnp.array([p[0] for p in PAIRS], I32)
    pair_hi = jnp.array([p[1] for p in PAIRS], I32)
    grp = tbc // len(PAIRS)
    telo = grp * EXPERTS_PER_GROUP + pair_lo[tbc % len(PAIRS)]
    tehi = grp * EXPERTS_PER_GROUP + pair_hi[tbc % len(PAIRS)]

    def wspec(shape, which):
        return pl.BlockSpec((1, 1) + shape, lambda r, lo, hi, *_: (layer, (lo, hi)[which][r], 0, 0))

    up_shape = (D_MODEL, EXPERT_DIM)
    down_shape = (EXPERT_DIM, D_MODEL)
    wr_cols, rb_cols = route_cols
    tile_shape = (tm * SLABS, LANES)
    y_sorted = pl.pallas_call(
        functools.partial(_expert_kernel, tm=tm),
        out_shape=jax.ShapeDtypeStruct((nt * tm * SLABS, LANES), F32),
        grid_spec=pltpu.PrefetchScalarGridSpec(
            num_scalar_prefetch=6, grid=(nt,),
            in_specs=[pl.BlockSpec(memory_space=pl.ANY),
                      pl.BlockSpec((D_MODEL, LANES), lambda r, *_: (0, 0)),
                      pl.BlockSpec((1, LANES), lambda r, *_: (0, 0)),
                      wspec(up_shape, 0), wspec(up_shape, 0), wspec(down_shape, 0),
                      wspec(up_shape, 1), wspec(up_shape, 1), wspec(down_shape, 1)],
            out_specs=pl.BlockSpec(tile_shape, lambda r, *_: (r, 0)),
            scratch_shapes=[pltpu.VMEM((2,) + tile_shape, F32), pltpu.VMEM((4,) + up_shape, BF16),
                            pltpu.VMEM((2,) + down_shape, BF16), pltpu.SMEM((nt * tm,), I32),
                            pltpu.SemaphoreType.DMA((2,))]),
        compiler_params=_cparams("arbitrary"),
        name=f"moe_experts_{layer}",
    )(telo, tehi, n_used.reshape(1), pos, pad_lo[0], pad_hi[0], xs_rt, wr_cols, rb_cols,
      w_gate, w_up, w_down, w_gate, w_up, w_down)
    return y_sorted, pos


def kernel(x_prompt, x_sample, cache_k, cache_v, a_norm_g, a_w_in, a_v_norm_g, a_w_s, a_b_s, a_w_out, kv_norm_g, w_kv, k_norm_g, b_norm_g, b_w_q, q_norm_g, b_w_o, ffn_norm_g, router_g_w, router_g_b, router_e_w, router_e_b, moe_w_gate, moe_w_up, moe_w_down):
    bsz, seq, _ = x_prompt.shape
    dbsz, dseq, _ = x_sample.shape
    past = cache_k.shape[1]
    tp, ts = bsz * seq, dbsz * dseq
    tk = STREAM_BLOCK
    tsp = -(-ts // tk) * tk
    n_tok = tp + tsp
    tb = MIXER_BLOCK
    gb = GMLP_BLOCK
    assert seq % ATTN_TILE == 0 and tp % tk == 0 and tk % tb == 0 and tk % ATTN_TILE == 0
    assert gb % dseq == 0 and dseq <= CHUNK and past % ATTN_TILE == 0
    nps, nss = tp // tb, tsp // tb
    npt, nst = tp // tk, tsp // tk
    xp = x_prompt.reshape(tp, D_MODEL)
    xsm = jnp.pad(x_sample.reshape(ts, D_MODEL), ((0, tsp - ts), (0, 0)))

    def rt_shape(n):
        return jax.ShapeDtypeStruct((n * SLABS, LANES), F32)

    def rt_spec(rows, index):
        return pl.BlockSpec((rows * SLABS, LANES), lambda i, *_: (index(i), 0))

    def route_specs(rows, n_prompt_steps):
        def pstep(i):
            return jnp.minimum(i, n_prompt_steps - 1)

        def sstep(i):
            return jnp.maximum(i - n_prompt_steps, 0)

        out_specs = [pl.BlockSpec((rows, D_MODEL), lambda i: (i, 0)), rt_spec(rows, lambda i: i),
                     pl.BlockSpec((rows // LANES, 1, LANES), lambda i: (i, 0, 0))]
        return pstep, sstep, out_specs

    route_in_specs = [_const_spec((1, D_MODEL)), _const_spec((ROUTER_ROWS, D_MODEL)),
                      _const_spec((ROUTER_ROWS, LANES))]
    route_out_shapes = (jax.ShapeDtypeStruct((n_tok, D_MODEL), F32), rt_shape(n_tok),
                        jax.ShapeDtypeStruct((n_tok // LANES, 1, LANES), I32))
    pstep, sstep, route_out_specs = route_specs(tb, nps)

    pos_i = jnp.arange(gb)
    mask = (pos_i[None, :] // CHUNK) <= (pos_i[:, None] // CHUNK)
    w_prompt = jnp.where(mask, a_w_s[0], 0.0)
    rep = gb // dseq
    w_sample = jnp.einsum("ij,hts->hitjs", jnp.eye(rep, dtype=F32), w_prompt[:, :dseq, :dseq]).reshape(
        A_HEADS, gb, gb)
    ws_all = jnp.stack([w_prompt, w_sample]).astype(BF16)
    b_prompt = a_b_s[0]
    b_sample = jnp.tile(a_b_s[0][:, :dseq], (1, rep))
    bs_all = jnp.stack([jnp.repeat(b.T, A_HEAD_DIM, axis=1) for b in (b_prompt, b_sample)])
    route0, cols0 = _router_operands(0, ffn_norm_g, router_g_w, router_g_b, router_e_w, router_e_b)
    h1, xs1, bucket1, v_rows = pl.pallas_call(
        functools.partial(_a_layer_kernel, nps=nps),
        out_shape=route_out_shapes + (jax.ShapeDtypeStruct((tsp, A_WIDTH), F32),),
        grid=(nps + nss,),
        in_specs=[pl.BlockSpec((tb, D_MODEL), lambda i: (pstep(i), 0)),
                  pl.BlockSpec((tb, D_MODEL), lambda i: (sstep(i), 0)),
                  _const_spec((1, D_MODEL)), _const_spec((D_MODEL, 2 * A_WIDTH)), _const_spec((1, A_WIDTH)),
                  pl.BlockSpec((1, A_HEADS, gb, gb), lambda i: (i // nps, 0, 0, 0)),
                  pl.BlockSpec((1, gb, A_WIDTH), lambda i: (i // nps, 0, 0)),
                  _const_spec((A_WIDTH, D_MODEL))] + route_in_specs,
        out_specs=route_out_specs + [pl.BlockSpec((tb, A_WIDTH), lambda i: (sstep(i), 0))],
        compiler_params=_cparams("arbitrary"),
        name="a_layer",
    )(xp, xsm, a_norm_g[0][None, :], a_w_in[0].astype(BF16), a_v_norm_g[0][None, :], ws_all, bs_all,
      a_w_out[0].astype(BF16), *route0)
    y1, pos1 = _moe(xs1, bucket1, 0, cols0, moe_w_gate, moe_w_up, moe_w_down)

    spt = seq // tk

    def ptile(i):
        return jnp.minimum(i, npt - 1)

    def stile(i):
        return jnp.maximum(i - npt, 0)

    row_spec = pl.BlockSpec((tk, D_MODEL), lambda i, *_: (i, 0))
    prow_spec = pl.BlockSpec((tk, D_MODEL), lambda i, *_: (ptile(i), 0))
    srow_spec = pl.BlockSpec((tk, D_MODEL), lambda i, *_: (stile(i), 0))
    head_spec = pl.BlockSpec((1, N_HEADS, tk, HEAD_DIM), lambda i, *_: (ptile(i) // spt, 0, ptile(i) % spt, 0))
    any_spec = pl.BlockSpec(memory_space=pl.ANY)
    gather_scratch = [pltpu.VMEM((2, tk * SLABS, LANES), F32), pltpu.SemaphoreType.DMA((2,))]
    f32_rows = lambda n: jax.ShapeDtypeStruct((n, D_MODEL), F32)
    head_major = jax.ShapeDtypeStruct((bsz, N_HEADS, seq, HEAD_DIM), BF16)
    h2, k_p, v_p, k_s, v_s, kb, vb, qb, q_s = pl.pallas_call(
        functools.partial(_kvq_kernel, npt=npt),
        out_shape=(f32_rows(n_tok), rt_shape(tp), rt_shape(tp), rt_shape(tsp), rt_shape(tsp),
                   head_major, head_major, head_major, f32_rows(tsp)),
        grid_spec=pltpu.PrefetchScalarGridSpec(
            num_scalar_prefetch=1, grid=(npt + nst,),
            in_specs=[row_spec, any_spec, _const_spec((1, D_MODEL)),
                      _const_spec((D_MODEL, 2 * D_MODEL)), _const_spec((1, HEAD_DIM)), _const_spec((1, D_MODEL)),
                      _const_spec((D_MODEL, D_MODEL)), _const_spec((1, HEAD_DIM))],
            out_specs=[row_spec, rt_spec(tk, ptile), rt_spec(tk, ptile), rt_spec(tk, stile), rt_spec(tk, stile),
                       head_spec, head_spec, head_spec, srow_spec],
            scratch_shapes=gather_scratch),
        compiler_params=_cparams("arbitrary"),
        name="kvq",
    )(pos1, h1, y1, kv_norm_g[None, :], w_kv.astype(BF16), k_norm_g[None, :], b_norm_g[0][None, :],
      b_w_q[0].astype(BF16), q_norm_g[0][None, :])

    tq = ATTN_TILE
    nq = seq // tq
    qspec = pl.BlockSpec((1, N_HEADS, tq, HEAD_DIM), lambda b, i: (b, 0, i, 0))
    kvspec = pl.BlockSpec((1, N_HEADS, seq, HEAD_DIM), lambda b, i: (b, 0, 0, 0), pipeline_mode=pl.Buffered(1))
    o_p = pl.pallas_call(
        _attn_kernel,
        out_shape=jax.ShapeDtypeStruct((tp, D_MODEL), BF16),
        grid=(bsz, nq),
        in_specs=[qspec, kvspec, kvspec],
        out_specs=pl.BlockSpec((tq, D_MODEL), lambda b, i: (b * nq + i, 0)),
        scratch_shapes=[pltpu.VMEM((N_HEADS, tq, 1), F32), pltpu.VMEM((tq, D_MODEL), F32)],
        compiler_params=_cparams("arbitrary", "arbitrary"),
        name="attn_prompt",
    )(qb, kb, vb)

    ck = ATTN_TILE
    nkc = past // ck
    new_spec = pl.BlockSpec((dseq, D_MODEL), lambda b: (b, 0))
    new_rt_spec = rt_spec(dseq, lambda b: b)
    newest_spec = rt_spec(ck, lambda b: b * nkc + nkc - 1)
    cache_k2 = cache_k.reshape(dbsz * past * N_HEADS, HEAD_DIM)
    cache_v2 = cache_v.reshape(dbsz * past * N_HEADS, HEAD_DIM)
    chunk_buf = pltpu.VMEM((ck * N_HEADS, HEAD_DIM), F32)
    o_s = pl.pallas_call(
        functools.partial(_sample_attn_kernel, nkc=nkc),
        out_shape=jax.ShapeDtypeStruct((ts, D_MODEL), BF16),
        grid=(dbsz,),
        in_specs=[new_spec, new_rt_spec, new_rt_spec, newest_spec, newest_spec, any_spec, any_spec],
        out_specs=new_spec,
        scratch_shapes=[pltpu.VMEM((N_HEADS, dseq, 1), F32), pltpu.VMEM((dseq, D_MODEL), F32), chunk_buf, chunk_buf,
                        pltpu.SemaphoreType.DMA((2,))],
        compiler_params=_cparams("arbitrary"),
        name="attn_sample",
    )(q_s, k_s, v_s, cache_k2, cache_v2, cache_k2, cache_v2)
    o_s = jnp.pad(o_s, ((0, tsp - ts), (0, 0)))

    route1, cols1 = _router_operands(1, ffn_norm_g, router_g_w, router_g_b, router_e_w, router_e_b)
    pstep, sstep, route_out_specs = route_specs(tk, npt)
    h3, xs2, bucket2 = pl.pallas_call(
        functools.partial(_oproj_kernel, nps=npt),
        out_shape=route_out_shapes,
        grid=(npt + nst,),
        in_specs=[pl.BlockSpec((tk, D_MODEL), lambda i: (pstep(i), 0)),
                  pl.BlockSpec((tk, D_MODEL), lambda i: (sstep(i), 0)),
                  pl.BlockSpec((tk, D_MODEL), lambda i: (i, 0)), _const_spec((D_MODEL, D_MODEL))] + route_in_specs,
        out_specs=route_out_specs,
        compiler_params=_cparams("arbitrary"),
        name="o_proj",
    )(o_p, o_s, h2, b_w_o[0].astype(BF16), *route1)
    y2, pos2 = _moe(xs2, bucket2, 1, cols1, moe_w_gate, moe_w_up, moe_w_down)

    y_p, y_s = pl.pallas_call(
        functools.partial(_final_kernel, npt=npt),
        out_shape=(f32_rows(tp), f32_rows(tsp)),
        grid_spec=pltpu.PrefetchScalarGridSpec(
            num_scalar_prefetch=1, grid=(npt + nst,),
            in_specs=[row_spec, any_spec],
            out_specs=[prow_spec, srow_spec],
            scratch_shapes=gather_scratch),
        compiler_params=_cparams("arbitrary"),
        name="final_residual",
    )(pos2, h3, y2)

    kv_shape_p = (bsz, seq, N_HEADS, HEAD_DIM)
    kv_shape_s = (dbsz, dseq, N_HEADS, HEAD_DIM)
    k_s = k_s[:ts * SLABS].reshape(kv_shape_s)
    v_s = v_s[:ts * SLABS].reshape(kv_shape_s)
    return (y_p.reshape(bsz, seq, D_MODEL), y_s[:ts].reshape(dbsz, dseq, D_MODEL),
            k_p.reshape(kv_shape_p), v_p.reshape(kv_shape_p), k_s, v_s,
            v_rows[:ts].reshape(1, dbsz, dseq, A_WIDTH))
```

```python
import functools

import jax
import jax.numpy as jnp
from jax import lax
from jax.experimental import pallas as pl
from jax.experimental.pallas import tpu as pltpu

F32 = jnp.float32
BF16 = jnp.bfloat16
I32 = jnp.int32

D_MODEL = 1024
CHUNK = 64
GMLP_BLOCK = 128
A_WIDTH = 2 * D_MODEL
A_HEADS = 8
A_HEAD_DIM = A_WIDTH // A_HEADS
N_HEADS = 8
HEAD_DIM = D_MODEL // N_HEADS
N_GROUPS = 4
EXPERTS_PER_GROUP = 4
EXPERT_DIM = D_MODEL // 2
EPS = 1e-6

LANES = 128
SUBLANES = 8
MXU_WIDTH = 256
SLABS = D_MODEL // LANES
assert SLABS == SUBLANES and N_HEADS == SLABS and HEAD_DIM == LANES
PAIRS = ((0, 1), (0, 2), (0, 3), (1, 2), (1, 3), (2, 3))
N_BUCKETS = N_GROUPS * len(PAIRS)
ROUTER_ROWS = 32
EXPERT_ROW0 = 8

MIXER_BLOCK = 4 * GMLP_BLOCK
STREAM_BLOCK = 512
EXPERT_TILE = 256
ATTN_TILE = 256
VMEM_LIMIT = 56 * 1024 * 1024
STOP_MASS = 105.0


def _cparams(*sem):
    return pltpu.CompilerParams(dimension_semantics=sem, vmem_limit_bytes=VMEM_LIMIT)


def _rms(x, g):
    ms = jnp.mean(x * x, axis=-1, keepdims=True)
    return x * lax.rsqrt(ms + EPS) * g


def _load_rows(ref, n):
    return jnp.concatenate([ref[pl.ds(c, n, stride=SLABS), :] for c in range(SLABS)], axis=1)


def _store_rows(ref, x):
    n = x.shape[0]
    for c in range(SLABS):
        ref[pl.ds(c, n, stride=SLABS), :] = x[:, c * LANES:(c + 1) * LANES]


def _row_copy(src_hbm, src_row, buf, slot, k, sem):
    src = src_row * SLABS if isinstance(src_row, int) else pl.multiple_of(src_row * SLABS, SLABS)
    return pltpu.make_async_copy(src_hbm.at[pl.ds(src, SLABS)], buf.at[slot, pl.ds(k * SLABS, SLABS)], sem.at[slot])


def _gather_step(idx_ref, src_hbm, buf, sem, rows, n_steps=None):
    i = pl.program_id(0)
    slot = i % 2
    if n_steps is None:
        n_steps = pl.num_programs(0)

    @pl.when(i == 0)
    def _():
        for k in range(rows):
            _row_copy(src_hbm, idx_ref[k], buf, 0, k, sem).start(priority=k % 2)

    @pl.when(i + 1 < n_steps)
    def _():
        for k in range(rows):
            _row_copy(src_hbm, idx_ref[(i + 1) * rows + k], buf, 1 - slot, k, sem).start(priority=k % 2)

    @pl.when(i < n_steps)
    def _():
        for k in range(rows):
            _row_copy(src_hbm, 0, buf, slot, k, sem).wait()

    return slot


def _first_argmax(vals):
    m = vals[0]
    for v in vals[1:]:
        m = jnp.maximum(m, v)
    idx = jnp.full(m.shape, len(vals) - 1, I32)
    for r in range(len(vals) - 2, -1, -1):
        idx = jnp.where(vals[r] == m, r, idx)
    return m, idx


def _route(xn, wrt_ref, rb_ref, xs_ref, bucket_ref):
    lt = lax.dot_general(wrt_ref[...], xn, (((1,), (1,)), ((), ())),
                         precision=lax.Precision.HIGHEST, preferred_element_type=F32) + rb_ref[:, 0:1]
    _, g_idx = _first_argmax([lt[r:r + 1, :] for r in range(N_GROUPS)])
    le = []
    for e in range(EXPERTS_PER_GROUP):
        row = EXPERT_ROW0 + EXPERTS_PER_GROUP * (N_GROUPS - 1) + e
        v = lt[row:row + 1, :]
        for g in range(N_GROUPS - 2, -1, -1):
            row = EXPERT_ROW0 + EXPERTS_PER_GROUP * g + e
            v = jnp.where(g_idx == g, lt[row:row + 1, :], v)
        le.append(v)
    _, i1 = _first_argmax(le)
    _, i2 = _first_argmax([jnp.where(i1 == e, -jnp.inf, le[e]) for e in range(EXPERTS_PER_GROUP)])
    lo = jnp.minimum(i1, i2)
    hi = jnp.maximum(i1, i2)
    pair = jnp.where(lo == 0, hi - 1, jnp.where(lo == 1, hi + 1, 5))
    bucket = g_idx * len(PAIRS) + pair
    for sb in range(xn.shape[0] // LANES):
        bucket_ref[sb] = bucket[:, sb * LANES:(sb + 1) * LANES]
    _store_rows(xs_ref, xn)


def _a_layer_kernel(xp_ref, xsm_ref, ag_ref, win_ref, vg_ref, ws_ref, bs_ref, wout_ref, fg_ref, wrt_ref, rb_ref,
                    h_ref, xs_ref, bucket_ref, vrow_ref, *, nps):
    i = pl.program_id(0)
    x = jnp.where(i < nps, xp_ref[...], xsm_ref[...])
    xn = _rms(x, ag_ref[...]).astype(BF16)
    z = jax.nn.gelu(jnp.dot(xn, win_ref[...], preferred_element_type=F32))
    u = z[:, :A_WIDTH]
    v = _rms(z[:, A_WIDTH:], vg_ref[...])

    @pl.when(i >= nps)
    def _():
        vrow_ref[...] = v

    vb = v.astype(BF16)
    gate = jnp.concatenate([
        jnp.concatenate(
            [jnp.dot(ws_ref[0, h], vb[sb * GMLP_BLOCK:(sb + 1) * GMLP_BLOCK, h * A_HEAD_DIM:(h + 1) * A_HEAD_DIM],
                     preferred_element_type=F32) for h in range(A_HEADS)], axis=1) + bs_ref[0]
        for sb in range(x.shape[0] // GMLP_BLOCK)], axis=0)
    s = (u * gate).astype(BF16)
    h1 = x + jnp.dot(s, wout_ref[...], preferred_element_type=F32)
    h_ref[...] = h1
    _route(_rms(h1, fg_ref[...]), wrt_ref, rb_ref, xs_ref, bucket_ref)


def _oproj_kernel(op_ref, osm_ref, h_in_ref, wo_ref, fg_ref, wrt_ref, rb_ref, h_ref, xs_ref, bucket_ref, *, nps):
    i = pl.program_id(0)
    o = jnp.where(i < nps, op_ref[...], osm_ref[...])
    h3 = h_in_ref[...] + jnp.dot(o, wo_ref[...], preferred_element_type=F32)
    h_ref[...] = h3
    _route(_rms(h3, fg_ref[...]), wrt_ref, rb_ref, xs_ref, bucket_ref)


def _rank_kernel(b_ref, pos_ref, tb_ref, lo_ref, hi_ref, *, tm, nt):
    bk = b_ref[...]
    nr = bk.shape[0]
    r_i = lax.broadcasted_iota(I32, (LANES, LANES), 0)
    c_i = lax.broadcasted_iota(I32, (LANES, LANES), 1)
    upper = (r_i <= c_i).astype(BF16)
    rr = lax.broadcasted_iota(I32, (nr, nr), 0)
    cc = lax.broadcasted_iota(I32, (nr, nr), 1)
    before_rows = (cc < rr).astype(BF16)
    lane = lax.broadcasted_iota(I32, (1, LANES), 1)
    tile_start = lane.astype(F32) * tm
    pos = jnp.zeros((nr, LANES), F32)
    seg_start = jnp.zeros((1, LANES), F32)
    tile_bucket = jnp.zeros((1, LANES), I32)
    pad_lo = jnp.zeros((1, LANES), F32)
    pad_hi = jnp.zeros((1, LANES), F32)
    for b in range(N_BUCKETS):
        m = bk == b
        pref = jnp.dot(jnp.where(m, 1.0, 0.0).astype(BF16), upper, preferred_element_type=F32)
        rowtot = jnp.broadcast_to(pref[:, LANES - 1:LANES], (nr, LANES))
        before = jnp.dot(before_rows, rowtot.astype(BF16), preferred_element_type=F32)
        cnt = jnp.sum(rowtot, axis=0, keepdims=True)
        pos = pos + jnp.where(m, seg_start + before + pref - 1.0, 0.0)
        pad_lo = jnp.where(lane == b, seg_start + cnt, pad_lo)
        seg_start = seg_start + jnp.ceil(cnt / tm) * tm
        pad_hi = jnp.where(lane == b, seg_start, pad_hi)
        tile_bucket = tile_bucket + (seg_start <= tile_start).astype(I32)
    pad_lo = jnp.where(lane == N_BUCKETS, seg_start, pad_lo)
    pad_hi = jnp.where(lane == N_BUCKETS, float(nt * tm), pad_hi)
    pos_ref[...] = pos.astype(I32)
    tb_ref[...] = tile_bucket
    lo_ref[...] = pad_lo.astype(I32)
    hi_ref[...] = pad_hi.astype(I32)


def _build_inverse(pos_ref, lo_ref, hi_ref, inv_ref):
    n_tok = pos_ref.shape[0]
    batch = 8
    assert n_tok % batch == 0

    def clear(r, v):
        inv_ref[r] = v
        return jnp.where(v + 1 == n_tok, 0, v + 1)

    def put(g, carry):
        base = g * batch
        dst = [pos_ref[base + j] for j in range(batch)]
        for j in range(batch):
            inv_ref[dst[j]] = base + j
        return carry

    for b in range(N_BUCKETS + 1):
        lax.fori_loop(lo_ref[b], hi_ref[b], clear, lax.rem(lo_ref[b], n_tok))
    lax.fori_loop(0, n_tok // batch, put, 0)


def _expert_kernel(telo_ref, tehi_ref, nused_ref, pos_ref, lo_ref, hi_ref, xs_hbm, wr_ref, rb_ref, wgl_ref, wul_ref,
                   wdl_ref, wgh_ref, wuh_ref, wdh_ref, y_ref, xbuf, wup, wdown, inv_ref, gsem, *, tm):
    r = pl.program_id(0)
    n_used = nused_ref[0]

    @pl.when(r == 0)
    def _():
        _build_inverse(pos_ref, lo_ref, hi_ref, inv_ref)

    slot = _gather_step(inv_ref, xs_hbm, xbuf, gsem, tm, n_steps=n_used)

    @pl.when(r < n_used)
    def _():
        prev = jnp.maximum(r - 1, 0)

        @pl.when(jnp.logical_or(r == 0, telo_ref[r] != telo_ref[prev]))
        def _():
            wup[0] = wgl_ref[0, 0].astype(BF16)
            wup[1] = wul_ref[0, 0].astype(BF16)
            wdown[0] = wdl_ref[0, 0].astype(BF16)

        @pl.when(jnp.logical_or(r == 0, tehi_ref[r] != tehi_ref[prev]))
        def _():
            wup[2] = wgh_ref[0, 0].astype(BF16)
            wup[3] = wuh_ref[0, 0].astype(BF16)
            wdown[1] = wdh_ref[0, 0].astype(BF16)

        x = _load_rows(xbuf.at[slot], tm).astype(BF16)
        logits = jnp.dot(x, wr_ref[...], preferred_element_type=F32) + rb_ref[...]
        lane = lax.broadcasted_iota(I32, logits.shape, 1)
        grp = telo_ref[r] // EXPERTS_PER_GROUP
        is_group = lane < N_GROUPS
        m = jnp.max(jnp.where(is_group, logits, -jnp.inf), axis=1, keepdims=True)
        ex = jnp.exp(logits - m)
        p_g = (jnp.sum(jnp.where(lane == grp, ex, 0.0), axis=1, keepdims=True)
               / jnp.sum(jnp.where(is_group, ex, 0.0), axis=1, keepdims=True))
        l_lo = jnp.sum(jnp.where(lane == EXPERT_ROW0 + telo_ref[r], logits, 0.0), axis=1, keepdims=True)
        l_hi = jnp.sum(jnp.where(lane == EXPERT_ROW0 + tehi_ref[r], logits, 0.0), axis=1, keepdims=True)
        mm = jnp.maximum(l_lo, l_hi)
        e_lo = jnp.exp(l_lo - mm)
        e_hi = jnp.exp(l_hi - mm)
        gates = (p_g * (e_lo / (e_lo + e_hi)), p_g * (e_hi / (e_lo + e_hi)))

        y = None
        for which in range(2):
            a = jnp.dot(x, wup[2 * which], preferred_element_type=F32)
            b = jnp.dot(x, wup[2 * which + 1], preferred_element_type=F32)
            hid = (jax.nn.silu(a) * b * gates[which]).astype(BF16)
            part = jnp.dot(hid, wdown[which], preferred_element_type=F32)
            y = part if y is None else y + part
        _store_rows(y_ref, y)

    @pl.when(r >= n_used)
    def _():
        y_ref[...] = jnp.zeros_like(y_ref)


def _head_rms(x, g):
    return jnp.concatenate([_rms(x[:, h * HEAD_DIM:(h + 1) * HEAD_DIM], g) for h in range(N_HEADS)], axis=1)


def _kvq_kernel(pos_ref, h_in_ref, y_hbm, kvg_ref, wkv_ref, kng_ref, bg_ref, wq_ref, qng_ref,
                h_ref, kp_ref, vp_ref, ksm_ref, vsm_ref, kb_ref, vb_ref, qb_ref, qsm_ref, ybuf, sem, *, npt):
    i = pl.program_id(0)
    tk = h_in_ref.shape[0]
    h2 = h_in_ref[...] + _load_rows(ybuf.at[_gather_step(pos_ref, y_hbm, ybuf, sem, tk)], tk)
    h_ref[...] = h2
    hn = h2 * lax.rsqrt(jnp.mean(h2 * h2, axis=-1, keepdims=True) + EPS)
    kv = jnp.dot((hn * kvg_ref[...]).astype(BF16), wkv_ref[...], preferred_element_type=F32)
    q = jnp.dot((hn * bg_ref[...]).astype(BF16), wq_ref[...], preferred_element_type=F32)
    k = _head_rms(kv[:, :D_MODEL], kng_ref[...])
    v = kv[:, D_MODEL:]
    q = _head_rms(q, qng_ref[...]) * (HEAD_DIM ** -0.5)

    @pl.when(i < npt)
    def _():
        _store_rows(kp_ref, k)
        _store_rows(vp_ref, v)
        for h in range(N_HEADS):
            sl = slice(h * HEAD_DIM, (h + 1) * HEAD_DIM)
            kb_ref[0, h] = k[:, sl].astype(BF16)
            vb_ref[0, h] = v[:, sl].astype(BF16)
            qb_ref[0, h] = q[:, sl].astype(BF16)

    @pl.when(i >= npt)
    def _():
        _store_rows(ksm_ref, k)
        _store_rows(vsm_ref, v)
        qsm_ref[...] = q


def _stick_blocks(qs, ks, vs, rs, visible, later_keys):
    n = len(qs)
    zs = [lax.dot_general(qs[h], ks[h], (((1,), (1,)), ((), ())), preferred_element_type=F32) for h in range(n)]
    sps = [jnp.maximum(z, 0.0) + jnp.log(1.0 + jnp.exp(-jnp.abs(z))) for z in zs]
    if visible is not None:
        sps = [jnp.where(visible, sp, 0.0) for sp in sps]
    his = [sp.astype(BF16) for sp in sps]
    los = [(sp - hi.astype(F32)).astype(BF16) for sp, hi in zip(sps, his)]
    cs = [jnp.dot(hi, later_keys, preferred_element_type=F32) + jnp.dot(lo, later_keys, preferred_element_type=F32)
          for hi, lo in zip(his, los)]
    ws = [jnp.exp(zs[h] - cs[h] - rs[h]) for h in range(n)]
    if visible is not None:
        ws = [jnp.where(visible, w, 0.0) for w in ws]
    outs = [jnp.dot(ws[h].astype(BF16), vs[h], preferred_element_type=F32) for h in range(n)]
    return outs, [rs[h] + cs[h][:, 0:1] for h in range(n)]


def _later_keys(n):
    rowi = lax.broadcasted_iota(I32, (n, n), 0)
    coli = lax.broadcasted_iota(I32, (n, n), 1)
    return (rowi >= coli).astype(BF16)


def _min_over(rs):
    m = rs[0]
    for r in rs[1:]:
        m = jnp.minimum(m, r)
    return jnp.min(m)


def _attn_kernel(q_ref, k_ref, v_ref, o_ref, r_sc, acc_sc):
    i = pl.program_id(1)
    tq = q_ref.shape[2]
    visible = lax.broadcasted_iota(I32, (tq, tq), 1) < lax.broadcasted_iota(I32, (tq, tq), 0)
    later_keys = _later_keys(tq)
    heads = range(N_HEADS)

    def block(j, first):
        start = pl.multiple_of(j * tq, tq)
        qs = [q_ref[0, h] for h in heads]
        ks = [k_ref[0, h, pl.ds(start, tq), :] for h in heads]
        vs = [v_ref[0, h, pl.ds(start, tq), :] for h in heads]
        rs = [jnp.zeros((tq, 1), F32) if first else r_sc[h] for h in heads]
        outs, rs = _stick_blocks(qs, ks, vs, rs, visible if first else None, later_keys)
        acc = jnp.concatenate(outs, axis=1)
        if first:
            acc_sc[...] = acc
        else:
            acc_sc[...] += acc
        for h in heads:
            r_sc[h] = rs[h]
        return _min_over(rs)

    rmin = block(i, True)
    lax.while_loop(lambda st: jnp.logical_and(st[0] >= 0, st[1] <= STOP_MASS),
                   lambda st: (st[0] - 1, block(st[0], False)), (i - 1, rmin))
    o_ref[...] = acc_sc[...].astype(o_ref.dtype)


def _sample_attn_kernel(q_ref, kn_ref, vn_ref, ck_ref, cv_ref, ck_hbm, cv_hbm, o_ref, r_sc, acc_sc, kbuf, vbuf, sem,
                        *, nkc):
    b = pl.program_id(0)
    ds = q_ref.shape[0]
    ck = ck_ref.shape[0] // N_HEADS
    heads = range(N_HEADS)

    def head_rows(ref, h, n):
        return ref[pl.ds(h, n, stride=N_HEADS), :].astype(BF16)

    qs = [q_ref[:, h * HEAD_DIM:(h + 1) * HEAD_DIM].astype(BF16) for h in heads]
    visible = lax.broadcasted_iota(I32, (ds, LANES), 1) < lax.broadcasted_iota(I32, (ds, LANES), 0)
    pad = jnp.zeros((LANES - ds, HEAD_DIM), BF16)
    outs, rs = _stick_blocks(qs, [jnp.concatenate([head_rows(kn_ref, h, ds), pad], axis=0) for h in heads],
                             [jnp.concatenate([head_rows(vn_ref, h, ds), pad], axis=0) for h in heads],
                             [jnp.zeros((ds, 1), F32)] * N_HEADS, visible, _later_keys(LANES))
    acc_sc[...] = jnp.concatenate(outs, axis=1)
    for h in heads:
        r_sc[h] = rs[h]

    def chunk(kref, vref):
        outs, rs = _stick_blocks(qs, [head_rows(kref, h, ck) for h in heads], [head_rows(vref, h, ck) for h in heads],
                                 [r_sc[h] for h in heads], None, _later_keys(ck))
        acc_sc[...] += jnp.concatenate(outs, axis=1)
        for h in heads:
            r_sc[h] = rs[h]
        return _min_over(rs)

    def older(st):
        c = st[0]
        rows = ck * N_HEADS
        start = pl.multiple_of((b * nkc + c) * rows, rows)
        cpk = pltpu.make_async_copy(ck_hbm.at[pl.ds(start, rows)], kbuf, sem.at[0])
        cpv = pltpu.make_async_copy(cv_hbm.at[pl.ds(start, rows)], vbuf, sem.at[1])
        cpk.start()
        cpv.start()
        cpk.wait()
        cpv.wait()
        return c - 1, chunk(kbuf, vbuf)

    lax.while_loop(lambda st: jnp.logical_and(st[0] >= 0, st[1] <= STOP_MASS), older,
                   (nkc - 2, chunk(ck_ref, cv_ref)))
    o_ref[...] = acc_sc[...].astype(o_ref.dtype)


def _final_kernel(pos_ref, h_ref, y_hbm, yp_ref, ysm_ref, ybuf, sem, *, npt):
    i = pl.program_id(0)
    tk = h_ref.shape[0]
    y = h_ref[...] + _load_rows(ybuf.at[_gather_step(pos_ref, y_hbm, ybuf, sem, tk)], tk)

    @pl.when(i < npt)
    def _():
        yp_ref[...] = y

    @pl.when(i >= npt)
    def _():
        ysm_ref[...] = y


def _const_spec(shape):
    return pl.BlockSpec(shape, lambda *_: (0,) * len(shape))


def _router_operands(layer, ffn_norm_g, router_g_w, router_g_b, router_e_w, router_e_b):
    n_e = N_GROUPS * EXPERTS_PER_GROUP
    wrt = jnp.zeros((ROUTER_ROWS, D_MODEL), F32)
    wrt = wrt.at[:N_GROUPS].set(router_g_w[layer].T)
    wrt = wrt.at[EXPERT_ROW0:EXPERT_ROW0 + n_e].set(router_e_w[layer].transpose(0, 2, 1).reshape(n_e, D_MODEL))
    rb = jnp.zeros((ROUTER_ROWS,), F32)
    rb = rb.at[:N_GROUPS].set(router_g_b[layer])
    rb = rb.at[EXPERT_ROW0:EXPERT_ROW0 + n_e].set(router_e_b[layer].reshape(-1))
    wr_cols = jnp.zeros((D_MODEL, LANES), F32).at[:, :ROUTER_ROWS].set(wrt.T).astype(BF16)
    rb_cols = jnp.zeros((1, LANES), F32).at[0, :ROUTER_ROWS].set(rb)
    return (ffn_norm_g[layer][None, :], wrt, jnp.broadcast_to(rb[:, None], (ROUTER_ROWS, LANES))), (wr_cols, rb_cols)


def _moe(xs_rt, bucket, layer, route_cols, w_gate, w_up, w_down):
    nblk = bucket.shape[0]
    n_tok = nblk * LANES
    tm = EXPERT_TILE
    nr = -(-nblk // LANES) * LANES
    nt = -(-(n_tok + N_BUCKETS * (tm - 1)) // tm)
    assert nt <= LANES
    bk = jnp.pad(bucket.reshape(nblk, LANES), ((0, nr - nblk), (0, 0)), constant_values=N_BUCKETS)
    lane_row = jax.ShapeDtypeStruct((1, LANES), I32)
    pos, tile_bucket, pad_lo, pad_hi = pl.pallas_call(
        functools.partial(_rank_kernel, tm=tm, nt=nt),
        out_shape=(jax.ShapeDtypeStruct((nr, LANES), I32), lane_row, lane_row, lane_row),
        name=f"moe_rank_{layer}",
    )(bk)
    pos = pos[:nblk].reshape(n_tok)
    tb = tile_bucket[0, :nt]
    n_used = jnp.sum((tb < N_BUCKETS).astype(I32))
    tsrc = jnp.minimum(jnp.arange(nt, dtype=I32), n_used - 1)
    tbc = jnp.minimum(tb[tsrc], N_BUCKETS - 1)
    pair_lo = jnp.array([p[0] for p in PAIRS], I32)
    pair_hi = jnp.array([p[1] for p in PAIRS], I32)
    grp = tbc // len(PAIRS)
    telo = grp * EXPERTS_PER_GROUP + pair_lo[tbc % len(PAIRS)]
    tehi = grp * EXPERTS_PER_GROUP + pair_hi[tbc % len(PAIRS)]

    def wspec(shape, which):
        return pl.BlockSpec((1, 1) + shape, lambda r, lo, hi, *_: (layer, (lo, hi)[which][r], 0, 0))

    up_shape = (D_MODEL, EXPERT_DIM)
    down_shape = (EXPERT_DIM, D_MODEL)
    wr_cols, rb_cols = route_cols
    tile_shape = (tm * SLABS, LANES)
    y_sorted = pl.pallas_call(
        functools.partial(_expert_kernel, tm=tm),
        out_shape=jax.ShapeDtypeStruct((nt * tm * SLABS, LANES), F32),
        grid_spec=pltpu.PrefetchScalarGridSpec(
            num_scalar_prefetch=6, grid=(nt,),
            in_specs=[pl.BlockSpec(memory_space=pl.ANY),
                      pl.BlockSpec((D_MODEL, LANES), lambda r, *_: (0, 0)),
                      pl.BlockSpec((1, LANES), lambda r, *_: (0, 0)),
                      wspec(up_shape, 0), wspec(up_shape, 0), wspec(down_shape, 0),
                      wspec(up_shape, 1), wspec(up_shape, 1), wspec(down_shape, 1)],
            out_specs=pl.BlockSpec(tile_shape, lambda r, *_: (r, 0)),
            scratch_shapes=[pltpu.VMEM((2,) + tile_shape, F32), pltpu.VMEM((4,) + up_shape, BF16),
                            pltpu.VMEM((2,) + down_shape, BF16), pltpu.SMEM((nt * tm,), I32),
                            pltpu.SemaphoreType.DMA((2,))]),
        compiler_params=_cparams("arbitrary"),
        name=f"moe_experts_{layer}",
    )(telo, tehi, n_used.reshape(1), pos, pad_lo[0], pad_hi[0], xs_rt, wr_cols, rb_cols,
      w_gate, w_up, w_down, w_gate, w_up, w_down)
    return y_sorted, pos


def kernel(x_prompt, x_sample, cache_k, cache_v, a_norm_g, a_w_in, a_v_norm_g, a_w_s, a_b_s, a_w_out, kv_norm_g, w_kv, k_norm_g, b_norm_g, b_w_q, q_norm_g, b_w_o, ffn_norm_g, router_g_w, router_g_b, router_e_w, router_e_b, moe_w_gate, moe_w_up, moe_w_down):
    bsz, seq, _ = x_prompt.shape
    dbsz, dseq, _ = x_sample.shape
    past = cache_k.shape[1]
    tp, ts = bsz * seq, dbsz * dseq
    tk = STREAM_BLOCK
    tsp = -(-ts // tk) * tk
    n_tok = tp + tsp
    tb = MIXER_BLOCK
    gb = GMLP_BLOCK
    assert seq % ATTN_TILE == 0 and tp % tk == 0 and tk % tb == 0 and tk % ATTN_TILE == 0
    assert gb % dseq == 0 and dseq <= CHUNK and past % ATTN_TILE == 0
    nps, nss = tp // tb, tsp // tb
    npt, nst = tp // tk, tsp // tk
    xp = x_prompt.reshape(tp, D_MODEL)
    xsm = jnp.pad(x_sample.reshape(ts, D_MODEL), ((0, tsp - ts), (0, 0)))

    def rt_shape(n):
        return jax.ShapeDtypeStruct((n * SLABS, LANES), F32)

    def rt_spec(rows, index):
        return pl.BlockSpec((rows * SLABS, LANES), lambda i, *_: (index(i), 0))

    def route_specs(rows, n_prompt_steps):
        def pstep(i):
            return jnp.minimum(i, n_prompt_steps - 1)

        def sstep(i):
            return jnp.maximum(i - n_prompt_steps, 0)

        out_specs = [pl.BlockSpec((rows, D_MODEL), lambda i: (i, 0)), rt_spec(rows, lambda i: i),
                     pl.BlockSpec((rows // LANES, 1, LANES), lambda i: (i, 0, 0))]
        return pstep, sstep, out_specs

    route_in_specs = [_const_spec((1, D_MODEL)), _const_spec((ROUTER_ROWS, D_MODEL)),
                      _const_spec((ROUTER_ROWS, LANES))]
    route_out_shapes = (jax.ShapeDtypeStruct((n_tok, D_MODEL), F32), rt_shape(n_tok),
                        jax.ShapeDtypeStruct((n_tok // LANES, 1, LANES), I32))
    pstep, sstep, route_out_specs = route_specs(tb, nps)

    pos_i = jnp.arange(gb)
    mask = (pos_i[None, :] // CHUNK) <= (pos_i[:, None] // CHUNK)
    w_prompt = jnp.where(mask, a_w_s[0], 0.0)
    rep = gb // dseq
    w_sample = jnp.einsum("ij,hts->hitjs", jnp.eye(rep, dtype=F32), w_prompt[:, :dseq, :dseq]).reshape(
        A_HEADS, gb, gb)
    ws_all = jnp.stack([w_prompt, w_sample]).astype(BF16)
    b_prompt = a_b_s[0]
    b_sample = jnp.tile(a_b_s[0][:, :dseq], (1, rep))
    bs_all = jnp.stack([jnp.repeat(b.T, A_HEAD_DIM, axis=1) for b in (b_prompt, b_sample)])
    route0, cols0 = _router_operands(0, ffn_norm_g, router_g_w, router_g_b, router_e_w, router_e_b)
    h1, xs1, bucket1, v_rows = pl.pallas_call(
        functools.partial(_a_layer_kernel, nps=nps),
        out_shape=route_out_shapes + (jax.ShapeDtypeStruct((tsp, A_WIDTH), F32),),
        grid=(nps + nss,),
        in_specs=[pl.BlockSpec((tb, D_MODEL), lambda i: (pstep(i), 0)),
                  pl.BlockSpec((tb, D_MODEL), lambda i: (sstep(i), 0)),
                  _const_spec((1, D_MODEL)), _const_spec((D_MODEL, 2 * A_WIDTH)), _const_spec((1, A_WIDTH)),
                  pl.BlockSpec((1, A_HEADS, gb, gb), lambda i: (i // nps, 0, 0, 0)),
                  pl.BlockSpec((1, gb, A_WIDTH), lambda i: (i // nps, 0, 0)),
                  _const_spec((A_WIDTH, D_MODEL))] + route_in_specs,
        out_specs=route_out_specs + [pl.BlockSpec((tb, A_WIDTH), lambda i: (sstep(i), 0))],
        compiler_params=_cparams("arbitrary"),
        name="a_layer",
    )(xp, xsm, a_norm_g[0][None, :], a_w_in[0].astype(BF16), a_v_norm_g[0][None, :], ws_all, bs_all,
      a_w_out[0].astype(BF16), *route0)
    y1, pos1 = _moe(xs1, bucket1, 0, cols0, moe_w_gate, moe_w_up, moe_w_down)

    spt = seq // tk

    def ptile(i):
        return jnp.minimum(i, npt - 1)

    def stile(i):
        return jnp.maximum(i - npt, 0)

    row_spec = pl.BlockSpec((tk, D_MODEL), lambda i, *_: (i, 0))
    prow_spec = pl.BlockSpec((tk, D_MODEL), lambda i, *_: (ptile(i), 0))
    srow_spec = pl.BlockSpec((tk, D_MODEL), lambda i, *_: (stile(i), 0))
    head_spec = pl.BlockSpec((1, N_HEADS, tk, HEAD_DIM), lambda i, *_: (ptile(i) // spt, 0, ptile(i) % spt, 0))
    any_spec = pl.BlockSpec(memory_space=pl.ANY)
    gather_scratch = [pltpu.VMEM((2, tk * SLABS, LANES), F32), pltpu.SemaphoreType.DMA((2,))]
    f32_rows = lambda n: jax.ShapeDtypeStruct((n, D_MODEL), F32)
    head_major = jax.ShapeDtypeStruct((bsz, N_HEADS, seq, HEAD_DIM), BF16)
    h2, k_p, v_p, k_s, v_s, kb, vb, qb, q_s = pl.pallas_call(
        functools.partial(_kvq_kernel, npt=npt),
        out_shape=(f32_rows(n_tok), rt_shape(tp), rt_shape(tp), rt_shape(tsp), rt_shape(tsp),
                   head_major, head_major, head_major, f32_rows(tsp)),
        grid_spec=pltpu.PrefetchScalarGridSpec(
            num_scalar_prefetch=1, grid=(npt + nst,),
            in_specs=[row_spec, any_spec, _const_spec((1, D_MODEL)),
                      _const_spec((D_MODEL, 2 * D_MODEL)), _const_spec((1, HEAD_DIM)), _const_spec((1, D_MODEL)),
                      _const_spec((D_MODEL, D_MODEL)), _const_spec((1, HEAD_DIM))],
            out_specs=[row_spec, rt_spec(tk, ptile), rt_spec(tk, ptile), rt_spec(tk, stile), rt_spec(tk, stile),
                       head_spec, head_spec, head_spec, srow_spec],
            scratch_shapes=gather_scratch),
        compiler_params=_cparams("arbitrary"),
        name="kvq",
    )(pos1, h1, y1, kv_norm_g[None, :], w_kv.astype(BF16), k_norm_g[None, :], b_norm_g[0][None, :],
      b_w_q[0].astype(BF16), q_norm_g[0][None, :])

    tq = ATTN_TILE
    nq = seq // tq
    qspec = pl.BlockSpec((1, N_HEADS, tq, HEAD_DIM), lambda b, i: (b, 0, i, 0))
    kvspec = pl.BlockSpec((1, N_HEADS, seq, HEAD_DIM), lambda b, i: (b, 0, 0, 0), pipeline_mode=pl.Buffered(1))
    o_p = pl.pallas_call(
        _attn_kernel,
        out_shape=jax.ShapeDtypeStruct((tp, D_MODEL), BF16),
        grid=(bsz, nq),
        in_specs=[qspec, kvspec, kvspec],
        out_specs=pl.BlockSpec((tq, D_MODEL), lambda b, i: (b * nq + i, 0)),
        scratch_shapes=[pltpu.VMEM((N_HEADS, tq, 1), F32), pltpu.VMEM((tq, D_MODEL), F32)],
        compiler_params=_cparams("arbitrary", "arbitrary"),
        name="attn_prompt",
    )(qb, kb, vb)

    ck = ATTN_TILE
    nkc = past // ck
    new_spec = pl.BlockSpec((dseq, D_MODEL), lambda b: (b, 0))
    new_rt_spec = rt_spec(dseq, lambda b: b)
    newest_spec = rt_spec(ck, lambda b: b * nkc + nkc - 1)
    cache_k2 = cache_k.reshape(dbsz * past * N_HEADS, HEAD_DIM)
    cache_v2 = cache_v.reshape(dbsz * past * N_HEADS, HEAD_DIM)
    chunk_buf = pltpu.VMEM((ck * N_HEADS, HEAD_DIM), F32)
    o_s = pl.pallas_call(
        functools.partial(_sample_attn_kernel, nkc=nkc),
        out_shape=jax.ShapeDtypeStruct((ts, D_MODEL), BF16),
        grid=(dbsz,),
        in_specs=[new_spec, new_rt_spec, new_rt_spec, newest_spec, newest_spec, any_spec, any_spec],
        out_specs=new_spec,
        scratch_shapes=[pltpu.VMEM((N_HEADS, dseq, 1), F32), pltpu.VMEM((dseq, D_MODEL), F32), chunk_buf, chunk_buf,
                        pltpu.SemaphoreType.DMA((2,))],
        compiler_params=_cparams("arbitrary"),
        name="attn_sample",
    )(q_s, k_s, v_s, cache_k2, cache_v2, cache_k2, cache_v2)
    o_s = jnp.pad(o_s, ((0, tsp - ts), (0, 0)))

    route1, cols1 = _router_operands(1, ffn_norm_g, router_g_w, router_g_b, router_e_w, router_e_b)
    pstep, sstep, route_out_specs = route_specs(tk, npt)
    h3, xs2, bucket2 = pl.pallas_call(
        functools.partial(_oproj_kernel, nps=npt),
        out_shape=route_out_shapes,
        grid=(npt + nst,),
        in_specs=[pl.BlockSpec((tk, D_MODEL), lambda i: (pstep(i), 0)),
                  pl.BlockSpec((tk, D_MODEL), lambda i: (sstep(i), 0)),
                  pl.BlockSpec((tk, D_MODEL), lambda i: (i, 0)), _const_spec((D_MODEL, D_MODEL))] + route_in_specs,
        out_specs=route_out_specs,
        compiler_params=_cparams("arbitrary"),
        name="o_proj",
    )(o_p, o_s, h2, b_w_o[0].astype(BF16), *route1)
    y2, pos2 = _moe(xs2, bucket2, 1, cols1, moe_w_gate, moe_w_up, moe_w_down)

    y_p, y_s = pl.pallas_call(
        functools.partial(_final_kernel, npt=npt),
        out_shape=(f32_rows(tp), f32_rows(tsp)),
        grid_spec=pltpu.PrefetchScalarGridSpec(
            num_scalar_prefetch=1, grid=(npt + nst,),
            in_specs=[row_spec, any_spec],
            out_specs=[prow_spec, srow_spec],
            scratch_shapes=gather_scratch),
        compiler_params=_cparams("arbitrary"),
        name="final_residual",
    )(pos2, h3, y2)

    kv_shape_p = (bsz, seq, N_HEADS, HEAD_DIM)
    kv_shape_s = (dbsz, dseq, N_HEADS, HEAD_DIM)
    k_s = k_s[:ts * SLABS].reshape(kv_shape_s)
    v_s = v_s[:ts * SLABS].reshape(kv_shape_s)
    return (y_p.reshape(bsz, seq, D_MODEL), y_s[:ts].reshape(dbsz, dseq, D_MODEL),
            k_p.reshape(kv_shape_p), v_p.reshape(kv_shape_p), k_s, v_s,
            v_rows[:ts].reshape(1, dbsz, dseq, A_WIDTH))
```

```python
import functools

import jax
import jax.numpy as jnp
from jax import lax
from jax.experimental import pallas as pl
from jax.experimental.pallas import tpu as pltpu

F32 = jnp.float32
BF16 = jnp.bfloat16
I32 = jnp.int32

D_MODEL = 1024
CHUNK = 64
GMLP_BLOCK = 128
A_WIDTH = 2 * D_MODEL
A_HEADS = 8
A_HEAD_DIM = A_WIDTH // A_HEADS
N_HEADS = 8
HEAD_DIM = D_MODEL // N_HEADS
N_GROUPS = 4
EXPERTS_PER_GROUP = 4
EXPERT_DIM = D_MODEL // 2
EPS = 1e-6

LANES = 128
SUBLANES = 8
MXU_WIDTH = 256
SLABS = D_MODEL // LANES
assert SLABS == SUBLANES and N_HEADS == SLABS and HEAD_DIM == LANES
PAIRS = ((0, 1), (0, 2), (0, 3), (1, 2), (1, 3), (2, 3))
N_BUCKETS = N_GROUPS * len(PAIRS)
ROUTER_ROWS = 32
EXPERT_ROW0 = 8

MIXER_BLOCK = 4 * GMLP_BLOCK
STREAM_BLOCK = 512
EXPERT_TILE = 256
ATTN_TILE = 256
VMEM_LIMIT = 56 * 1024 * 1024
STOP_MASS = 105.0


def _cparams(*sem):
    return pltpu.CompilerParams(dimension_semantics=sem, vmem_limit_bytes=VMEM_LIMIT)


def _rms(x, g):
    ms = jnp.mean(x * x, axis=-1, keepdims=True)
    return x * lax.rsqrt(ms + EPS) * g


def _load_rows(ref, n):
    return jnp.concatenate([ref[pl.ds(c, n, stride=SLABS), :] for c in range(SLABS)], axis=1)


def _store_rows(ref, x):
    n = x.shape[0]
    for c in range(SLABS):
        ref[pl.ds(c, n, stride=SLABS), :] = x[:, c * LANES:(c + 1) * LANES]


def _row_copy(src_hbm, src_row, buf, slot, k, sem):
    src = src_row * SLABS if isinstance(src_row, int) else pl.multiple_of(src_row * SLABS, SLABS)
    return pltpu.make_async_copy(src_hbm.at[pl.ds(src, SLABS)], buf.at[slot, pl.ds(k * SLABS, SLABS)], sem.at[slot])


def _gather_step(idx_ref, src_hbm, buf, sem, rows, n_steps=None):
    i = pl.program_id(0)
    slot = i % 2
    if n_steps is None:
        n_steps = pl.num_programs(0)

    @pl.when(i == 0)
    def _():
        for k in range(rows):
            _row_copy(src_hbm, idx_ref[k], buf, 0, k, sem).start(priority=k % 2)

    @pl.when(i + 1 < n_steps)
    def _():
        for k in range(rows):
            _row_copy(src_hbm, idx_ref[(i + 1) * rows + k], buf, 1 - slot, k, sem).start(priority=k % 2)

    @pl.when(i < n_steps)
    def _():
        for k in range(rows):
            _row_copy(src_hbm, 0, buf, slot, k, sem).wait()

    return slot


def _first_argmax(vals):
    m = vals[0]
    for v in vals[1:]:
        m = jnp.maximum(m, v)
    idx = jnp.full(m.shape, len(vals) - 1, I32)
    for r in range(len(vals) - 2, -1, -1):
        idx = jnp.where(vals[r] == m, r, idx)
    return m, idx


def _route(xn, wrt_ref, rb_ref, xs_ref, bucket_ref):
    lt = lax.dot_general(wrt_ref[...], xn, (((1,), (1,)), ((), ())),
                         precision=lax.Precision.HIGHEST, preferred_element_type=F32) + rb_ref[:, 0:1]
    _, g_idx = _first_argmax([lt[r:r + 1, :] for r in range(N_GROUPS)])
    le = []
    for e in range(EXPERTS_PER_GROUP):
        row = EXPERT_ROW0 + EXPERTS_PER_GROUP * (N_GROUPS - 1) + e
        v = lt[row:row + 1, :]
        for g in range(N_GROUPS - 2, -1, -1):
            row = EXPERT_ROW0 + EXPERTS_PER_GROUP * g + e
            v = jnp.where(g_idx == g, lt[row:row + 1, :], v)
        le.append(v)
    _, i1 = _first_argmax(le)
    _, i2 = _first_argmax([jnp.where(i1 == e, -jnp.inf, le[e]) for e in range(EXPERTS_PER_GROUP)])
    lo = jnp.minimum(i1, i2)
    hi = jnp.maximum(i1, i2)
    pair = jnp.where(lo == 0, hi - 1, jnp.where(lo == 1, hi + 1, 5))
    bucket = g_idx * len(PAIRS) + pair
    for sb in range(xn.shape[0] // LANES):
        bucket_ref[sb] = bucket[:, sb * LANES:(sb + 1) * LANES]
    _store_rows(xs_ref, xn)


def _a_layer_kernel(xp_ref, xsm_ref, ag_ref, win_ref, vg_ref, ws_ref, bs_ref, wout_ref, fg_ref, wrt_ref, rb_ref,
                    h_ref, xs_ref, bucket_ref, vrow_ref, *, nps):
    i = pl.program_id(0)
    x = jnp.where(i < nps, xp_ref[...], xsm_ref[...])
    xn = _rms(x, ag_ref[...]).astype(BF16)
    z = jax.nn.gelu(jnp.dot(xn, win_ref[...], preferred_element_type=F32))
    u = z[:, :A_WIDTH]
    v = _rms(z[:, A_WIDTH:], vg_ref[...])

    @pl.when(i >= nps)
    def _():
        vrow_ref[...] = v

    vb = v.astype(BF16)
    gate = jnp.concatenate([
        jnp.concatenate(
            [jnp.dot(ws_ref[0, h], vb[sb * GMLP_BLOCK:(sb + 1) * GMLP_BLOCK, h * A_HEAD_DIM:(h + 1) * A_HEAD_DIM],
                     preferred_element_type=F32) for h in range(A_HEADS)], axis=1) + bs_ref[0]
        for sb in range(x.shape[0] // GMLP_BLOCK)], axis=0)
    s = (u * gate).astype(BF16)
    h1 = x + jnp.dot(s, wout_ref[...], preferred_element_type=F32)
    h_ref[...] = h1
    _route(_rms(h1, fg_ref[...]), wrt_ref, rb_ref, xs_ref, bucket_ref)


def _oproj_kernel(op_ref, osm_ref, h_in_ref, wo_ref, fg_ref, wrt_ref, rb_ref, h_ref, xs_ref, bucket_ref, *, nps):
    i = pl.program_id(0)
    o = jnp.where(i < nps, op_ref[...], osm_ref[...])
    h3 = h_in_ref[...] + jnp.dot(o, wo_ref[...], preferred_element_type=F32)
    h_ref[...] = h3
    _route(_rms(h3, fg_ref[...]), wrt_ref, rb_ref, xs_ref, bucket_ref)


def _rank_kernel(b_ref, pos_ref, tb_ref, *, tm):
    bk = b_ref[...]
    nr = bk.shape[0]
    r_i = lax.broadcasted_iota(I32, (LANES, LANES), 0)
    c_i = lax.broadcasted_iota(I32, (LANES, LANES), 1)
    upper = (r_i <= c_i).astype(BF16)
    rr = lax.broadcasted_iota(I32, (nr, nr), 0)
    cc = lax.broadcasted_iota(I32, (nr, nr), 1)
    before_rows = (cc < rr).astype(BF16)
    lane = lax.broadcasted_iota(I32, (1, LANES), 1)
    tile_start = lane.astype(F32) * tm
    pos = jnp.zeros((nr, LANES), F32)
    seg_start = jnp.zeros((1, LANES), F32)
    tile_bucket = jnp.zeros((1, LANES), I32)
    for b in range(N_BUCKETS):
        m = bk == b
        pref = jnp.dot(jnp.where(m, 1.0, 0.0).astype(BF16), upper, preferred_element_type=F32)
        rowtot = jnp.broadcast_to(pref[:, LANES - 1:LANES], (nr, LANES))
        before = jnp.dot(before_rows, rowtot.astype(BF16), preferred_element_type=F32)
        cnt = jnp.sum(rowtot, axis=0, keepdims=True)
        pos = pos + jnp.where(m, seg_start + before + pref - 1.0, 0.0)
        seg_start = seg_start + jnp.ceil(cnt / tm) * tm
        tile_bucket = tile_bucket + (seg_start <= tile_start).astype(I32)
    pos_ref[...] = pos.astype(I32)
    tb_ref[...] = tile_bucket


def _build_inverse(pos_ref, inv_ref):
    n_tok = pos_ref.shape[0]
    n_rows = inv_ref.shape[0]
    batch = 8
    assert n_tok % batch == 0 and n_rows % batch == 0

    def fill(g, v):
        for j in range(batch):
            inv_ref[g * batch + j] = v + j
        return jnp.where(v + batch == n_tok, 0, v + batch)

    def put(g, carry):
        base = g * batch
        dst = [pos_ref[base + j] for j in range(batch)]
        for j in range(batch):
            inv_ref[dst[j]] = base + j
        return carry

    lax.fori_loop(0, n_rows // batch, fill, 0)
    lax.fori_loop(0, n_tok // batch, put, 0)


def _expert_kernel(telo_ref, tehi_ref, nused_ref, pos_ref, xs_hbm, wr_ref, rb_ref, wgl_ref, wul_ref, wdl_ref,
                   wgh_ref, wuh_ref, wdh_ref, y_ref, xbuf, wup, wdown, inv_ref, gsem, *, tm):
    r = pl.program_id(0)
    n_used = nused_ref[0]

    @pl.when(r == 0)
    def _():
        _build_inverse(pos_ref, inv_ref)

    slot = _gather_step(inv_ref, xs_hbm, xbuf, gsem, tm, n_steps=n_used)

    @pl.when(r < n_used)
    def _():
        prev = jnp.maximum(r - 1, 0)

        @pl.when(jnp.logical_or(r == 0, telo_ref[r] != telo_ref[prev]))
        def _():
            wup[0] = wgl_ref[0, 0].astype(BF16)
            wup[1] = wul_ref[0, 0].astype(BF16)
            wdown[0] = wdl_ref[0, 0].astype(BF16)

        @pl.when(jnp.logical_or(r == 0, tehi_ref[r] != tehi_ref[prev]))
        def _():
            wup[2] = wgh_ref[0, 0].astype(BF16)
            wup[3] = wuh_ref[0, 0].astype(BF16)
            wdown[1] = wdh_ref[0, 0].astype(BF16)

        x = _load_rows(xbuf.at[slot], tm).astype(BF16)
        logits = jnp.dot(x, wr_ref[...], preferred_element_type=F32) + rb_ref[...]
        lane = lax.broadcasted_iota(I32, logits.shape, 1)
        grp = telo_ref[r] // EXPERTS_PER_GROUP
        is_group = lane < N_GROUPS
        m = jnp.max(jnp.where(is_group, logits, -jnp.inf), axis=1, keepdims=True)
        ex = jnp.exp(logits - m)
        p_g = (jnp.sum(jnp.where(lane == grp, ex, 0.0), axis=1, keepdims=True)
               / jnp.sum(jnp.where(is_group, ex, 0.0), axis=1, keepdims=True))
        l_lo = jnp.sum(jnp.where(lane == EXPERT_ROW0 + telo_ref[r], logits, 0.0), axis=1, keepdims=True)
        l_hi = jnp.sum(jnp.where(lane == EXPERT_ROW0 + tehi_ref[r], logits, 0.0), axis=1, keepdims=True)
        mm = jnp.maximum(l_lo, l_hi)
        e_lo = jnp.exp(l_lo - mm)
        e_hi = jnp.exp(l_hi - mm)
        gates = (p_g * (e_lo / (e_lo + e_hi)), p_g * (e_hi / (e_lo + e_hi)))

        y = None
        for which in range(2):
            a = jnp.dot(x, wup[2 * which], preferred_element_type=F32)
            b = jnp.dot(x, wup[2 * which + 1], preferred_element_type=F32)
            hid = (jax.nn.silu(a) * b * gates[which]).astype(BF16)
            part = jnp.dot(hid, wdown[which], preferred_element_type=F32)
            y = part if y is None else y + part
        _store_rows(y_ref, y)

    @pl.when(r >= n_used)
    def _():
        y_ref[...] = jnp.zeros_like(y_ref)


def _head_rms(x, g):
    return jnp.concatenate([_rms(x[:, h * HEAD_DIM:(h + 1) * HEAD_DIM], g) for h in range(N_HEADS)], axis=1)


def _kvq_kernel(pos_ref, h_in_ref, y_hbm, kvg_ref, wkv_ref, kng_ref, bg_ref, wq_ref, qng_ref,
                h_ref, kp_ref, vp_ref, ksm_ref, vsm_ref, kb_ref, vb_ref, qb_ref, qsm_ref, ybuf, sem, *, npt):
    i = pl.program_id(0)
    tk = h_in_ref.shape[0]
    h2 = h_in_ref[...] + _load_rows(ybuf.at[_gather_step(pos_ref, y_hbm, ybuf, sem, tk)], tk)
    h_ref[...] = h2
    hn = h2 * lax.rsqrt(jnp.mean(h2 * h2, axis=-1, keepdims=True) + EPS)
    kv = jnp.dot((hn * kvg_ref[...]).astype(BF16), wkv_ref[...], preferred_element_type=F32)
    q = jnp.dot((hn * bg_ref[...]).astype(BF16), wq_ref[...], preferred_element_type=F32)
    k = _head_rms(kv[:, :D_MODEL], kng_ref[...])
    v = kv[:, D_MODEL:]
    q = _head_rms(q, qng_ref[...]) * (HEAD_DIM ** -0.5)

    @pl.when(i < npt)
    def _():
        _store_rows(kp_ref, k)
        _store_rows(vp_ref, v)
        for h in range(N_HEADS):
            sl = slice(h * HEAD_DIM, (h + 1) * HEAD_DIM)
            kb_ref[0, h] = k[:, sl].astype(BF16)
            vb_ref[0, h] = v[:, sl].astype(BF16)
            qb_ref[0, h] = q[:, sl].astype(BF16)

    @pl.when(i >= npt)
    def _():
        _store_rows(ksm_ref, k)
        _store_rows(vsm_ref, v)
        qsm_ref[...] = q


def _stick_blocks(qs, ks, vs, rs, visible, later_keys):
    n = len(qs)
    zs = [lax.dot_general(qs[h], ks[h], (((1,), (1,)), ((), ())), preferred_element_type=F32) for h in range(n)]
    sps = [jnp.maximum(z, 0.0) + jnp.log(1.0 + jnp.exp(-jnp.abs(z))) for z in zs]
    if visible is not None:
        sps = [jnp.where(visible, sp, 0.0) for sp in sps]
    his = [sp.astype(BF16) for sp in sps]
    los = [(sp - hi.astype(F32)).astype(BF16) for sp, hi in zip(sps, his)]
    cs = [jnp.dot(hi, later_keys, preferred_element_type=F32) + jnp.dot(lo, later_keys, preferred_element_type=F32)
          for hi, lo in zip(his, los)]
    ws = [jnp.exp(zs[h] - cs[h] - rs[h]) for h in range(n)]
    if visible is not None:
        ws = [jnp.where(visible, w, 0.0) for w in ws]
    outs = [jnp.dot(ws[h].astype(BF16), vs[h], preferred_element_type=F32) for h in range(n)]
    return outs, [rs[h] + cs[h][:, 0:1] for h in range(n)]


def _later_keys(n):
    rowi = lax.broadcasted_iota(I32, (n, n), 0)
    coli = lax.broadcasted_iota(I32, (n, n), 1)
    return (rowi >= coli).astype(BF16)


def _min_over(rs):
    m = rs[0]
    for r in rs[1:]:
        m = jnp.minimum(m, r)
    return jnp.min(m)


def _attn_kernel(q_ref, k_ref, v_ref, o_ref, r_sc, acc_sc):
    i = pl.program_id(1)
    tq = q_ref.shape[2]
    visible = lax.broadcasted_iota(I32, (tq, tq), 1) < lax.broadcasted_iota(I32, (tq, tq), 0)
    later_keys = _later_keys(tq)
    heads = range(N_HEADS)

    def block(j, first):
        start = pl.multiple_of(j * tq, tq)
        qs = [q_ref[0, h] for h in heads]
        ks = [k_ref[0, h, pl.ds(start, tq), :] for h in heads]
        vs = [v_ref[0, h, pl.ds(start, tq), :] for h in heads]
        rs = [jnp.zeros((tq, 1), F32) if first else r_sc[h] for h in heads]
        outs, rs = _stick_blocks(qs, ks, vs, rs, visible if first else None, later_keys)
        acc = jnp.concatenate(outs, axis=1)
        if first:
            acc_sc[...] = acc
        else:
            acc_sc[...] += acc
        for h in heads:
            r_sc[h] = rs[h]
        return _min_over(rs)

    rmin = block(i, True)
    lax.while_loop(lambda st: jnp.logical_and(st[0] >= 0, st[1] <= STOP_MASS),
                   lambda st: (st[0] - 1, block(st[0], False)), (i - 1, rmin))
    o_ref[...] = acc_sc[...].astype(o_ref.dtype)


def _sample_attn_kernel(q_ref, kn_ref, vn_ref, ck_ref, cv_ref, ck_hbm, cv_hbm, o_ref, r_sc, acc_sc, kbuf, vbuf, sem,
                        *, nkc):
    b = pl.program_id(0)
    ds = q_ref.shape[0]
    ck = ck_ref.shape[0] // N_HEADS
    heads = range(N_HEADS)

    def head_rows(ref, h, n):
        return ref[pl.ds(h, n, stride=N_HEADS), :].astype(BF16)

    qs = [q_ref[:, h * HEAD_DIM:(h + 1) * HEAD_DIM].astype(BF16) for h in heads]
    visible = lax.broadcasted_iota(I32, (ds, LANES), 1) < lax.broadcasted_iota(I32, (ds, LANES), 0)
    pad = jnp.zeros((LANES - ds, HEAD_DIM), BF16)
    outs, rs = _stick_blocks(qs, [jnp.concatenate([head_rows(kn_ref, h, ds), pad], axis=0) for h in heads],
                             [jnp.concatenate([head_rows(vn_ref, h, ds), pad], axis=0) for h in heads],
                             [jnp.zeros((ds, 1), F32)] * N_HEADS, visible, _later_keys(LANES))
    acc_sc[...] = jnp.concatenate(outs, axis=1)
    for h in heads:
        r_sc[h] = rs[h]

    def chunk(kref, vref):
        outs, rs = _stick_blocks(qs, [head_rows(kref, h, ck) for h in heads], [head_rows(vref, h, ck) for h in heads],
                                 [r_sc[h] for h in heads], None, _later_keys(ck))
        acc_sc[...] += jnp.concatenate(outs, axis=1)
        for h in heads:
            r_sc[h] = rs[h]
        return _min_over(rs)

    def older(st):
        c = st[0]
        rows = ck * N_HEADS
        start = pl.multiple_of((b * nkc + c) * rows, rows)
        cpk = pltpu.make_async_copy(ck_hbm.at[pl.ds(start, rows)], kbuf, sem.at[0])
        cpv = pltpu.make_async_copy(cv_hbm.at[pl.ds(start, rows)], vbuf, sem.at[1])
        cpk.start()
        cpv.start()
        cpk.wait()
        cpv.wait()
        return c - 1, chunk(kbuf, vbuf)

    lax.while_loop(lambda st: jnp.logical_and(st[0] >= 0, st[1] <= STOP_MASS), older,
                   (nkc - 2, chunk(ck_ref, cv_ref)))
    o_ref[...] = acc_sc[...].astype(o_ref.dtype)


def _final_kernel(pos_ref, h_ref, y_hbm, yp_ref, ysm_ref, ybuf, sem, *, npt):
    i = pl.program_id(0)
    tk = h_ref.shape[0]
    y = h_ref[...] + _load_rows(ybuf.at[_gather_step(pos_ref, y_hbm, ybuf, sem, tk)], tk)

    @pl.when(i < npt)
    def _():
        yp_ref[...] = y

    @pl.when(i >= npt)
    def _():
        ysm_ref[...] = y


def _const_spec(shape):
    return pl.BlockSpec(shape, lambda *_: (0,) * len(shape))


def _router_operands(layer, ffn_norm_g, router_g_w, router_g_b, router_e_w, router_e_b):
    n_e = N_GROUPS * EXPERTS_PER_GROUP
    wrt = jnp.zeros((ROUTER_ROWS, D_MODEL), F32)
    wrt = wrt.at[:N_GROUPS].set(router_g_w[layer].T)
    wrt = wrt.at[EXPERT_ROW0:EXPERT_ROW0 + n_e].set(router_e_w[layer].transpose(0, 2, 1).reshape(n_e, D_MODEL))
    rb = jnp.zeros((ROUTER_ROWS,), F32)
    rb = rb.at[:N_GROUPS].set(router_g_b[layer])
    rb = rb.at[EXPERT_ROW0:EXPERT_ROW0 + n_e].set(router_e_b[layer].reshape(-1))
    wr_cols = jnp.zeros((D_MODEL, LANES), F32).at[:, :ROUTER_ROWS].set(wrt.T).astype(BF16)
    rb_cols = jnp.zeros((1, LANES), F32).at[0, :ROUTER_ROWS].set(rb)
    return (ffn_norm_g[layer][None, :], wrt, jnp.broadcast_to(rb[:, None], (ROUTER_ROWS, LANES))), (wr_cols, rb_cols)


def _moe(xs_rt, bucket, layer, route_cols, w_gate, w_up, w_down):
    nblk = bucket.shape[0]
    n_tok = nblk * LANES
    tm = EXPERT_TILE
    nr = -(-nblk // LANES) * LANES
    nt = -(-(n_tok + N_BUCKETS * (tm - 1)) // tm)
    assert nt <= LANES
    bk = jnp.pad(bucket.reshape(nblk, LANES), ((0, nr - nblk), (0, 0)), constant_values=N_BUCKETS)
    pos, tile_bucket = pl.pallas_call(
        functools.partial(_rank_kernel, tm=tm),
        out_shape=(jax.ShapeDtypeStruct((nr, LANES), I32), jax.ShapeDtypeStruct((1, LANES), I32)),
        name=f"moe_rank_{layer}",
    )(bk)
    pos = pos[:nblk].reshape(n_tok)
    tb = tile_bucket[0, :nt]
    n_used = jnp.sum((tb < N_BUCKETS).astype(I32))
    tsrc = jnp.minimum(jnp.arange(nt, dtype=I32), n_used - 1)
    tbc = jnp.minimum(tb[tsrc], N_BUCKETS - 1)
    pair_lo = jnp.array([p[0] for p in PAIRS], I32)
    pair_hi = jnp.array([p[1] for p in PAIRS], I32)
    grp = tbc // len(PAIRS)
    telo = grp * EXPERTS_PER_GROUP + pair_lo[tbc % len(PAIRS)]
    tehi = grp * EXPERTS_PER_GROUP + pair_hi[tbc % len(PAIRS)]

    def wspec(shape, which):
        return pl.BlockSpec((1, 1) + shape, lambda r, lo, hi, *_: (layer, (lo, hi)[which][r], 0, 0))

    up_shape = (D_MODEL, EXPERT_DIM)
    down_shape = (EXPERT_DIM, D_MODEL)
    wr_cols, rb_cols = route_cols
    tile_shape = (tm * SLABS, LANES)
    y_sorted = pl.pallas_call(
        functools.partial(_expert_kernel, tm=tm),
        out_shape=jax.ShapeDtypeStruct((nt * tm * SLABS, LANES), F32),
        grid_spec=pltpu.PrefetchScalarGridSpec(
            num_scalar_prefetch=4, grid=(nt,),
            in_specs=[pl.BlockSpec(memory_space=pl.ANY),
                      pl.BlockSpec((D_MODEL, LANES), lambda r, *_: (0, 0)),
                      pl.BlockSpec((1, LANES), lambda r, *_: (0, 0)),
                      wspec(up_shape, 0), wspec(up_shape, 0), wspec(down_shape, 0),
                      wspec(up_shape, 1), wspec(up_shape, 1), wspec(down_shape, 1)],
            out_specs=pl.BlockSpec(tile_shape, lambda r, *_: (r, 0)),
            scratch_shapes=[pltpu.VMEM((2,) + tile_shape, F32), pltpu.VMEM((4,) + up_shape, BF16),
                            pltpu.VMEM((2,) + down_shape, BF16), pltpu.SMEM((nt * tm,), I32),
                            pltpu.SemaphoreType.DMA((2,))]),
        compiler_params=_cparams("arbitrary"),
        name=f"moe_experts_{layer}",
    )(telo, tehi, n_used.reshape(1), pos, xs_rt, wr_cols, rb_cols, w_gate, w_up, w_down, w_gate, w_up, w_down)
    return y_sorted, pos


def kernel(x_prompt, x_sample, cache_k, cache_v, a_norm_g, a_w_in, a_v_norm_g, a_w_s, a_b_s, a_w_out, kv_norm_g, w_kv, k_norm_g, b_norm_g, b_w_q, q_norm_g, b_w_o, ffn_norm_g, router_g_w, router_g_b, router_e_w, router_e_b, moe_w_gate, moe_w_up, moe_w_down):
    bsz, seq, _ = x_prompt.shape
    dbsz, dseq, _ = x_sample.shape
    past = cache_k.shape[1]
    tp, ts = bsz * seq, dbsz * dseq
    tk = STREAM_BLOCK
    tsp = -(-ts // tk) * tk
    n_tok = tp + tsp
    tb = MIXER_BLOCK
    gb = GMLP_BLOCK
    assert seq % ATTN_TILE == 0 and tp % tk == 0 and tk % tb == 0 and tk % ATTN_TILE == 0
    assert gb % dseq == 0 and dseq <= CHUNK and past % ATTN_TILE == 0
    nps, nss = tp // tb, tsp // tb
    npt, nst = tp // tk, tsp // tk
    xp = x_prompt.reshape(tp, D_MODEL)
    xsm = jnp.pad(x_sample.reshape(ts, D_MODEL), ((0, tsp - ts), (0, 0)))

    def rt_shape(n):
        return jax.ShapeDtypeStruct((n * SLABS, LANES), F32)

    def rt_spec(rows, index):
        return pl.BlockSpec((rows * SLABS, LANES), lambda i, *_: (index(i), 0))

    def route_specs(rows, n_prompt_steps):
        def pstep(i):
            return jnp.minimum(i, n_prompt_steps - 1)

        def sstep(i):
            return jnp.maximum(i - n_prompt_steps, 0)

        out_specs = [pl.BlockSpec((rows, D_MODEL), lambda i: (i, 0)), rt_spec(rows, lambda i: i),
                     pl.BlockSpec((rows // LANES, 1, LANES), lambda i: (i, 0, 0))]
        return pstep, sstep, out_specs

    route_in_specs = [_const_spec((1, D_MODEL)), _const_spec((ROUTER_ROWS, D_MODEL)),
                      _const_spec((ROUTER_ROWS, LANES))]
    route_out_shapes = (jax.ShapeDtypeStruct((n_tok, D_MODEL), F32), rt_shape(n_tok),
                        jax.ShapeDtypeStruct((n_tok // LANES, 1, LANES), I32))
    pstep, sstep, route_out_specs = route_specs(tb, nps)

    pos_i = jnp.arange(gb)
    mask = (pos_i[None, :] // CHUNK) <= (pos_i[:, None] // CHUNK)
    w_prompt = jnp.where(mask, a_w_s[0], 0.0)
    rep = gb // dseq
    w_sample = jnp.einsum("ij,hts->hitjs", jnp.eye(rep, dtype=F32), w_prompt[:, :dseq, :dseq]).reshape(
        A_HEADS, gb, gb)
    ws_all = jnp.stack([w_prompt, w_sample]).astype(BF16)
    b_prompt = a_b_s[0]
    b_sample = jnp.tile(a_b_s[0][:, :dseq], (1, rep))
    bs_all = jnp.stack([jnp.repeat(b.T, A_HEAD_DIM, axis=1) for b in (b_prompt, b_sample)])
    route0, cols0 = _router_operands(0, ffn_norm_g, router_g_w, router_g_b, router_e_w, router_e_b)
    h1, xs1, bucket1, v_rows = pl.pallas_call(
        functools.partial(_a_layer_kernel, nps=nps),
        out_shape=route_out_shapes + (jax.ShapeDtypeStruct((tsp, A_WIDTH), F32),),
        grid=(nps + nss,),
        in_specs=[pl.BlockSpec((tb, D_MODEL), lambda i: (pstep(i), 0)),
                  pl.BlockSpec((tb, D_MODEL), lambda i: (sstep(i), 0)),
                  _const_spec((1, D_MODEL)), _const_spec((D_MODEL, 2 * A_WIDTH)), _const_spec((1, A_WIDTH)),
                  pl.BlockSpec((1, A_HEADS, gb, gb), lambda i: (i // nps, 0, 0, 0)),
                  pl.BlockSpec((1, gb, A_WIDTH), lambda i: (i // nps, 0, 0)),
                  _const_spec((A_WIDTH, D_MODEL))] + route_in_specs,
        out_specs=route_out_specs + [pl.BlockSpec((tb, A_WIDTH), lambda i: (sstep(i), 0))],
        compiler_params=_cparams("arbitrary"),
        name="a_layer",
    )(xp, xsm, a_norm_g[0][None, :], a_w_in[0].astype(BF16), a_v_norm_g[0][None, :], ws_all, bs_all,
      a_w_out[0].astype(BF16), *route0)
    y1, pos1 = _moe(xs1, bucket1, 0, cols0, moe_w_gate, moe_w_up, moe_w_down)

    spt = seq // tk

    def ptile(i):
        return jnp.minimum(i, npt - 1)

    def stile(i):
        return jnp.maximum(i - npt, 0)

    row_spec = pl.BlockSpec((tk, D_MODEL), lambda i, *_: (i, 0))
    prow_spec = pl.BlockSpec((tk, D_MODEL), lambda i, *_: (ptile(i), 0))
    srow_spec = pl.BlockSpec((tk, D_MODEL), lambda i, *_: (stile(i), 0))
    head_spec = pl.BlockSpec((1, N_HEADS, tk, HEAD_DIM), lambda i, *_: (ptile(i) // spt, 0, ptile(i) % spt, 0))
    any_spec = pl.BlockSpec(memory_space=pl.ANY)
    gather_scratch = [pltpu.VMEM((2, tk * SLABS, LANES), F32), pltpu.SemaphoreType.DMA((2,))]
    f32_rows = lambda n: jax.ShapeDtypeStruct((n, D_MODEL), F32)
    head_major = jax.ShapeDtypeStruct((bsz, N_HEADS, seq, HEAD_DIM), BF16)
    h2, k_p, v_p, k_s, v_s, kb, vb, qb, q_s = pl.pallas_call(
        functools.partial(_kvq_kernel, npt=npt),
        out_shape=(f32_rows(n_tok), rt_shape(tp), rt_shape(tp), rt_shape(tsp), rt_shape(tsp),
                   head_major, head_major, head_major, f32_rows(tsp)),
        grid_spec=pltpu.PrefetchScalarGridSpec(
            num_scalar_prefetch=1, grid=(npt + nst,),
            in_specs=[row_spec, any_spec, _const_spec((1, D_MODEL)),
                      _const_spec((D_MODEL, 2 * D_MODEL)), _const_spec((1, HEAD_DIM)), _const_spec((1, D_MODEL)),
                      _const_spec((D_MODEL, D_MODEL)), _const_spec((1, HEAD_DIM))],
            out_specs=[row_spec, rt_spec(tk, ptile), rt_spec(tk, ptile), rt_spec(tk, stile), rt_spec(tk, stile),
                       head_spec, head_spec, head_spec, srow_spec],
            scratch_shapes=gather_scratch),
        compiler_params=_cparams("arbitrary"),
        name="kvq",
    )(pos1, h1, y1, kv_norm_g[None, :], w_kv.astype(BF16), k_norm_g[None, :], b_norm_g[0][None, :],
      b_w_q[0].astype(BF16), q_norm_g[0][None, :])

    tq = ATTN_TILE
    nq = seq // tq
    qspec = pl.BlockSpec((1, N_HEADS, tq, HEAD_DIM), lambda b, i: (b, 0, i, 0))
    kvspec = pl.BlockSpec((1, N_HEADS, seq, HEAD_DIM), lambda b, i: (b, 0, 0, 0), pipeline_mode=pl.Buffered(1))
    o_p = pl.pallas_call(
        _attn_kernel,
        out_shape=jax.ShapeDtypeStruct((tp, D_MODEL), BF16),
        grid=(bsz, nq),
        in_specs=[qspec, kvspec, kvspec],
        out_specs=pl.BlockSpec((tq, D_MODEL), lambda b, i: (b * nq + i, 0)),
        scratch_shapes=[pltpu.VMEM((N_HEADS, tq, 1), F32), pltpu.VMEM((tq, D_MODEL), F32)],
        compiler_params=_cparams("arbitrary", "arbitrary"),
        name="attn_prompt",
    )(qb, kb, vb)

    ck = ATTN_TILE
    nkc = past // ck
    new_spec = pl.BlockSpec((dseq, D_MODEL), lambda b: (b, 0))
    new_rt_spec = rt_spec(dseq, lambda b: b)
    newest_spec = rt_spec(ck, lambda b: b * nkc + nkc - 1)
    cache_k2 = cache_k.reshape(dbsz * past * N_HEADS, HEAD_DIM)
    cache_v2 = cache_v.reshape(dbsz * past * N_HEADS, HEAD_DIM)
    chunk_buf = pltpu.VMEM((ck * N_HEADS, HEAD_DIM), F32)
    o_s = pl.pallas_call(
        functools.partial(_sample_attn_kernel, nkc=nkc),
        out_shape=jax.ShapeDtypeStruct((ts, D_MODEL), BF16),
        grid=(dbsz,),
        in_specs=[new_spec, new_rt_spec, new_rt_spec, newest_spec, newest_spec, any_spec, any_spec],
        out_specs=new_spec,
        scratch_shapes=[pltpu.VMEM((N_HEADS, dseq, 1), F32), pltpu.VMEM((dseq, D_MODEL), F32), chunk_buf, chunk_buf,
                        pltpu.SemaphoreType.DMA((2,))],
        compiler_params=_cparams("arbitrary"),
        name="attn_sample",
    )(q_s, k_s, v_s, cache_k2, cache_v2, cache_k2, cache_v2)
    o_s = jnp.pad(o_s, ((0, tsp - ts), (0, 0)))

    route1, cols1 = _router_operands(1, ffn_norm_g, router_g_w, router_g_b, router_e_w, router_e_b)
    pstep, sstep, route_out_specs = route_specs(tk, npt)
    h3, xs2, bucket2 = pl.pallas_call(
        functools.partial(_oproj_kernel, nps=npt),
        out_shape=route_out_shapes,
        grid=(npt + nst,),
        in_specs=[pl.BlockSpec((tk, D_MODEL), lambda i: (pstep(i), 0)),
                  pl.BlockSpec((tk, D_MODEL), lambda i: (sstep(i), 0)),
                  pl.BlockSpec((tk, D_MODEL), lambda i: (i, 0)), _const_spec((D_MODEL, D_MODEL))] + route_in_specs,
        out_specs=route_out_specs,
        compiler_params=_cparams("arbitrary"),
        name="o_proj",
    )(o_p, o_s, h2, b_w_o[0].astype(BF16), *route1)
    y2, pos2 = _moe(xs2, bucket2, 1, cols1, moe_w_gate, moe_w_up, moe_w_down)

    y_p, y_s = pl.pallas_call(
        functools.partial(_final_kernel, npt=npt),
        out_shape=(f32_rows(tp), f32_rows(tsp)),
        grid_spec=pltpu.PrefetchScalarGridSpec(
            num_scalar_prefetch=1, grid=(npt + nst,),
            in_specs=[row_spec, any_spec],
            out_specs=[prow_spec, srow_spec],
            scratch_shapes=gather_scratch),
        compiler_params=_cparams("arbitrary"),
        name="final_residual",
    )(pos2, h3, y2)

    kv_shape_p = (bsz, seq, N_HEADS, HEAD_DIM)
    kv_shape_s = (dbsz, dseq, N_HEADS, HEAD_DIM)
    k_s = k_s[:ts * SLABS].reshape(kv_shape_s)
    v_s = v_s[:ts * SLABS].reshape(kv_shape_s)
    return (y_p.reshape(bsz, seq, D_MODEL), y_s[:ts].reshape(dbsz, dseq, D_MODEL),
            k_p.reshape(kv_shape_p), v_p.reshape(kv_shape_p), k_s, v_s,
            v_rows[:ts].reshape(1, dbsz, dseq, A_WIDTH))
```

```python
import functools

import jax
import jax.numpy as jnp
from jax import lax
from jax.experimental import pallas as pl
from jax.experimental.pallas import tpu as pltpu

F32 = jnp.float32
BF16 = jnp.bfloat16
I32 = jnp.int32

D_MODEL = 1024
CHUNK = 64
GMLP_BLOCK = 128
A_WIDTH = 2 * D_MODEL
A_HEADS = 8
A_HEAD_DIM = A_WIDTH // A_HEADS
N_HEADS = 8
HEAD_DIM = D_MODEL // N_HEADS
N_GROUPS = 4
EXPERTS_PER_GROUP = 4
EXPERT_DIM = D_MODEL // 2
EPS = 1e-6

LANES = 128
SUBLANES = 8
MXU_WIDTH = 256
SLABS = D_MODEL // LANES
assert SLABS == SUBLANES and N_HEADS == SLABS and HEAD_DIM == LANES
PAIRS = ((0, 1), (0, 2), (0, 3), (1, 2), (1, 3), (2, 3))
N_BUCKETS = N_GROUPS * len(PAIRS)
ROUTER_ROWS = 32
EXPERT_ROW0 = 8

MIXER_BLOCK = 4 * GMLP_BLOCK
STREAM_BLOCK = 512
EXPERT_TILE = 256
ATTN_TILE = 256
VMEM_LIMIT = 56 * 1024 * 1024
STOP_MASS = 105.0


def _cparams(*sem):
    return pltpu.CompilerParams(dimension_semantics=sem, vmem_limit_bytes=VMEM_LIMIT)


def _rms(x, g):
    ms = jnp.mean(x * x, axis=-1, keepdims=True)
    return x * lax.rsqrt(ms + EPS) * g


def _load_rows(ref, n):
    return jnp.concatenate([ref[pl.ds(c, n, stride=SLABS), :] for c in range(SLABS)], axis=1)


def _store_rows(ref, x):
    n = x.shape[0]
    for c in range(SLABS):
        ref[pl.ds(c, n, stride=SLABS), :] = x[:, c * LANES:(c + 1) * LANES]


def _row_copy(src_hbm, src_row, buf, slot, k, sem):
    src = src_row * SLABS if isinstance(src_row, int) else pl.multiple_of(src_row * SLABS, SLABS)
    return pltpu.make_async_copy(src_hbm.at[pl.ds(src, SLABS)], buf.at[slot, pl.ds(k * SLABS, SLABS)], sem.at[slot])


def _gather_step(idx_ref, src_hbm, buf, sem, rows, n_steps=None):
    i = pl.program_id(0)
    slot = i % 2
    if n_steps is None:
        n_steps = pl.num_programs(0)

    @pl.when(i == 0)
    def _():
        for k in range(rows):
            _row_copy(src_hbm, idx_ref[k], buf, 0, k, sem).start(priority=k % 2)

    @pl.when(i + 1 < n_steps)
    def _():
        for k in range(rows):
            _row_copy(src_hbm, idx_ref[(i + 1) * rows + k], buf, 1 - slot, k, sem).start(priority=k % 2)

    @pl.when(i < n_steps)
    def _():
        for k in range(rows):
            _row_copy(src_hbm, 0, buf, slot, k, sem).wait()

    return slot


def _first_argmax(vals):
    m = vals[0]
    for v in vals[1:]:
        m = jnp.maximum(m, v)
    idx = jnp.full(m.shape, len(vals) - 1, I32)
    for r in range(len(vals) - 2, -1, -1):
        idx = jnp.where(vals[r] == m, r, idx)
    return m, idx


def _route(xn, wrt_ref, rb_ref, xs_ref, bucket_ref):
    lt = lax.dot_general(wrt_ref[...], xn, (((1,), (1,)), ((), ())),
                         precision=lax.Precision.HIGHEST, preferred_element_type=F32) + rb_ref[:, 0:1]
    _, g_idx = _first_argmax([lt[r:r + 1, :] for r in range(N_GROUPS)])
    le = []
    for e in range(EXPERTS_PER_GROUP):
        row = EXPERT_ROW0 + EXPERTS_PER_GROUP * (N_GROUPS - 1) + e
        v = lt[row:row + 1, :]
        for g in range(N_GROUPS - 2, -1, -1):
            row = EXPERT_ROW0 + EXPERTS_PER_GROUP * g + e
            v = jnp.where(g_idx == g, lt[row:row + 1, :], v)
        le.append(v)
    _, i1 = _first_argmax(le)
    _, i2 = _first_argmax([jnp.where(i1 == e, -jnp.inf, le[e]) for e in range(EXPERTS_PER_GROUP)])
    lo = jnp.minimum(i1, i2)
    hi = jnp.maximum(i1, i2)
    pair = jnp.where(lo == 0, hi - 1, jnp.where(lo == 1, hi + 1, 5))
    bucket = g_idx * len(PAIRS) + pair
    for sb in range(xn.shape[0] // LANES):
        bucket_ref[sb] = bucket[:, sb * LANES:(sb + 1) * LANES]
    _store_rows(xs_ref, xn)


def _a_layer_kernel(xp_ref, xsm_ref, ag_ref, win_ref, vg_ref, ws_ref, bs_ref, wout_ref, fg_ref, wrt_ref, rb_ref,
                    h_ref, xs_ref, bucket_ref, vrow_ref, *, nps):
    i = pl.program_id(0)
    x = jnp.where(i < nps, xp_ref[...], xsm_ref[...])
    xn = _rms(x, ag_ref[...]).astype(BF16)
    z = jax.nn.gelu(jnp.dot(xn, win_ref[...], preferred_element_type=F32))
    u = z[:, :A_WIDTH]
    v = _rms(z[:, A_WIDTH:], vg_ref[...])

    @pl.when(i >= nps)
    def _():
        vrow_ref[...] = v

    vb = v.astype(BF16)
    gate = jnp.concatenate([
        jnp.concatenate(
            [jnp.dot(ws_ref[0, h], vb[sb * GMLP_BLOCK:(sb + 1) * GMLP_BLOCK, h * A_HEAD_DIM:(h + 1) * A_HEAD_DIM],
                     preferred_element_type=F32) for h in range(A_HEADS)], axis=1) + bs_ref[0]
        for sb in range(x.shape[0] // GMLP_BLOCK)], axis=0)
    s = (u * gate).astype(BF16)
    h1 = x + jnp.dot(s, wout_ref[...], preferred_element_type=F32)
    h_ref[...] = h1
    _route(_rms(h1, fg_ref[...]), wrt_ref, rb_ref, xs_ref, bucket_ref)


def _oproj_kernel(op_ref, osm_ref, h_in_ref, wo_ref, fg_ref, wrt_ref, rb_ref, h_ref, xs_ref, bucket_ref, *, nps):
    i = pl.program_id(0)
    o = jnp.where(i < nps, op_ref[...], osm_ref[...])
    h3 = h_in_ref[...] + jnp.dot(o, wo_ref[...], preferred_element_type=F32)
    h_ref[...] = h3
    _route(_rms(h3, fg_ref[...]), wrt_ref, rb_ref, xs_ref, bucket_ref)


def _rank_kernel(b_ref, pos_ref, tb_ref, *, tm):
    bk = b_ref[...]
    nr = bk.shape[0]
    r_i = lax.broadcasted_iota(I32, (LANES, LANES), 0)
    c_i = lax.broadcasted_iota(I32, (LANES, LANES), 1)
    upper = (r_i <= c_i).astype(BF16)
    rr = lax.broadcasted_iota(I32, (nr, nr), 0)
    cc = lax.broadcasted_iota(I32, (nr, nr), 1)
    before_rows = (cc < rr).astype(BF16)
    lane = lax.broadcasted_iota(I32, (1, LANES), 1)
    tile_start = lane.astype(F32) * tm
    pos = jnp.zeros((nr, LANES), F32)
    seg_start = jnp.zeros((1, LANES), F32)
    tile_bucket = jnp.zeros((1, LANES), I32)
    for b in range(N_BUCKETS):
        m = bk == b
        pref = jnp.dot(jnp.where(m, 1.0, 0.0).astype(BF16), upper, preferred_element_type=F32)
        rowtot = jnp.broadcast_to(pref[:, LANES - 1:LANES], (nr, LANES))
        before = jnp.dot(before_rows, rowtot.astype(BF16), preferred_element_type=F32)
        cnt = jnp.sum(rowtot, axis=0, keepdims=True)
        pos = pos + jnp.where(m, seg_start + before + pref - 1.0, 0.0)
        seg_start = seg_start + jnp.ceil(cnt / tm) * tm
        tile_bucket = tile_bucket + (seg_start <= tile_start).astype(I32)
    pos_ref[...] = pos.astype(I32)
    tb_ref[...] = tile_bucket


def _build_inverse(pos_ref, inv_ref):
    n_tok = pos_ref.shape[0]
    n_rows = inv_ref.shape[0]
    batch = 8
    assert n_tok % batch == 0 and n_rows % batch == 0

    def fill(g, v):
        for j in range(batch):
            inv_ref[g * batch + j] = v + j
        return jnp.where(v + batch == n_tok, 0, v + batch)

    def put(g, carry):
        base = g * batch
        dst = [pos_ref[base + j] for j in range(batch)]
        for j in range(batch):
            inv_ref[dst[j]] = base + j
        return carry

    lax.fori_loop(0, n_rows // batch, fill, 0)
    lax.fori_loop(0, n_tok // batch, put, 0)


def _expert_kernel(telo_ref, tehi_ref, nused_ref, pos_ref, nxlo_ref, nxhi_ref, palo_ref, pahi_ref, xs_hbm, wr_ref,
                   rb_ref, wg_hbm, wu_hbm, wd_hbm, y_ref, xbuf, wup, wdown, fgate, fup, fdown, inv_ref, gsem, wsem,
                   *, tm, layer):
    r = pl.program_id(0)
    n_used = nused_ref[0]

    @pl.when(r == 0)
    def _():
        _build_inverse(pos_ref, inv_ref)

    slot = _gather_step(inv_ref, xs_hbm, xbuf, gsem, tm, n_steps=n_used)

    def weight_copies(which, expert, parity):
        s = 2 * which + parity
        return [pltpu.make_async_copy(src.at[layer, expert], dst.at[s], wsem.at[s])
                for src, dst in ((wg_hbm, fgate), (wu_hbm, fup), (wd_hbm, fdown))]

    def refresh(which, t_ref, nx_ref, pa_ref):
        prev = jnp.maximum(r - 1, 0)
        parity = pa_ref[r]

        @pl.when(r == 0)
        def _():
            for cp in weight_copies(which, t_ref[0], 0):
                cp.start()

        @pl.when(jnp.logical_or(r == 0, t_ref[r] != t_ref[prev]))
        def _():
            for cp in weight_copies(which, 0, parity):
                cp.wait()
            s = 2 * which + parity
            wup[2 * which] = fgate[s].astype(BF16)
            wup[2 * which + 1] = fup[s].astype(BF16)
            wdown[which] = fdown[s].astype(BF16)

            @pl.when(nx_ref[r] >= 0)
            def _():
                for cp in weight_copies(which, nx_ref[r], 1 - parity):
                    cp.start()

    @pl.when(r < n_used)
    def _():
        refresh(0, telo_ref, nxlo_ref, palo_ref)
        refresh(1, tehi_ref, nxhi_ref, pahi_ref)

        x = _load_rows(xbuf.at[slot], tm).astype(BF16)
        logits = jnp.dot(x, wr_ref[...], preferred_element_type=F32) + rb_ref[...]
        lane = lax.broadcasted_iota(I32, logits.shape, 1)
        grp = telo_ref[r] // EXPERTS_PER_GROUP
        is_group = lane < N_GROUPS
        m = jnp.max(jnp.where(is_group, logits, -jnp.inf), axis=1, keepdims=True)
        ex = jnp.exp(logits - m)
        p_g = (jnp.sum(jnp.where(lane == grp, ex, 0.0), axis=1, keepdims=True)
               / jnp.sum(jnp.where(is_group, ex, 0.0), axis=1, keepdims=True))
        l_lo = jnp.sum(jnp.where(lane == EXPERT_ROW0 + telo_ref[r], logits, 0.0), axis=1, keepdims=True)
        l_hi = jnp.sum(jnp.where(lane == EXPERT_ROW0 + tehi_ref[r], logits, 0.0), axis=1, keepdims=True)
        mm = jnp.maximum(l_lo, l_hi)
        e_lo = jnp.exp(l_lo - mm)
        e_hi = jnp.exp(l_hi - mm)
        gates = (p_g * (e_lo / (e_lo + e_hi)), p_g * (e_hi / (e_lo + e_hi)))

        y = None
        for which in range(2):
            a = jnp.dot(x, wup[2 * which], preferred_element_type=F32)
            b = jnp.dot(x, wup[2 * which + 1], preferred_element_type=F32)
            hid = (jax.nn.silu(a) * b * gates[which]).astype(BF16)
            part = jnp.dot(hid, wdown[which], preferred_element_type=F32)
            y = part if y is None else y + part
        _store_rows(y_ref, y)

    @pl.when(r >= n_used)
    def _():
        y_ref[...] = jnp.zeros_like(y_ref)


def _head_rms(x, g):
    return jnp.concatenate([_rms(x[:, h * HEAD_DIM:(h + 1) * HEAD_DIM], g) for h in range(N_HEADS)], axis=1)


def _kvq_kernel(pos_ref, h_in_ref, y_hbm, kvg_ref, wkv_ref, kng_ref, bg_ref, wq_ref, qng_ref,
                h_ref, kp_ref, vp_ref, ksm_ref, vsm_ref, kb_ref, vb_ref, qb_ref, qsm_ref, ybuf, sem, *, npt):
    i = pl.program_id(0)
    tk = h_in_ref.shape[0]
    h2 = h_in_ref[...] + _load_rows(ybuf.at[_gather_step(pos_ref, y_hbm, ybuf, sem, tk)], tk)
    h_ref[...] = h2
    hn = h2 * lax.rsqrt(jnp.mean(h2 * h2, axis=-1, keepdims=True) + EPS)
    kv = jnp.dot((hn * kvg_ref[...]).astype(BF16), wkv_ref[...], preferred_element_type=F32)
    q = jnp.dot((hn * bg_ref[...]).astype(BF16), wq_ref[...], preferred_element_type=F32)
    k = _head_rms(kv[:, :D_MODEL], kng_ref[...])
    v = kv[:, D_MODEL:]
    q = _head_rms(q, qng_ref[...]) * (HEAD_DIM ** -0.5)

    @pl.when(i < npt)
    def _():
        _store_rows(kp_ref, k)
        _store_rows(vp_ref, v)
        for h in range(N_HEADS):
            sl = slice(h * HEAD_DIM, (h + 1) * HEAD_DIM)
            kb_ref[0, h] = k[:, sl].astype(BF16)
            vb_ref[0, h] = v[:, sl].astype(BF16)
            qb_ref[0, h] = q[:, sl].astype(BF16)

    @pl.when(i >= npt)
    def _():
        _store_rows(ksm_ref, k)
        _store_rows(vsm_ref, v)
        qsm_ref[...] = q


def _stick_blocks(qs, ks, vs, rs, visible, later_keys):
    n = len(qs)
    zs = [lax.dot_general(qs[h], ks[h], (((1,), (1,)), ((), ())), preferred_element_type=F32) for h in range(n)]
    sps = [jnp.maximum(z, 0.0) + jnp.log(1.0 + jnp.exp(-jnp.abs(z))) for z in zs]
    if visible is not None:
        sps = [jnp.where(visible, sp, 0.0) for sp in sps]
    his = [sp.astype(BF16) for sp in sps]
    los = [(sp - hi.astype(F32)).astype(BF16) for sp, hi in zip(sps, his)]
    cs = [jnp.dot(hi, later_keys, preferred_element_type=F32) + jnp.dot(lo, later_keys, preferred_element_type=F32)
          for hi, lo in zip(his, los)]
    ws = [jnp.exp(zs[h] - cs[h] - rs[h]) for h in range(n)]
    if visible is not None:
        ws = [jnp.where(visible, w, 0.0) for w in ws]
    outs = [jnp.dot(ws[h].astype(BF16), vs[h], preferred_element_type=F32) for h in range(n)]
    return outs, [rs[h] + cs[h][:, 0:1] for h in range(n)]


def _later_keys(n):
    rowi = lax.broadcasted_iota(I32, (n, n), 0)
    coli = lax.broadcasted_iota(I32, (n, n), 1)
    return (rowi >= coli).astype(BF16)


def _min_over(rs):
    m = rs[0]
    for r in rs[1:]:
        m = jnp.minimum(m, r)
    return jnp.min(m)


def _attn_kernel(q_ref, k_ref, v_ref, o_ref, r_sc, acc_sc):
    i = pl.program_id(1)
    tq = q_ref.shape[2]
    visible = lax.broadcasted_iota(I32, (tq, tq), 1) < lax.broadcasted_iota(I32, (tq, tq), 0)
    later_keys = _later_keys(tq)
    heads = range(N_HEADS)

    def block(j, first):
        start = pl.multiple_of(j * tq, tq)
        qs = [q_ref[0, h] for h in heads]
        ks = [k_ref[0, h, pl.ds(start, tq), :] for h in heads]
        vs = [v_ref[0, h, pl.ds(start, tq), :] for h in heads]
        rs = [jnp.zeros((tq, 1), F32) if first else r_sc[h] for h in heads]
        outs, rs = _stick_blocks(qs, ks, vs, rs, visible if first else None, later_keys)
        acc = jnp.concatenate(outs, axis=1)
        if first:
            acc_sc[...] = acc
        else:
            acc_sc[...] += acc
        for h in heads:
            r_sc[h] = rs[h]
        return _min_over(rs)

    rmin = block(i, True)
    lax.while_loop(lambda st: jnp.logical_and(st[0] >= 0, st[1] <= STOP_MASS),
                   lambda st: (st[0] - 1, block(st[0], False)), (i - 1, rmin))
    o_ref[...] = acc_sc[...].astype(o_ref.dtype)


def _sample_attn_kernel(q_ref, kn_ref, vn_ref, ck_ref, cv_ref, ck_hbm, cv_hbm, o_ref, r_sc, acc_sc, kbuf, vbuf, sem,
                        *, nkc):
    b = pl.program_id(0)
    ds = q_ref.shape[0]
    ck = ck_ref.shape[0] // N_HEADS
    heads = range(N_HEADS)

    def head_rows(ref, h, n):
        return ref[pl.ds(h, n, stride=N_HEADS), :].astype(BF16)

    qs = [q_ref[:, h * HEAD_DIM:(h + 1) * HEAD_DIM].astype(BF16) for h in heads]
    visible = lax.broadcasted_iota(I32, (ds, LANES), 1) < lax.broadcasted_iota(I32, (ds, LANES), 0)
    pad = jnp.zeros((LANES - ds, HEAD_DIM), BF16)
    outs, rs = _stick_blocks(qs, [jnp.concatenate([head_rows(kn_ref, h, ds), pad], axis=0) for h in heads],
                             [jnp.concatenate([head_rows(vn_ref, h, ds), pad], axis=0) for h in heads],
                             [jnp.zeros((ds, 1), F32)] * N_HEADS, visible, _later_keys(LANES))
    acc_sc[...] = jnp.concatenate(outs, axis=1)
    for h in heads:
        r_sc[h] = rs[h]

    def chunk(kref, vref):
        outs, rs = _stick_blocks(qs, [head_rows(kref, h, ck) for h in heads], [head_rows(vref, h, ck) for h in heads],
                                 [r_sc[h] for h in heads], None, _later_keys(ck))
        acc_sc[...] += jnp.concatenate(outs, axis=1)
        for h in heads:
            r_sc[h] = rs[h]
        return _min_over(rs)

    def older(st):
        c = st[0]
        rows = ck * N_HEADS
        start = pl.multiple_of((b * nkc + c) * rows, rows)
        cpk = pltpu.make_async_copy(ck_hbm.at[pl.ds(start, rows)], kbuf, sem.at[0])
        cpv = pltpu.make_async_copy(cv_hbm.at[pl.ds(start, rows)], vbuf, sem.at[1])
        cpk.start()
        cpv.start()
        cpk.wait()
        cpv.wait()
        return c - 1, chunk(kbuf, vbuf)

    lax.while_loop(lambda st: jnp.logical_and(st[0] >= 0, st[1] <= STOP_MASS), older,
                   (nkc - 2, chunk(ck_ref, cv_ref)))
    o_ref[...] = acc_sc[...].astype(o_ref.dtype)


def _final_kernel(pos_ref, h_ref, y_hbm, yp_ref, ysm_ref, ybuf, sem, *, npt):
    i = pl.program_id(0)
    tk = h_ref.shape[0]
    y = h_ref[...] + _load_rows(ybuf.at[_gather_step(pos_ref, y_hbm, ybuf, sem, tk)], tk)

    @pl.when(i < npt)
    def _():
        yp_ref[...] = y

    @pl.when(i >= npt)
    def _():
        ysm_ref[...] = y


def _const_spec(shape):
    return pl.BlockSpec(shape, lambda *_: (0,) * len(shape))


def _router_operands(layer, ffn_norm_g, router_g_w, router_g_b, router_e_w, router_e_b):
    n_e = N_GROUPS * EXPERTS_PER_GROUP
    wrt = jnp.zeros((ROUTER_ROWS, D_MODEL), F32)
    wrt = wrt.at[:N_GROUPS].set(router_g_w[layer].T)
    wrt = wrt.at[EXPERT_ROW0:EXPERT_ROW0 + n_e].set(router_e_w[layer].transpose(0, 2, 1).reshape(n_e, D_MODEL))
    rb = jnp.zeros((ROUTER_ROWS,), F32)
    rb = rb.at[:N_GROUPS].set(router_g_b[layer])
    rb = rb.at[EXPERT_ROW0:EXPERT_ROW0 + n_e].set(router_e_b[layer].reshape(-1))
    wr_cols = jnp.zeros((D_MODEL, LANES), F32).at[:, :ROUTER_ROWS].set(wrt.T).astype(BF16)
    rb_cols = jnp.zeros((1, LANES), F32).at[0, :ROUTER_ROWS].set(rb)
    return (ffn_norm_g[layer][None, :], wrt, jnp.broadcast_to(rb[:, None], (ROUTER_ROWS, LANES))), (wr_cols, rb_cols)


def _moe(xs_rt, bucket, layer, route_cols, w_gate, w_up, w_down):
    nblk = bucket.shape[0]
    n_tok = nblk * LANES
    tm = EXPERT_TILE
    nr = -(-nblk // LANES) * LANES
    nt = -(-(n_tok + N_BUCKETS * (tm - 1)) // tm)
    assert nt <= LANES
    bk = jnp.pad(bucket.reshape(nblk, LANES), ((0, nr - nblk), (0, 0)), constant_values=N_BUCKETS)
    pos, tile_bucket = pl.pallas_call(
        functools.partial(_rank_kernel, tm=tm),
        out_shape=(jax.ShapeDtypeStruct((nr, LANES), I32), jax.ShapeDtypeStruct((1, LANES), I32)),
        name=f"moe_rank_{layer}",
    )(bk)
    pos = pos[:nblk].reshape(n_tok)
    tb = tile_bucket[0, :nt]
    n_used = jnp.sum((tb < N_BUCKETS).astype(I32))
    tsrc = jnp.minimum(jnp.arange(nt, dtype=I32), n_used - 1)
    tbc = jnp.minimum(tb[tsrc], N_BUCKETS - 1)
    pair_lo = jnp.array([p[0] for p in PAIRS], I32)
    pair_hi = jnp.array([p[1] for p in PAIRS], I32)
    grp = tbc // len(PAIRS)
    telo = grp * EXPERTS_PER_GROUP + pair_lo[tbc % len(PAIRS)]
    tehi = grp * EXPERTS_PER_GROUP + pair_hi[tbc % len(PAIRS)]

    def run_info(t):
        idx = jnp.arange(nt, dtype=I32)
        starts = jnp.concatenate([jnp.ones((1,), bool), t[1:] != t[:-1]])
        parity = (jnp.cumsum(starts.astype(I32)) - 1) % 2
        later = (idx[None, :] > idx[:, None]) & (t[None, :] != t[:, None]) & (idx[None, :] < n_used)
        nxt = jnp.where(jnp.any(later, axis=1), t[jnp.argmax(later, axis=1)], -1)
        return nxt.astype(I32), parity.astype(I32)

    nxlo, palo = run_info(telo)
    nxhi, pahi = run_info(tehi)
    up_shape = (D_MODEL, EXPERT_DIM)
    down_shape = (EXPERT_DIM, D_MODEL)
    wr_cols, rb_cols = route_cols
    tile_shape = (tm * SLABS, LANES)
    any_spec = pl.BlockSpec(memory_space=pl.ANY)
    y_sorted = pl.pallas_call(
        functools.partial(_expert_kernel, tm=tm, layer=layer),
        out_shape=jax.ShapeDtypeStruct((nt * tm * SLABS, LANES), F32),
        grid_spec=pltpu.PrefetchScalarGridSpec(
            num_scalar_prefetch=8, grid=(nt,),
            in_specs=[any_spec,
                      pl.BlockSpec((D_MODEL, LANES), lambda r, *_: (0, 0)),
                      pl.BlockSpec((1, LANES), lambda r, *_: (0, 0)),
                      any_spec, any_spec, any_spec],
            out_specs=pl.BlockSpec(tile_shape, lambda r, *_: (r, 0)),
            scratch_shapes=[pltpu.VMEM((2,) + tile_shape, F32), pltpu.VMEM((4,) + up_shape, BF16),
                            pltpu.VMEM((2,) + down_shape, BF16), pltpu.VMEM((4,) + up_shape, F32),
                            pltpu.VMEM((4,) + up_shape, F32), pltpu.VMEM((4,) + down_shape, F32),
                            pltpu.SMEM((nt * tm,), I32), pltpu.SemaphoreType.DMA((2,)),
                            pltpu.SemaphoreType.DMA((4,))]),
        compiler_params=_cparams("arbitrary"),
        name=f"moe_experts_{layer}",
    )(telo, tehi, n_used.reshape(1), pos, nxlo, nxhi, palo, pahi, xs_rt, wr_cols, rb_cols, w_gate, w_up, w_down)
    return y_sorted, pos


def kernel(x_prompt, x_sample, cache_k, cache_v, a_norm_g, a_w_in, a_v_norm_g, a_w_s, a_b_s, a_w_out, kv_norm_g, w_kv, k_norm_g, b_norm_g, b_w_q, q_norm_g, b_w_o, ffn_norm_g, router_g_w, router_g_b, router_e_w, router_e_b, moe_w_gate, moe_w_up, moe_w_down):
    bsz, seq, _ = x_prompt.shape
    dbsz, dseq, _ = x_sample.shape
    past = cache_k.shape[1]
    tp, ts = bsz * seq, dbsz * dseq
    tk = STREAM_BLOCK
    tsp = -(-ts // tk) * tk
    n_tok = tp + tsp
    tb = MIXER_BLOCK
    gb = GMLP_BLOCK
    assert seq % ATTN_TILE == 0 and tp % tk == 0 and tk % tb == 0 and tk % ATTN_TILE == 0
    assert gb % dseq == 0 and dseq <= CHUNK and past % ATTN_TILE == 0
    nps, nss = tp // tb, tsp // tb
    npt, nst = tp // tk, tsp // tk
    xp = x_prompt.reshape(tp, D_MODEL)
    xsm = jnp.pad(x_sample.reshape(ts, D_MODEL), ((0, tsp - ts), (0, 0)))

    def rt_shape(n):
        return jax.ShapeDtypeStruct((n * SLABS, LANES), F32)

    def rt_spec(rows, index):
        return pl.BlockSpec((rows * SLABS, LANES), lambda i, *_: (index(i), 0))

    def route_specs(rows, n_prompt_steps):
        def pstep(i):
            return jnp.minimum(i, n_prompt_steps - 1)

        def sstep(i):
            return jnp.maximum(i - n_prompt_steps, 0)

        out_specs = [pl.BlockSpec((rows, D_MODEL), lambda i: (i, 0)), rt_spec(rows, lambda i: i),
                     pl.BlockSpec((rows // LANES, 1, LANES), lambda i: (i, 0, 0))]
        return pstep, sstep, out_specs

    route_in_specs = [_const_spec((1, D_MODEL)), _const_spec((ROUTER_ROWS, D_MODEL)),
                      _const_spec((ROUTER_ROWS, LANES))]
    route_out_shapes = (jax.ShapeDtypeStruct((n_tok, D_MODEL), F32), rt_shape(n_tok),
                        jax.ShapeDtypeStruct((n_tok // LANES, 1, LANES), I32))
    pstep, sstep, route_out_specs = route_specs(tb, nps)

    pos_i = jnp.arange(gb)
    mask = (pos_i[None, :] // CHUNK) <= (pos_i[:, None] // CHUNK)
    w_prompt = jnp.where(mask, a_w_s[0], 0.0)
    rep = gb // dseq
    w_sample = jnp.einsum("ij,hts->hitjs", jnp.eye(rep, dtype=F32), w_prompt[:, :dseq, :dseq]).reshape(
        A_HEADS, gb, gb)
    ws_all = jnp.stack([w_prompt, w_sample]).astype(BF16)
    b_prompt = a_b_s[0]
    b_sample = jnp.tile(a_b_s[0][:, :dseq], (1, rep))
    bs_all = jnp.stack([jnp.repeat(b.T, A_HEAD_DIM, axis=1) for b in (b_prompt, b_sample)])
    route0, cols0 = _router_operands(0, ffn_norm_g, router_g_w, router_g_b, router_e_w, router_e_b)
    h1, xs1, bucket1, v_rows = pl.pallas_call(
        functools.partial(_a_layer_kernel, nps=nps),
        out_shape=route_out_shapes + (jax.ShapeDtypeStruct((tsp, A_WIDTH), F32),),
        grid=(nps + nss,),
        in_specs=[pl.BlockSpec((tb, D_MODEL), lambda i: (pstep(i), 0)),
                  pl.BlockSpec((tb, D_MODEL), lambda i: (sstep(i), 0)),
                  _const_spec((1, D_MODEL)), _const_spec((D_MODEL, 2 * A_WIDTH)), _const_spec((1, A_WIDTH)),
                  pl.BlockSpec((1, A_HEADS, gb, gb), lambda i: (i // nps, 0, 0, 0)),
                  pl.BlockSpec((1, gb, A_WIDTH), lambda i: (i // nps, 0, 0)),
                  _const_spec((A_WIDTH, D_MODEL))] + route_in_specs,
        out_specs=route_out_specs + [pl.BlockSpec((tb, A_WIDTH), lambda i: (sstep(i), 0))],
        compiler_params=_cparams("arbitrary"),
        name="a_layer",
    )(xp, xsm, a_norm_g[0][None, :], a_w_in[0].astype(BF16), a_v_norm_g[0][None, :], ws_all, bs_all,
      a_w_out[0].astype(BF16), *route0)
    y1, pos1 = _moe(xs1, bucket1, 0, cols0, moe_w_gate, moe_w_up, moe_w_down)

    spt = seq // tk

    def ptile(i):
        return jnp.minimum(i, npt - 1)

    def stile(i):
        return jnp.maximum(i - npt, 0)

    row_spec = pl.BlockSpec((tk, D_MODEL), lambda i, *_: (i, 0))
    prow_spec = pl.BlockSpec((tk, D_MODEL), lambda i, *_: (ptile(i), 0))
    srow_spec = pl.BlockSpec((tk, D_MODEL), lambda i, *_: (stile(i), 0))
    head_spec = pl.BlockSpec((1, N_HEADS, tk, HEAD_DIM), lambda i, *_: (ptile(i) // spt, 0, ptile(i) % spt, 0))
    any_spec = pl.BlockSpec(memory_space=pl.ANY)
    gather_scratch = [pltpu.VMEM((2, tk * SLABS, LANES), F32), pltpu.SemaphoreType.DMA((2,))]
    f32_rows = lambda n: jax.ShapeDtypeStruct((n, D_MODEL), F32)
    head_major = jax.ShapeDtypeStruct((bsz, N_HEADS, seq, HEAD_DIM), BF16)
    h2, k_p, v_p, k_s, v_s, kb, vb, qb, q_s = pl.pallas_call(
        functools.partial(_kvq_kernel, npt=npt),
        out_shape=(f32_rows(n_tok), rt_shape(tp), rt_shape(tp), rt_shape(tsp), rt_shape(tsp),
                   head_major, head_major, head_major, f32_rows(tsp)),
        grid_spec=pltpu.PrefetchScalarGridSpec(
            num_scalar_prefetch=1, grid=(npt + nst,),
            in_specs=[row_spec, any_spec, _const_spec((1, D_MODEL)),
                      _const_spec((D_MODEL, 2 * D_MODEL)), _const_spec((1, HEAD_DIM)), _const_spec((1, D_MODEL)),
                      _const_spec((D_MODEL, D_MODEL)), _const_spec((1, HEAD_DIM))],
            out_specs=[row_spec, rt_spec(tk, ptile), rt_spec(tk, ptile), rt_spec(tk, stile), rt_spec(tk, stile),
                       head_spec, head_spec, head_spec, srow_spec],
            scratch_shapes=gather_scratch),
        compiler_params=_cparams("arbitrary"),
        name="kvq",
    )(pos1, h1, y1, kv_norm_g[None, :], w_kv.astype(BF16), k_norm_g[None, :], b_norm_g[0][None, :],
      b_w_q[0].astype(BF16), q_norm_g[0][None, :])

    tq = ATTN_TILE
    nq = seq // tq
    qspec = pl.BlockSpec((1, N_HEADS, tq, HEAD_DIM), lambda b, i: (b, 0, i, 0))
    kvspec = pl.BlockSpec((1, N_HEADS, seq, HEAD_DIM), lambda b, i: (b, 0, 0, 0), pipeline_mode=pl.Buffered(1))
    o_p = pl.pallas_call(
        _attn_kernel,
        out_shape=jax.ShapeDtypeStruct((tp, D_MODEL), BF16),
        grid=(bsz, nq),
        in_specs=[qspec, kvspec, kvspec],
        out_specs=pl.BlockSpec((tq, D_MODEL), lambda b, i: (b * nq + i, 0)),
        scratch_shapes=[pltpu.VMEM((N_HEADS, tq, 1), F32), pltpu.VMEM((tq, D_MODEL), F32)],
        compiler_params=_cparams("arbitrary", "arbitrary"),
        name="attn_prompt",
    )(qb, kb, vb)

    ck = ATTN_TILE
    nkc = past // ck
    new_spec = pl.BlockSpec((dseq, D_MODEL), lambda b: (b, 0))
    new_rt_spec = rt_spec(dseq, lambda b: b)
    newest_spec = rt_spec(ck, lambda b: b * nkc + nkc - 1)
    cache_k2 = cache_k.reshape(dbsz * past * N_HEADS, HEAD_DIM)
    cache_v2 = cache_v.reshape(dbsz * past * N_HEADS, HEAD_DIM)
    chunk_buf = pltpu.VMEM((ck * N_HEADS, HEAD_DIM), F32)
    o_s = pl.pallas_call(
        functools.partial(_sample_attn_kernel, nkc=nkc),
        out_shape=jax.ShapeDtypeStruct((ts, D_MODEL), BF16),
        grid=(dbsz,),
        in_specs=[new_spec, new_rt_spec, new_rt_spec, newest_spec, newest_spec, any_spec, any_spec],
        out_specs=new_spec,
        scratch_shapes=[pltpu.VMEM((N_HEADS, dseq, 1), F32), pltpu.VMEM((dseq, D_MODEL), F32), chunk_buf, chunk_buf,
                        pltpu.SemaphoreType.DMA((2,))],
        compiler_params=_cparams("arbitrary"),
        name="attn_sample",
    )(q_s, k_s, v_s, cache_k2, cache_v2, cache_k2, cache_v2)
    o_s = jnp.pad(o_s, ((0, tsp - ts), (0, 0)))

    route1, cols1 = _router_operands(1, ffn_norm_g, router_g_w, router_g_b, router_e_w, router_e_b)
    pstep, sstep, route_out_specs = route_specs(tk, npt)
    h3, xs2, bucket2 = pl.pallas_call(
        functools.partial(_oproj_kernel, nps=npt),
        out_shape=route_out_shapes,
        grid=(npt + nst,),
        in_specs=[pl.BlockSpec((tk, D_MODEL), lambda i: (pstep(i), 0)),
                  pl.BlockSpec((tk, D_MODEL), lambda i: (sstep(i), 0)),
                  pl.BlockSpec((tk, D_MODEL), lambda i: (i, 0)), _const_spec((D_MODEL, D_MODEL))] + route_in_specs,
        out_specs=route_out_specs,
        compiler_params=_cparams("arbitrary"),
        name="o_proj",
    )(o_p, o_s, h2, b_w_o[0].astype(BF16), *route1)
    y2, pos2 = _moe(xs2, bucket2, 1, cols1, moe_w_gate, moe_w_up, moe_w_down)

    y_p, y_s = pl.pallas_call(
        functools.partial(_final_kernel, npt=npt),
        out_shape=(f32_rows(tp), f32_rows(tsp)),
        grid_spec=pltpu.PrefetchScalarGridSpec(
            num_scalar_prefetch=1, grid=(npt + nst,),
            in_specs=[row_spec, any_spec],
            out_specs=[prow_spec, srow_spec],
            scratch_shapes=gather_scratch),
        compiler_params=_cparams("arbitrary"),
        name="final_residual",
    )(pos2, h3, y2)

    kv_shape_p = (bsz, seq, N_HEADS, HEAD_DIM)
    kv_shape_s = (dbsz, dseq, N_HEADS, HEAD_DIM)
    k_s = k_s[:ts * SLABS].reshape(kv_shape_s)
    v_s = v_s[:ts * SLABS].reshape(kv_shape_s)
    return (y_p.reshape(bsz, seq, D_MODEL), y_s[:ts].reshape(dbsz, dseq, D_MODEL),
            k_p.reshape(kv_shape_p), v_p.reshape(kv_shape_p), k_s, v_s,
            v_rows[:ts].reshape(1, dbsz, dseq, A_WIDTH))
```
